```python
import math
import jax, jax.numpy as jnp
from jax import lax
import numpy as np

D_MODEL = 1024
BATCH = 16
SEQ = 2048
DEPTH = 2
DEC_BATCH = 8
DEC_SEQ = 8192
PAST_LEN = 128

HEAD_DIM = 64
DIL_GROUPS = ((128, 1), (512, 4), (2048, 16))
HEADS_PER_GROUP = 4
N_HEADS_A = HEADS_PER_GROUP * len(DIL_GROUPS)
N_HEADS_B = 8
N_HEADS = N_HEADS_A + N_HEADS_B
WIDTH_A = HEADS_PER_GROUP * HEAD_DIM
WIDTH_B = N_HEADS_B * HEAD_DIM
QKV_WIDTH = 3 * N_HEADS * HEAD_DIM
BLK_A = 64
N_BUCKETS = 32
MAX_DISTANCE = 1024
GRID_W = 64
KH_MAX = 8
KW = 16
QRB = 2
QCB = 16
KCB = 32
D_FF = 4 * D_MODEL
N_MOD = 6
EPS = 1e-6
NEG = -1e30

kernel_name = 'dilated_natten_gated_hybrid_encoder'


def rms_norm(x, g):
    xf = x.astype(jnp.float32)
    y = xf * lax.rsqrt(jnp.mean(xf * xf, axis=-1, keepdims=True) + EPS)
    return (y * g.astype(jnp.float32)).astype(x.dtype)


def t5_bucket(rel):
    nb = N_BUCKETS // 2
    max_exact = nb // 2
    ret = (rel > 0).astype(np.int32) * nb
    n = np.abs(rel)
    large = max_exact + (np.log(np.maximum(n, 1) / max_exact) / np.log(MAX_DISTANCE / max_exact) * (nb - max_exact)).astype(np.int32)
    large = np.minimum(large, nb - 1)
    return (ret + np.where(n < max_exact, n, large)).astype(np.int32)


def dilated_group(q, k, v, table, d, half):
    B_, L, H, E = q.shape
    n = L // d
    nb = -(-n // BLK_A)
    n_pad = nb * BLK_A

    def sub(t):
        return t.reshape(B_, n, d, H, E).transpose(0, 2, 1, 3, 4)

    qs = jnp.pad(sub(q), ((0, 0), (0, 0), (0, n_pad - n), (0, 0), (0, 0))).reshape(B_, d, nb, BLK_A, H, E)

    def windows(t):
        tp = jnp.pad(sub(t), ((0, 0), (0, 0), (BLK_A, n_pad - n + BLK_A), (0, 0), (0, 0)))
        tp = tp.reshape(B_, d, nb + 2, BLK_A, H, E)
        return jnp.concatenate([tp[:, :, :-2], tp[:, :, 1:-1], tp[:, :, 2:]], axis=3)

    kw, vw = windows(k), windows(v)
    j = np.arange(3 * BLK_A)[None, :] - BLK_A - np.arange(BLK_A)[:, None]
    bias = jnp.transpose(table[t5_bucket(d * j)], (2, 0, 1)).astype(jnp.float32)
    key_idx = np.arange(nb)[:, None] * BLK_A + np.arange(3 * BLK_A)[None, :] - BLK_A
    mask = (np.abs(j) <= half)[None] & ((key_idx >= 0) & (key_idx < n))[:, None, :]
    s = jnp.einsum('bdiqhe,bdikhe->bdihqk', qs, kw, preferred_element_type=jnp.float32) * (HEAD_DIM ** -0.5) + bias
    s = jnp.where(mask[:, None], s, NEG)
    lse = jax.nn.logsumexp(s, axis=-1)
    p = jnp.exp(s - lse[..., None]).astype(v.dtype)
    o = jnp.einsum('bdihqk,bdikhe->bdiqhe', p, vw).reshape(B_, d, n_pad, H, E)[:, :, :n]
    o = o.transpose(0, 2, 1, 3, 4).reshape(B_, L, H, E)
    lse = lse.transpose(0, 1, 2, 4, 3).reshape(B_, d, n_pad, H)[:, :, :n]
    lse = lse.transpose(0, 2, 1, 3).reshape(B_, L, H)
    return o, lse


def dilated_mixer(q, k, v, table):
    B_, L, _, E = q.shape
    outs, lses = [], []
    for g, (w, d) in enumerate(DIL_GROUPS):
        sl = slice(g * HEADS_PER_GROUP, (g + 1) * HEADS_PER_GROUP)
        o, l = dilated_group(q[:, :, sl], k[:, :, sl], v[:, :, sl], table[:, sl], d, w // (2 * d))
        outs.append(o)
        lses.append(l)
    wts = jax.nn.softmax(jnp.stack(lses, axis=0), axis=0)
    o = jnp.sum(wts[..., None].astype(q.dtype) * jnp.stack(outs, axis=0), axis=0)
    return o.reshape(B_, L, WIDTH_A)


def neighborhood_mixer(q, k, v, rpb):
    B_, L, H, E = q.shape
    rows = L // GRID_W
    kh = min(KH_MAX, rows)
    krb = min(kh + 2, rows)
    n_rb = rows // QRB
    n_cb = GRID_W // QCB
    q_rows = np.arange(n_rb)[:, None] * QRB + np.arange(QRB)[None, :]
    r_start = np.clip(q_rows - kh // 2, 0, rows - kh)
    band = np.clip(np.arange(n_rb) * QRB - kh // 2, 0, rows - krb)
    k_rows = band[:, None] + np.arange(krb)[None, :]
    row_ok = (k_rows[:, None, :] >= r_start[:, :, None]) & (k_rows[:, None, :] < r_start[:, :, None] + kh)
    dr = np.clip(k_rows[:, None, :] - q_rows[:, :, None], 1 - KH_MAX, KH_MAX - 1) + KH_MAX - 1
    q_cols = np.arange(n_cb)[:, None] * QCB + np.arange(QCB)[None, :]
    c_start = np.clip(q_cols - KW // 2, 0, GRID_W - KW)
    k_cols = np.clip(np.arange(n_cb) * QCB - KW // 2, 0, GRID_W - KCB)[:, None] + np.arange(KCB)[None, :]
    col_ok = (k_cols[:, None, :] >= c_start[:, :, None]) & (k_cols[:, None, :] < c_start[:, :, None] + KW)
    dc = np.clip(k_cols[:, None, :] - q_cols[:, :, None], 1 - KW, KW - 1) + KW - 1
    qg = q.reshape(B_, rows, GRID_W, H, E)
    kg = k.reshape(B_, rows, GRID_W, H, E)
    vg = v.reshape(B_, rows, GRID_W, H, E)
    scale = HEAD_DIM ** -0.5

    def block(args):
        a, b0, rok, dri = args
        qb = lax.dynamic_slice_in_dim(qg, a * QRB, QRB, axis=1).reshape(B_, QRB, n_cb, QCB, H, E)
        kb = lax.dynamic_slice_in_dim(kg, b0, krb, axis=1)[:, :, k_cols]
        vb = lax.dynamic_slice_in_dim(vg, b0, krb, axis=1)[:, :, k_cols]
        s = jnp.einsum('bqnchd,bknjhd->bnhqckj', qb, kb, preferred_element_type=jnp.float32) * scale
        bias = rpb[:, dri[None, :, None, :, None], dc[:, None, :, None, :]]
        ok = rok[None, :, None, :, None] & col_ok[:, None, :, None, :]
        s = jnp.where(ok[:, None], s + jnp.moveaxis(bias, 0, 1).astype(jnp.float32), NEG)
        p = jax.nn.softmax(s.reshape(s.shape[:-2] + (krb * KCB,)), axis=-1).reshape(s.shape).astype(v.dtype)
        o = jnp.einsum('bnhqckj,bknjhd->bqnchd', p, vb)
        return o.reshape(B_, QRB, GRID_W, H, E)

    xs = (jnp.arange(n_rb, dtype=jnp.int32), jnp.asarray(band, dtype=jnp.int32), jnp.asarray(row_ok), jnp.asarray(dr, dtype=jnp.int32))
    out = lax.map(block, xs)
    return out.transpose(1, 0, 2, 3, 4, 5).reshape(B_, L, H * E)


def layer(x, c, norm1_g, norm2_g, w_mod, b_mod, w_in, q_norm_g, k_norm_g, rel_bias, rpb, w_gate, b_gate, w_up_a, w_up_b, w_o, w_ff1, w_ff2):
    B_, L, _ = x.shape
    mod = (jax.nn.silu(c) @ w_mod + b_mod)[:, None, :]
    sh1, sc1, g1, sh2, sc2, g2 = jnp.split(mod, N_MOD, axis=-1)
    h = rms_norm(x, norm1_g) * (1 + sc1) + sh1
    qkv = (h @ w_in).reshape(B_, L, 3, N_HEADS, HEAD_DIM)
    q = rms_norm(qkv[:, :, 0], q_norm_g)
    k = rms_norm(qkv[:, :, 1], k_norm_g)
    v = qkv[:, :, 2]
    o_a = dilated_mixer(q[:, :, :N_HEADS_A], k[:, :, :N_HEADS_A], v[:, :, :N_HEADS_A], rel_bias)
    o_b = neighborhood_mixer(q[:, :, N_HEADS_A:], k[:, :, N_HEADS_A:], v[:, :, N_HEADS_A:], rpb)
    gate_a, gate_b = jnp.split(jax.nn.sigmoid(h @ w_gate + b_gate), 2, axis=-1)
    mixed = gate_a * (o_a @ w_up_a) + gate_b * (o_b @ w_up_b)
    x = x + g1 * (mixed @ w_o)
    h2 = rms_norm(x, norm2_g) * (1 + sc2) + sh2
    f = jnp.square(jax.nn.relu(h2 @ w_ff1)) @ w_ff2
    return x + g2 * f


def setup_inputs(seed: int = 0) -> dict:
    key = jax.random.key(seed)
    ks = jax.random.split(key, 20)

    def nrm(k, shape, s):
        return jax.random.normal(k, shape, jnp.float32) * s

    return {
        'x_prompt': nrm(ks[0], (BATCH, SEQ, D_MODEL), 1.0),
        'x_sample': nrm(ks[1], (DEC_BATCH, DEC_SEQ, D_MODEL), 1.0),
        'c_prompt': nrm(ks[2], (BATCH, D_MODEL), 1.0),
        'c_sample': nrm(ks[3], (DEC_BATCH, D_MODEL), 1.0),
        'norm1_g': 1.0 + nrm(ks[4], (DEPTH, D_MODEL), 0.05),
        'norm2_g': 1.0 + nrm(ks[5], (DEPTH, D_MODEL), 0.05),
        'w_mod': nrm(ks[6], (DEPTH, D_MODEL, N_MOD * D_MODEL), 0.5 * D_MODEL ** -0.5),
        'b_mod': nrm(ks[7], (DEPTH, N_MOD * D_MODEL), 0.02),
        'w_in': nrm(ks[8], (DEPTH, D_MODEL, QKV_WIDTH), D_MODEL ** -0.5),
        'q_norm_g': 1.0 + nrm(ks[9], (DEPTH, N_HEADS, HEAD_DIM), 0.05),
        'k_norm_g': 1.0 + nrm(ks[10], (DEPTH, N_HEADS, HEAD_DIM), 0.05),
        'rel_bias': nrm(ks[11], (N_BUCKETS, N_HEADS_A), 0.5),
        'rpb': nrm(ks[12], (DEPTH, N_HEADS_B, 2 * KH_MAX - 1, 2 * KW - 1), 0.5),
        'w_gate': nrm(ks[13], (DEPTH, D_MODEL, 2 * D_MODEL), D_MODEL ** -0.5),
        'b_gate': nrm(ks[14], (DEPTH, 2 * D_MODEL), 0.02),
        'w_up_a': nrm(ks[15], (DEPTH, WIDTH_A, D_MODEL), WIDTH_A ** -0.5),
        'w_up_b': nrm(ks[16], (DEPTH, WIDTH_B, D_MODEL), WIDTH_B ** -0.5),
        'w_o': nrm(ks[17], (DEPTH, D_MODEL, D_MODEL), D_MODEL ** -0.5),
        'w_ff1': nrm(ks[18], (DEPTH, D_MODEL, D_FF), D_MODEL ** -0.5),
        'w_ff2': nrm(ks[19], (DEPTH, D_FF, D_MODEL), D_FF ** -0.5),
    }


def reference(x_prompt, x_sample, c_prompt, c_sample, norm1_g, norm2_g, w_mod, b_mod, w_in, q_norm_g, k_norm_g, rel_bias, rpb, w_gate, b_gate, w_up_a, w_up_b, w_o, w_ff1, w_ff2):
    y_prompt = x_prompt
    y_sample = x_sample
    for l in range(DEPTH):
        y_prompt = layer(y_prompt, c_prompt, norm1_g[l], norm2_g[l], w_mod[l], b_mod[l], w_in[l], q_norm_g[l], k_norm_g[l], rel_bias, rpb[l], w_gate[l], b_gate[l], w_up_a[l], w_up_b[l], w_o[l], w_ff1[l], w_ff2[l])
        y_sample = layer(y_sample, c_sample, norm1_g[l], norm2_g[l], w_mod[l], b_mod[l], w_in[l], q_norm_g[l], k_norm_g[l], rel_bias, rpb[l], w_gate[l], b_gate[l], w_up_a[l], w_up_b[l], w_o[l], w_ff1[l], w_ff2[l])
    return (y_prompt, y_sample)
```

```python
import functools

import jax
import jax.numpy as jnp
import numpy as np
from jax import lax
from jax.experimental import pallas as pl
from jax.experimental.pallas import tpu as pltpu

D_MODEL = 1024
DEPTH = 2
HEAD_DIM = 64
DIL_GROUPS = ((128, 1), (512, 4), (2048, 16))
HEADS_PER_GROUP = 4
N_GROUPS = len(DIL_GROUPS)
N_HEADS_A = HEADS_PER_GROUP * N_GROUPS
N_HEADS_B = 8
N_HEADS = N_HEADS_A + N_HEADS_B
WIDTH_A = HEADS_PER_GROUP * HEAD_DIM
WIDTH_B = N_HEADS_B * HEAD_DIM
QKV_PART = N_HEADS * HEAD_DIM
N_BUCKETS = 32
MAX_DISTANCE = 1024
GRID_W = 64
KH_MAX = 8
KW = 16
D_FF = 4 * D_MODEL
N_MOD = 6
EPS = 1e-6
NEG = -1e30

LANE_TILE = 256
QB = 128
HALO_A = 64
QROWS_B = QB // GRID_W
KROWS_B = KH_MAX + QROWS_B
KB_B = KROWS_B * GRID_W
N_VAR_B = 5
TOK_TILE_PRE = 512
TOK_TILE_POST = 256
FF_CHUNK = 1024
VMEM_LIMIT = 56 * 1024 * 1024

F32 = jnp.float32
BF16 = jnp.bfloat16


def _rms(x, g):
    return x * lax.rsqrt(jnp.mean(x * x, axis=-1, keepdims=True) + EPS) * g


def _mod_kernel(c_ref, w_ref, b_ref, o_ref):
    c = c_ref[...]
    s = c * jax.nn.sigmoid(c)
    o_ref[...] = jnp.dot(s, w_ref[...], preferred_element_type=F32) + b_ref[...]


def _modulation(c_all, w_mod, b_mod):
    nb = c_all.shape[0]
    ncol = N_MOD * D_MODEL
    blk = 1536
    return pl.pallas_call(
        _mod_kernel,
        grid=(DEPTH, ncol // blk),
        in_specs=[
            pl.BlockSpec((nb, D_MODEL), lambda l, j: (0, 0)),
            pl.BlockSpec((None, D_MODEL, blk), lambda l, j: (l, 0, j)),
            pl.BlockSpec((None, 1, blk), lambda l, j: (l, 0, j)),
        ],
        out_specs=pl.BlockSpec((None, nb, blk), lambda l, j: (l, 0, j)),
        out_shape=jax.ShapeDtypeStruct((DEPTH, nb, ncol), F32),
        compiler_params=pltpu.CompilerParams(vmem_limit_bytes=VMEM_LIMIT),
        name="modulation",
    )(c_all, w_mod, b_mod.reshape(DEPTH, 1, ncol))


def _pre_kernel(x_ref, mod_ref, n1_ref, win_ref, qg_ref, kg_ref, ones_ref,
                qa_ref, ka_ref, va_ref, qb_ref, kb_ref, vb_ref):
    x = x_ref[...]
    sh1 = mod_ref[0:1, :]
    sc1 = mod_ref[1:2, :]
    h = (_rms(x, n1_ref[...]) * (1.0 + sc1) + sh1).astype(BF16)
    ones = ones_ref[...]
    n_chunk = QKV_PART // LANE_TILE
    n_chunk_a = (N_HEADS_A * HEAD_DIM) // LANE_TILE

    def store(ref_a, ref_b, c, val):
        if c < n_chunk_a:
            ref_a[:, c * LANE_TILE:(c + 1) * LANE_TILE] = val.astype(ref_a.dtype)
        else:
            cb = c - n_chunk_a
            ref_b[:, cb * LANE_TILE:(cb + 1) * LANE_TILE] = val.astype(ref_b.dtype)

    for part, (g_ref, ref_a, ref_b) in enumerate(
            ((qg_ref, qa_ref, qb_ref), (kg_ref, ka_ref, kb_ref), (None, va_ref, vb_ref))):
        for c in range(n_chunk):
            col = part * QKV_PART + c * LANE_TILE
            y = jnp.dot(h, win_ref[:, col:col + LANE_TILE], preferred_element_type=F32)
            if g_ref is not None:
                ss = jnp.dot((y * y).astype(BF16), ones, preferred_element_type=F32)
                y = y * lax.rsqrt(ss * (1.0 / HEAD_DIM) + EPS) * g_ref[:, c * LANE_TILE:(c + 1) * LANE_TILE]
            store(ref_a, ref_b, c, y)


def _pre_attention(x, mod, n1, w_in, qg, kg, ones):
    b_, l_, _ = x.shape
    t = TOK_TILE_PRE
    per_b = l_ // t
    tok = lambda w: pl.BlockSpec((None, t, w), lambda i: (i // per_b, i % per_b, 0))
    const = lambda shape: pl.BlockSpec(shape, lambda i: (0,) * len(shape), pipeline_mode=pl.Buffered(1))
    wa = N_HEADS_A * HEAD_DIM
    out_shape = [jax.ShapeDtypeStruct((b_, l_, wa), BF16)] * 3 + [jax.ShapeDtypeStruct((b_, l_, WIDTH_B), BF16)] * 3
    outs = pl.pallas_call(
        _pre_kernel,
        grid=(b_ * per_b,),
        in_specs=[
            tok(D_MODEL),
            pl.BlockSpec((None, N_MOD, D_MODEL), lambda i: (i // per_b, 0, 0)),
            const((1, D_MODEL)),
            const((D_MODEL, 3 * QKV_PART)),
            const((1, QKV_PART)),
            const((1, QKV_PART)),
            const((LANE_TILE, LANE_TILE)),
        ],
        out_specs=[tok(wa)] * 3 + [tok(WIDTH_B)] * 3,
        out_shape=out_shape,
        compiler_params=pltpu.CompilerParams(vmem_limit_bytes=VMEM_LIMIT),
        name="pre_attention",
    )(x, mod, n1, w_in, qg, kg, ones)
    qa, ka, va, qb, kb, vb = outs
    return qa, ka, va, qb, kb, vb


def _attn_core(q, k, v, bias):
    nq = q.shape[0]
    head_of_lane = lax.broadcasted_iota(jnp.int32, (nq, LANE_TILE), 1) // HEAD_DIM
    zero = jnp.zeros_like(q)
    qs = jnp.concatenate([jnp.where(head_of_lane == h, q, zero) for h in range(HEADS_PER_GROUP)], axis=0)
    s = lax.dot_general(qs, k, (((1,), (1,)), ((), ())), preferred_element_type=F32) + bias
    m = jnp.max(s, axis=-1, keepdims=True)
    p = jnp.exp(s - m)
    l = jnp.sum(p, axis=-1, keepdims=True)
    o4 = jnp.dot(p.astype(BF16), v, preferred_element_type=F32)
    o4 = o4 * (1.0 / l)
    lse4 = m + jnp.log(l)
    o = jnp.zeros((nq, LANE_TILE), F32)
    lse = jnp.zeros((nq, LANE_TILE), F32)
    for h in range(HEADS_PER_GROUP):
        sel = head_of_lane == h
        o = jnp.where(sel, o4[h * nq:(h + 1) * nq, :], o)
        lse = jnp.where(sel, lse4[h * nq:(h + 1) * nq, :], lse)
    return o, lse


def _attn_a_kernel(q_ref, k_ref, v_ref, bias_ref, o_ref, lse_ref, *, n, chunk, kb):
    i = pl.program_id(2)
    blocks_per_chunk = chunk // QB
    last_block = n // QB - 1

    def body(j, carry):
        blk = i * blocks_per_chunk + j
        q0 = blk * QB
        start = pl.multiple_of(jnp.clip(q0 - HALO_A, 0, n - kb), HALO_A)
        var = jnp.where(blk == 0, 0, jnp.where(blk == last_block, 2, 1))
        rows = pl.ds(pl.multiple_of(j * QB, QB), QB)
        o, lse = _attn_core(q_ref[rows, :], k_ref[pl.ds(start, kb), :], v_ref[pl.ds(start, kb), :], bias_ref[var])
        o_ref[rows, :] = o.astype(o_ref.dtype)
        lse_ref[rows, :] = lse
        return carry

    lax.fori_loop(0, blocks_per_chunk, body, 0)


def _dilated_group(qa, ka, va, bias, g, d):
    b_, l_, wa = qa.shape
    n = l_ // d
    kb = min(2 * QB, n)
    chunk = min(n, 1024)
    assert n % QB == 0 and n % chunk == 0 and (n == QB or n >= 2 * QB)
    view = lambda a: a.reshape(b_, n, d * wa)
    qspec = pl.BlockSpec((None, chunk, LANE_TILE), lambda b, r, i: (b, i, r * N_GROUPS + g))
    kvspec = pl.BlockSpec((None, n, LANE_TILE), lambda b, r, i: (b, 0, r * N_GROUPS + g))
    ospec = pl.BlockSpec((None, chunk, LANE_TILE), lambda b, r, i: (b, i, r))
    o, lse = pl.pallas_call(
        functools.partial(_attn_a_kernel, n=n, chunk=chunk, kb=kb),
        grid=(b_, d, n // chunk),
        in_specs=[qspec, kvspec, kvspec, pl.BlockSpec(bias.shape, lambda b, r, i: (0, 0, 0))],
        out_specs=[ospec, ospec],
        out_shape=[jax.ShapeDtypeStruct((b_, n, d * LANE_TILE), BF16),
                   jax.ShapeDtypeStruct((b_, n, d * LANE_TILE), F32)],
        compiler_params=pltpu.CompilerParams(vmem_limit_bytes=VMEM_LIMIT),
        name=f"dilated_attention_d{d}",
    )(view(qa), view(ka), view(va), bias)
    return o.reshape(b_, l_, LANE_TILE), lse.reshape(b_, l_, LANE_TILE)


def _t5_bucket(rel):
    nb = N_BUCKETS // 2
    max_exact = nb // 2
    ret = (rel > 0).astype(np.int32) * nb
    n = np.abs(rel)
    large = max_exact + (np.log(np.maximum(n, 1) / max_exact) / np.log(MAX_DISTANCE / max_exact) * (nb - max_exact)).astype(np.int32)
    large = np.minimum(large, nb - 1)
    return (ret + np.where(n < max_exact, n, large)).astype(np.int32)


def _bias_a(table, g, w, d, n):
    half = w // (2 * d)
    assert half == HALO_A
    kb = min(2 * QB, n)
    sl = slice(g * HEADS_PER_GROUP, (g + 1) * HEADS_PER_GROUP)
    out = []
    for off in (0, -HALO_A, QB - kb):
        rel = np.arange(kb)[None, :] + off - np.arange(QB)[:, None]
        bias = jnp.transpose(table[_t5_bucket(d * rel)][:, :, sl], (2, 0, 1)).astype(F32)
        bias = jnp.where((np.abs(rel) <= half)[None], bias, NEG)
        out.append(bias.reshape(HEADS_PER_GROUP * QB, kb))
    return jnp.stack(out, axis=0)


def _attn_b_kernel(q_ref, k_ref, v_ref, bias_ref, o_ref, *, rows):
    i = pl.program_id(2)
    band = jnp.clip(i * QROWS_B - KH_MAX // 2, 0, rows - KROWS_B)
    start = pl.multiple_of(band * GRID_W, GRID_W)
    o, _ = _attn_core(q_ref[...], k_ref[pl.ds(start, KB_B), :], v_ref[pl.ds(start, KB_B), :], bias_ref[...])
    o_ref[...] = o.astype(o_ref.dtype)


def _variant_b(i, n_rb):
    return jnp.where(i < 2, i, jnp.where(i >= n_rb - 2, i - (n_rb - N_VAR_B), 2))


def _neighborhood(qb, kb, vb, bias):
    b_, l_, _ = qb.shape
    rows = l_ // GRID_W
    n_rb = rows // QROWS_B
    halves = WIDTH_B // LANE_TILE
    qspec = pl.BlockSpec((None, QB, LANE_TILE), lambda b, hf, i: (b, i, hf))
    kvspec = pl.BlockSpec((None, l_, LANE_TILE), lambda b, hf, i: (b, 0, hf))
    bspec = pl.BlockSpec((None, None, HEADS_PER_GROUP * QB, KB_B), lambda b, hf, i: (_variant_b(i, n_rb), hf, 0, 0))
    return pl.pallas_call(
        functools.partial(_attn_b_kernel, rows=rows),
        grid=(b_, halves, n_rb),
        in_specs=[qspec, kvspec, kvspec, bspec],
        out_specs=qspec,
        out_shape=jax.ShapeDtypeStruct((b_, l_, WIDTH_B), BF16),
        compiler_params=pltpu.CompilerParams(vmem_limit_bytes=VMEM_LIMIT),
        name="neighborhood_attention",
    )(qb, kb, vb, bias)


def _bias_b(rpb, rows):
    assert rows >= KROWS_B and rows % QROWS_B == 0
    n_rb = rows // QROWS_B
    assert n_rb >= N_VAR_B
    kh = KH_MAX

    def placement(i):
        q_rows = i * QROWS_B + np.arange(QROWS_B)
        band = int(np.clip(i * QROWS_B - kh // 2, 0, rows - KROWS_B))
        r_start = np.clip(q_rows - kh // 2, 0, rows - kh)
        k_rows = band + np.arange(KROWS_B)
        row_ok = (k_rows[None, :] >= r_start[:, None]) & (k_rows[None, :] < r_start[:, None] + kh)
        dr = np.clip(k_rows[None, :] - q_rows[:, None], 1 - KH_MAX, KH_MAX - 1) + KH_MAX - 1
        return row_ok, dr

    reps = [0, 1, 2, n_rb - 2, n_rb - 1]
    var_of = np.where(np.arange(n_rb) < 2, np.arange(n_rb),
                      np.where(np.arange(n_rb) >= n_rb - 2, np.arange(n_rb) - (n_rb - N_VAR_B), 2))
    for i in range(n_rb):
        a, b = placement(i), placement(reps[var_of[i]])
        assert np.array_equal(a[0], b[0]) and np.array_equal(np.where(a[0], a[1], 0), np.where(b[0], b[1], 0))

    q_cols = np.arange(GRID_W)
    k_cols = np.arange(GRID_W)
    c_start = np.clip(q_cols - KW // 2, 0, GRID_W - KW)
    col_ok = (k_cols[None, :] >= c_start[:, None]) & (k_cols[None, :] < c_start[:, None] + KW)
    dc = np.clip(k_cols[None, :] - q_cols[:, None], 1 - KW, KW - 1) + KW - 1
    out = []
    for i in reps:
        row_ok, dr = placement(i)
        bias = rpb[:, dr[:, None, :, None], dc[None, :, None, :]]
        ok = row_ok[:, None, :, None] & col_ok[None, :, None, :]
        bias = jnp.where(ok[None], bias.astype(F32), NEG)
        out.append(bias.reshape(WIDTH_B // LANE_TILE, HEADS_PER_GROUP * QB, KB_B))
    return jnp.stack(out, axis=0)


def _post_kernel(x_ref, mod_ref, n1_ref, n2_ref, oa0_ref, oa1_ref, oa2_ref, l0_ref, l1_ref, l2_ref, ob_ref,
                 wg_ref, bg_ref, wua_ref, wub_ref, wo_ref, w1_ref, w2_ref, out_ref):
    x = x_ref[...]
    sh1, sc1, g1 = mod_ref[0:1, :], mod_ref[1:2, :], mod_ref[2:3, :]
    sh2, sc2, g2 = mod_ref[3:4, :], mod_ref[4:5, :], mod_ref[5:6, :]
    h = (_rms(x, n1_ref[...]) * (1.0 + sc1) + sh1).astype(BF16)
    gate = jax.nn.sigmoid(jnp.dot(h, wg_ref[...], preferred_element_type=F32) + bg_ref[...])

    l0, l1, l2 = l0_ref[...], l1_ref[...], l2_ref[...]
    mx = jnp.maximum(jnp.maximum(l0, l1), l2)
    e0, e1, e2 = jnp.exp(l0 - mx), jnp.exp(l1 - mx), jnp.exp(l2 - mx)
    inv = 1.0 / (e0 + e1 + e2)
    oa = ((e0 * inv) * oa0_ref[...].astype(F32) + (e1 * inv) * oa1_ref[...].astype(F32)
          + (e2 * inv) * oa2_ref[...].astype(F32))

    ua = jnp.dot(oa.astype(BF16), wua_ref[...], preferred_element_type=F32)
    ub = jnp.dot(ob_ref[...], wub_ref[...], preferred_element_type=F32)
    mixed = gate[:, :D_MODEL] * ua + gate[:, D_MODEL:] * ub
    x1 = x + g1 * jnp.dot(mixed.astype(BF16), wo_ref[...], preferred_element_type=F32)

    h2 = (_rms(x1, n2_ref[...]) * (1.0 + sc2) + sh2).astype(BF16)
    acc = jnp.zeros_like(x1)
    for c in range(D_FF // FF_CHUNK):
        f = jnp.dot(h2, w1_ref[:, c * FF_CHUNK:(c + 1) * FF_CHUNK], preferred_element_type=F32)
        f = jnp.square(jnp.maximum(f, 0.0)).astype(BF16)
        acc = acc + jnp.dot(f, w2_ref[c * FF_CHUNK:(c + 1) * FF_CHUNK, :], preferred_element_type=F32)
    out_ref[...] = x1 + g2 * acc


def _post_attention(x, mod, n1, n2, oas, lses, ob, wg, bg, wua, wub, wo, w1, w2):
    b_, l_, _ = x.shape
    t = TOK_TILE_POST
    per_b = l_ // t
    tok = lambda w: pl.BlockSpec((None, t, w), lambda i: (i // per_b, i % per_b, 0))
    const = lambda shape: pl.BlockSpec(shape, lambda i: (0,) * len(shape), pipeline_mode=pl.Buffered(1))
    return pl.pallas_call(
        _post_kernel,
        grid=(b_ * per_b,),
        in_specs=[
            tok(D_MODEL),
            pl.BlockSpec((None, N_MOD, D_MODEL), lambda i: (i // per_b, 0, 0)),
            const((1, D_MODEL)), const((1, D_MODEL)),
            tok(LANE_TILE), tok(LANE_TILE), tok(LANE_TILE),
            tok(LANE_TILE), tok(LANE_TILE), tok(LANE_TILE),
            tok(WIDTH_B),
            const((D_MODEL, 2 * D_MODEL)), const((1, 2 * D_MODEL)),
            const((WIDTH_A, D_MODEL)), const((WIDTH_B, D_MODEL)), const((D_MODEL, D_MODEL)),
            const((D_MODEL, D_FF)), const((D_FF, D_MODEL)),
        ],
        out_specs=tok(D_MODEL),
        out_shape=jax.ShapeDtypeStruct(x.shape, F32),
        compiler_params=pltpu.CompilerParams(vmem_limit_bytes=VMEM_LIMIT),
        name="post_attention_mlp",
    )(x, mod, n1, n2, *oas, *lses, ob, wg, bg, wua, wub, wo, w1, w2)


def _head_sum_matrix():
    head = np.arange(LANE_TILE) // HEAD_DIM
    return jnp.asarray((head[:, None] == head[None, :]).astype(np.float32), dtype=BF16)


def kernel(x_prompt, x_sample, c_prompt, c_sample, norm1_g, norm2_g, w_mod, b_mod, w_in, q_norm_g, k_norm_g,
           rel_bias, rpb, w_gate, b_gate, w_up_a, w_up_b, w_o, w_ff1, w_ff2):
    n_prompt = c_prompt.shape[0]
    mod_all = _modulation(jnp.concatenate([c_prompt, c_sample], axis=0), w_mod, b_mod)
    mod_all = mod_all.reshape(DEPTH, -1, N_MOD, D_MODEL)
    ones = _head_sum_matrix()
    scale = HEAD_DIM ** -0.5
    qg = (q_norm_g.reshape(DEPTH, 1, QKV_PART) * scale).astype(F32)
    kg = k_norm_g.reshape(DEPTH, 1, QKV_PART).astype(F32)
    w_in_b, w_gate_b = w_in.astype(BF16), w_gate.astype(BF16)
    w_up_a_b, w_up_b_b, w_o_b = w_up_a.astype(BF16), w_up_b.astype(BF16), w_o.astype(BF16)
    w_ff1_b, w_ff2_b = w_ff1.astype(BF16), w_ff2.astype(BF16)

    ys = []
    for x, mod_x in ((x_prompt, mod_all[:, :n_prompt]), (x_sample, mod_all[:, n_prompt:])):
        l_ = x.shape[1]
        bias_a = [_bias_a(rel_bias, g, w, d, l_ // d) for g, (w, d) in enumerate(DIL_GROUPS)]
        for l in range(DEPTH):
            n1 = norm1_g[l].reshape(1, D_MODEL)
            n2 = norm2_g[l].reshape(1, D_MODEL)
            qa, ka, va, qb, kb, vb = _pre_attention(x, mod_x[l], n1, w_in_b[l], qg[l], kg[l], ones)
            oas, lses = [], []
            for g, (w, d) in enumerate(DIL_GROUPS):
                o, lse = _dilated_group(qa, ka, va, bias_a[g], g, d)
                oas.append(o)
                lses.append(lse)
            ob = _neighborhood(qb, kb, vb, _bias_b(rpb[l], l_ // GRID_W))
            x = _post_attention(x, mod_x[l], n1, n2, oas, lses, ob, w_gate_b[l], b_gate[l].reshape(1, -1),
                                w_up_a_b[l], w_up_b_b[l], w_o_b[l], w_ff1_b[l], w_ff2_b[l])
        ys.append(x)
    return tuple(ys)
```

```python
import functools

import jax
import jax.numpy as jnp
import numpy as np
from jax import lax
from jax.experimental import pallas as pl
from jax.experimental.pallas import tpu as pltpu

D_MODEL = 1024
DEPTH = 2
HEAD_DIM = 64
DIL_GROUPS = ((128, 1), (512, 4), (2048, 16))
HEADS_PER_GROUP = 4
N_GROUPS = len(DIL_GROUPS)
N_HEADS_A = HEADS_PER_GROUP * N_GROUPS
N_HEADS_B = 8
N_HEADS = N_HEADS_A + N_HEADS_B
WIDTH_A = HEADS_PER_GROUP * HEAD_DIM
WIDTH_B = N_HEADS_B * HEAD_DIM
QKV_PART = N_HEADS * HEAD_DIM
N_BUCKETS = 32
MAX_DISTANCE = 1024
GRID_W = 64
KH_MAX = 8
KW = 16
D_FF = 4 * D_MODEL
N_MOD = 6
EPS = 1e-6
NEG = -1e30

LANE_TILE = 256
QB = 128
HALO_A = 64
QROWS_B = QB // GRID_W
KROWS_B = KH_MAX + QROWS_B
KB_B = KROWS_B * GRID_W
N_VAR_B = 5
TOK_TILE_PRE = 512
TOK_TILE_POST = 256
FF_CHUNK = 1024
BF16_SUBLANES = 16
VREG_LANES = 128
LANE_HALVES = (slice(0, VREG_LANES), slice(VREG_LANES, LANE_TILE))
VMEM_LIMIT = 56 * 1024 * 1024

F32 = jnp.float32
BF16 = jnp.bfloat16


def _rms(x, g):
    return x * lax.rsqrt(jnp.mean(x * x, axis=-1, keepdims=True) + EPS) * g


def _const_spec(shape):
    return pl.BlockSpec(shape, lambda *_: (0,) * len(shape), pipeline_mode=pl.Buffered(1))


def _layer_spec(shape, layer):
    return pl.BlockSpec((None,) + shape, lambda *_: (layer,) + (0,) * len(shape), pipeline_mode=pl.Buffered(1))


def _mod_kernel(c_ref, w_ref, b_ref, o_ref):
    c = c_ref[...]
    s = c * jax.nn.sigmoid(c)
    o_ref[...] = jnp.dot(s, w_ref[...], preferred_element_type=F32) + b_ref[...]


def _modulation(c_all, w_mod, b_mod):
    nb = c_all.shape[0]
    ncol = N_MOD * D_MODEL
    blk = 1536
    return pl.pallas_call(
        _mod_kernel,
        grid=(DEPTH, ncol // blk),
        in_specs=[
            pl.BlockSpec((nb, D_MODEL), lambda l, j: (0, 0)),
            pl.BlockSpec((None, D_MODEL, blk), lambda l, j: (l, 0, j)),
            pl.BlockSpec((None, 1, blk), lambda l, j: (l, 0, j)),
        ],
        out_specs=pl.BlockSpec((None, nb, blk), lambda l, j: (l, 0, j)),
        out_shape=jax.ShapeDtypeStruct((DEPTH, nb, ncol), F32),
        compiler_params=pltpu.CompilerParams(vmem_limit_bytes=VMEM_LIMIT),
        name="modulation",
    )(c_all, w_mod, b_mod.reshape(DEPTH, 1, ncol))


def _pre_kernel(x_ref, mod_ref, n1_ref, win_ref, qg_ref, kg_ref, ones_ref, *refs):
    scr_refs = refs[-2:]
    outs = refs[:-2]
    t = x_ref.shape[0]
    x = x_ref[...]
    sh1 = mod_ref[0:1, :]
    sc1 = mod_ref[1:2, :]
    h = (_rms(x, n1_ref[...]) * (1.0 + sc1) + sh1).astype(BF16)
    ones = ones_ref[...]
    n_chunk = QKV_PART // LANE_TILE
    per_part = N_GROUPS + 1

    for part, g_ref in enumerate((qg_ref, kg_ref, None)):
        for c in range(n_chunk):
            col = part * QKV_PART + c * LANE_TILE
            y = jnp.dot(h, win_ref[:, col:col + LANE_TILE], preferred_element_type=F32)
            if g_ref is not None:
                ss = jnp.dot((y * y).astype(BF16), ones, preferred_element_type=F32)
                y = y * lax.rsqrt(ss * (1.0 / HEAD_DIM) + EPS) * g_ref[:, c * LANE_TILE:(c + 1) * LANE_TILE]
            if c < N_GROUPS:
                out = outs[part * per_part + c]
                d = DIL_GROUPS[c][1]
                if d == 1:
                    out[0] = y.astype(out.dtype)
                else:
                    for s_ref, lanes in zip(scr_refs, LANE_HALVES):
                        s_ref[...] = y[:, lanes]
                    for r in range(d):
                        for s_ref, lanes in zip(scr_refs, LANE_HALVES):
                            out[r, :, lanes] = s_ref[pl.ds(r, t // d, stride=d), :].astype(out.dtype)
            else:
                out = outs[part * per_part + N_GROUPS]
                cb = c - N_GROUPS
                out[:, cb * LANE_TILE:(cb + 1) * LANE_TILE] = y.astype(out.dtype)


def _pre_attention(x, mod, n1, w_in, qg, kg, ones, layer):
    b_, l_, _ = x.shape
    t = TOK_TILE_PRE
    per_b = l_ // t
    tok = lambda w: pl.BlockSpec((None, t, w), lambda i: (i // per_b, i % per_b, 0))
    out_specs, out_shape = [], []
    for _ in range(3):
        for _, d in DIL_GROUPS:
            assert t % (BF16_SUBLANES * d) == 0
            out_specs.append(pl.BlockSpec((None, d, t // d, LANE_TILE), lambda i: (i // per_b, 0, i % per_b, 0)))
            out_shape.append(jax.ShapeDtypeStruct((b_, d, l_ // d, LANE_TILE), BF16))
        out_specs.append(tok(WIDTH_B))
        out_shape.append(jax.ShapeDtypeStruct((b_, l_, WIDTH_B), BF16))
    outs = pl.pallas_call(
        _pre_kernel,
        grid=(b_ * per_b,),
        in_specs=[
            tok(D_MODEL),
            pl.BlockSpec((None, None, N_MOD, D_MODEL), lambda i: (layer, i // per_b, 0, 0)),
            _layer_spec((1, D_MODEL), layer),
            _layer_spec((D_MODEL, 3 * QKV_PART), layer),
            _layer_spec((1, QKV_PART), layer),
            _layer_spec((1, QKV_PART), layer),
            _const_spec((LANE_TILE, LANE_TILE)),
        ],
        out_specs=out_specs,
        out_shape=out_shape,
        scratch_shapes=[pltpu.VMEM((t, VREG_LANES), F32)] * 2,
        compiler_params=pltpu.CompilerParams(vmem_limit_bytes=VMEM_LIMIT),
        name="pre_attention",
    )(x, mod, n1, w_in, qg, kg, ones)
    per_part = N_GROUPS + 1
    return [outs[p * per_part:(p + 1) * per_part] for p in range(3)]


def _attn_core(q, k, v, bias):
    nq = q.shape[0]
    head_of_lane = lax.broadcasted_iota(jnp.int32, (nq, LANE_TILE), 1) // HEAD_DIM
    zero = jnp.zeros_like(q)
    qs = jnp.concatenate([jnp.where(head_of_lane == h, q, zero) for h in range(HEADS_PER_GROUP)], axis=0)
    s = lax.dot_general(qs, k, (((1,), (1,)), ((), ())), preferred_element_type=F32) + bias
    m = jnp.max(s, axis=-1, keepdims=True)
    p = jnp.exp(s - m)
    l = jnp.sum(p, axis=-1, keepdims=True)
    o4 = jnp.dot(p.astype(BF16), v, preferred_element_type=F32)
    o4 = o4 * (1.0 / l)
    lse4 = m + jnp.log(l)
    o = jnp.zeros((nq, LANE_TILE), F32)
    lse = jnp.zeros((nq, LANE_TILE), F32)
    for h in range(HEADS_PER_GROUP):
        sel = head_of_lane == h
        o = jnp.where(sel, o4[h * nq:(h + 1) * nq, :], o)
        lse = jnp.where(sel, lse4[h * nq:(h + 1) * nq, :], lse)
    return o, lse


def _toeplitz(f, nq, nk):
    p = nq + nk - 1
    assert f.shape[-1] == p and nq >= 2
    u = jnp.roll(f, -(nq - 1), axis=-1)
    flat = jnp.tile(u, nq)[..., :nq * (p - 1)]
    return flat.reshape(f.shape[:-1] + (nq, p - 1))[..., :nk]


def _attn_a_kernel(q_ref, k_ref, v_ref, bias_ref, o_ref, lse_ref, *, n, chunk, kb):
    i = pl.program_id(2)
    blocks_per_chunk = chunk // QB
    last_block = n // QB - 1

    def body(j, carry):
        blk = i * blocks_per_chunk + j
        q0 = blk * QB
        start = pl.multiple_of(jnp.clip(q0 - HALO_A, 0, n - kb), HALO_A)
        var = jnp.where(blk == 0, 0, jnp.where(blk == last_block, 2, 1))
        rows = pl.ds(pl.multiple_of(j * QB, QB), QB)
        o, lse = _attn_core(q_ref[rows, :], k_ref[pl.ds(start, kb), :], v_ref[pl.ds(start, kb), :], bias_ref[var])
        o_ref[rows, :] = o.astype(o_ref.dtype)
        lse_ref[rows, :] = lse
        return carry

    lax.fori_loop(0, blocks_per_chunk, body, 0)


def _dilated_group(q, k, v, bias, d):
    b_, d_, n, _ = q.shape
    assert d_ == d
    kb = bias.shape[-1]
    chunk = min(n, 1024)
    assert n % QB == 0 and n % chunk == 0 and kb == min(2 * QB, n) and (n == QB or n >= 2 * QB)
    qspec = pl.BlockSpec((None, None, chunk, LANE_TILE), lambda b, r, i: (b, r, i, 0))
    kvspec = pl.BlockSpec((None, None, n, LANE_TILE), lambda b, r, i: (b, r, 0, 0))
    return pl.pallas_call(
        functools.partial(_attn_a_kernel, n=n, chunk=chunk, kb=kb),
        grid=(b_, d, n // chunk),
        in_specs=[qspec, kvspec, kvspec, pl.BlockSpec(bias.shape, lambda b, r, i: (0, 0, 0))],
        out_specs=[qspec, qspec],
        out_shape=[jax.ShapeDtypeStruct(q.shape, BF16), jax.ShapeDtypeStruct(q.shape, F32)],
        compiler_params=pltpu.CompilerParams(vmem_limit_bytes=VMEM_LIMIT),
        name=f"dilated_attention_d{d}",
    )(q, k, v, bias)


def _t5_bucket(rel):
    nb = N_BUCKETS // 2
    max_exact = nb // 2
    ret = (rel > 0).astype(np.int32) * nb
    n = np.abs(rel)
    large = max_exact + (np.log(np.maximum(n, 1) / max_exact) / np.log(MAX_DISTANCE / max_exact) * (nb - max_exact)).astype(np.int32)
    large = np.minimum(large, nb - 1)
    return (ret + np.where(n < max_exact, n, large)).astype(np.int32)


def _bias_a(table, g, w, d, n):
    half = w // (2 * d)
    assert half == HALO_A
    kb = min(2 * QB, n)
    sl = slice(g * HEADS_PER_GROUP, (g + 1) * HEADS_PER_GROUP)
    out = []
    for off in (0, -HALO_A, QB - kb):
        rel = np.arange(QB + kb - 1) - (QB - 1) + off
        f = jnp.where((np.abs(rel) <= half)[None, :], table[_t5_bucket(d * rel)][:, sl].T.astype(F32), NEG)
        out.append(_toeplitz(f, QB, kb).reshape(HEADS_PER_GROUP * QB, kb))
    return jnp.stack(out, axis=0)


def _attn_b_kernel(q_ref, k_ref, v_ref, bias_ref, o_ref, *, rows):
    i = pl.program_id(2)
    band = jnp.clip(i * QROWS_B - KH_MAX // 2, 0, rows - KROWS_B)
    start = pl.multiple_of(band * GRID_W, GRID_W)
    o, _ = _attn_core(q_ref[...], k_ref[pl.ds(start, KB_B), :], v_ref[pl.ds(start, KB_B), :], bias_ref[...])
    o_ref[...] = o.astype(o_ref.dtype)


def _variant_b(i, n_rb):
    return jnp.where(i < 2, i, jnp.where(i >= n_rb - 2, i - (n_rb - N_VAR_B), 2))


def _neighborhood(qb, kb, vb, bias):
    b_, l_, _ = qb.shape
    rows = l_ // GRID_W
    n_rb = rows // QROWS_B
    halves = WIDTH_B // LANE_TILE
    qspec = pl.BlockSpec((None, QB, LANE_TILE), lambda b, hf, i: (b, i, hf))
    kvspec = pl.BlockSpec((None, l_, LANE_TILE), lambda b, hf, i: (b, 0, hf))
    bspec = pl.BlockSpec((None, None, HEADS_PER_GROUP * QB, KB_B), lambda b, hf, i: (_variant_b(i, n_rb), hf, 0, 0))
    return pl.pallas_call(
        functools.partial(_attn_b_kernel, rows=rows),
        grid=(b_, halves, n_rb),
        in_specs=[qspec, kvspec, kvspec, bspec],
        out_specs=qspec,
        out_shape=jax.ShapeDtypeStruct((b_, l_, WIDTH_B), BF16),
        compiler_params=pltpu.CompilerParams(vmem_limit_bytes=VMEM_LIMIT),
        name="neighborhood_attention",
    )(qb, kb, vb, bias)


def _placement_b(i, rows):
    q_rows = i * QROWS_B + np.arange(QROWS_B)
    band = int(np.clip(i * QROWS_B - KH_MAX // 2, 0, rows - KROWS_B))
    r_start = np.clip(q_rows - KH_MAX // 2, 0, rows - KH_MAX)
    k_rows = band + np.arange(KROWS_B)
    row_ok = (k_rows[None, :] >= r_start[:, None]) & (k_rows[None, :] < r_start[:, None] + KH_MAX)
    dr = np.clip(k_rows[None, :] - q_rows[:, None], 1 - KH_MAX, KH_MAX - 1) + KH_MAX - 1
    return row_ok, np.where(row_ok, dr, 0)


def _bias_b(rpb, rows_list):
    variants = None
    for rows in rows_list:
        assert rows >= KROWS_B and rows % QROWS_B == 0
        n_rb = rows // QROWS_B
        assert n_rb >= N_VAR_B
        idx = np.arange(n_rb)
        var_of = np.where(idx < 2, idx, np.where(idx >= n_rb - 2, idx - (n_rb - N_VAR_B), 2))
        reps = [_placement_b(i, rows) for i in (0, 1, 2, n_rb - 2, n_rb - 1)]
        for i in range(n_rb):
            a, b = _placement_b(i, rows), reps[var_of[i]]
            assert np.array_equal(a[0], b[0]) and np.array_equal(a[1], b[1])
        if variants is not None:
            assert all(np.array_equal(a[0], b[0]) and np.array_equal(a[1], b[1]) for a, b in zip(variants, reps))
        variants = reps

    cols = np.arange(GRID_W)
    c_start = np.clip(cols - KW // 2, 0, GRID_W - KW)
    col_ok = (cols[None, :] >= c_start[:, None]) & (cols[None, :] < c_start[:, None] + KW)
    assert np.all(np.abs(cols[None, :] - cols[:, None])[col_ok] <= KW - 1)
    pad = GRID_W - KW
    toep = _toeplitz(jnp.pad(rpb.astype(F32), ((0, 0), (0, 0), (pad, pad))), GRID_W, GRID_W)
    toep = jnp.where(col_ok[None, None], toep, NEG)
    masked = jnp.full((N_HEADS_B, GRID_W, GRID_W), NEG, F32)
    out = []
    for row_ok, dr in variants:
        blocks = [jnp.stack([toep[:, dr[a, kr]] if row_ok[a, kr] else masked for kr in range(KROWS_B)], axis=2)
                  for a in range(QROWS_B)]
        bias = jnp.stack(blocks, axis=1)
        out.append(bias.reshape(WIDTH_B // LANE_TILE, HEADS_PER_GROUP * QB, KB_B))
    return jnp.stack(out, axis=0)


def _post_kernel(x_ref, mod_ref, n1_ref, n2_ref, oa0_ref, oa1_ref, oa2_ref, l0_ref, l1_ref, l2_ref, ob_ref,
                 wg_ref, bg_ref, wua_ref, wub_ref, wo_ref, w1_ref, w2_ref, out_ref, *scr):
    t = x_ref.shape[0]
    x = x_ref[...]
    sh1, sc1, g1 = mod_ref[0:1, :], mod_ref[1:2, :], mod_ref[2:3, :]
    sh2, sc2, g2 = mod_ref[3:4, :], mod_ref[4:5, :], mod_ref[5:6, :]
    h = (_rms(x, n1_ref[...]) * (1.0 + sc1) + sh1).astype(BF16)
    gate = jax.nn.sigmoid(jnp.dot(h, wg_ref[...], preferred_element_type=F32) + bg_ref[...])

    def token_order(ref, scr_refs):
        d = ref.shape[0]
        if d == 1:
            return ref[0].astype(F32)
        for r in range(d):
            for s_ref, lanes in zip(scr_refs, LANE_HALVES):
                s_ref[pl.ds(r, t // d, stride=d), :] = ref[r, :, lanes].astype(F32)
        return jnp.concatenate([s_ref[...] for s_ref in scr_refs], axis=-1)

    oa0, oa1, oa2 = token_order(oa0_ref, None), token_order(oa1_ref, scr[0:2]), token_order(oa2_ref, scr[2:4])
    l0, l1, l2 = token_order(l0_ref, None), token_order(l1_ref, scr[4:6]), token_order(l2_ref, scr[6:8])
    mx = jnp.maximum(jnp.maximum(l0, l1), l2)
    e0, e1, e2 = jnp.exp(l0 - mx), jnp.exp(l1 - mx), jnp.exp(l2 - mx)
    inv = 1.0 / (e0 + e1 + e2)
    oa = (e0 * inv) * oa0 + (e1 * inv) * oa1 + (e2 * inv) * oa2

    ua = jnp.dot(oa.astype(BF16), wua_ref[...], preferred_element_type=F32)
    ub = jnp.dot(ob_ref[...], wub_ref[...], preferred_element_type=F32)
    mixed = gate[:, :D_MODEL] * ua + gate[:, D_MODEL:] * ub
    x1 = x + g1 * jnp.dot(mixed.astype(BF16), wo_ref[...], preferred_element_type=F32)

    h2 = (_rms(x1, n2_ref[...]) * (1.0 + sc2) + sh2).astype(BF16)
    acc = jnp.zeros_like(x1)
    for c in range(D_FF // FF_CHUNK):
        f = jnp.dot(h2, w1_ref[:, c * FF_CHUNK:(c + 1) * FF_CHUNK], preferred_element_type=F32)
        f = jnp.square(jnp.maximum(f, 0.0)).astype(BF16)
        acc = acc + jnp.dot(f, w2_ref[c * FF_CHUNK:(c + 1) * FF_CHUNK, :], preferred_element_type=F32)
    out_ref[...] = x1 + g2 * acc


def _post_attention(x, mod, n1, n2, oas, lses, ob, wg, bg, wua, wub, wo, w1, w2, layer):
    b_, l_, _ = x.shape
    t = TOK_TILE_POST
    per_b = l_ // t
    tok = lambda w: pl.BlockSpec((None, t, w), lambda i: (i // per_b, i % per_b, 0))
    dil_specs = []
    for _, d in DIL_GROUPS:
        assert t % (BF16_SUBLANES * d) == 0
        dil_specs.append(pl.BlockSpec((None, d, t // d, LANE_TILE), lambda i: (i // per_b, 0, i % per_b, 0)))
    return pl.pallas_call(
        _post_kernel,
        grid=(b_ * per_b,),
        in_specs=[
            tok(D_MODEL),
            pl.BlockSpec((None, None, N_MOD, D_MODEL), lambda i: (layer, i // per_b, 0, 0)),
            _layer_spec((1, D_MODEL), layer), _layer_spec((1, D_MODEL), layer),
            *dil_specs, *dil_specs,
            tok(WIDTH_B),
            _layer_spec((D_MODEL, 2 * D_MODEL), layer), _layer_spec((1, 2 * D_MODEL), layer),
            _layer_spec((WIDTH_A, D_MODEL), layer), _layer_spec((WIDTH_B, D_MODEL), layer),
            _layer_spec((D_MODEL, D_MODEL), layer),
            _layer_spec((D_MODEL, D_FF), layer), _layer_spec((D_FF, D_MODEL), layer),
        ],
        out_specs=tok(D_MODEL),
        out_shape=jax.ShapeDtypeStruct(x.shape, F32),
        scratch_shapes=[pltpu.VMEM((t, VREG_LANES), F32)] * 8,
        compiler_params=pltpu.CompilerParams(vmem_limit_bytes=VMEM_LIMIT),
        name="post_attention_mlp",
    )(x, mod, n1, n2, *oas, *lses, ob, wg, bg, wua, wub, wo, w1, w2)


def _head_sum_matrix():
    head = np.arange(LANE_TILE) // HEAD_DIM
    return jnp.asarray((head[:, None] == head[None, :]).astype(np.float32), dtype=BF16)


def kernel(x_prompt, x_sample, c_prompt, c_sample, norm1_g, norm2_g, w_mod, b_mod, w_in, q_norm_g, k_norm_g,
           rel_bias, rpb, w_gate, b_gate, w_up_a, w_up_b, w_o, w_ff1, w_ff2):
    n_prompt = c_prompt.shape[0]
    mod_all = _modulation(jnp.concatenate([c_prompt, c_sample], axis=0), w_mod, b_mod)
    mod_all = mod_all.reshape(DEPTH, -1, N_MOD, D_MODEL)
    ones = _head_sum_matrix()
    scale = HEAD_DIM ** -0.5
    qg = (q_norm_g.reshape(DEPTH, 1, QKV_PART) * scale).astype(F32)
    kg = k_norm_g.reshape(DEPTH, 1, QKV_PART).astype(F32)
    n1 = norm1_g.reshape(DEPTH, 1, D_MODEL)
    n2 = norm2_g.reshape(DEPTH, 1, D_MODEL)
    bg = b_gate.reshape(DEPTH, 1, 2 * D_MODEL)
    w_in_b, w_gate_b = w_in.astype(BF16), w_gate.astype(BF16)
    w_up_a_b, w_up_b_b, w_o_b = w_up_a.astype(BF16), w_up_b.astype(BF16), w_o.astype(BF16)
    w_ff1_b, w_ff2_b = w_ff1.astype(BF16), w_ff2.astype(BF16)

    xs = (x_prompt, x_sample)
    bias_b = [_bias_b(rpb[l], [x.shape[1] // GRID_W for x in xs]) for l in range(DEPTH)]
    bias_a = {}
    for x in xs:
        for g, (w, d) in enumerate(DIL_GROUPS):
            n = x.shape[1] // d
            if (g, min(2 * QB, n)) not in bias_a:
                bias_a[(g, min(2 * QB, n))] = _bias_a(rel_bias, g, w, d, n)

    ys = []
    for x, mod_x in ((x_prompt, mod_all[:, :n_prompt]), (x_sample, mod_all[:, n_prompt:])):
        l_ = x.shape[1]
        for l in range(DEPTH):
            q, k, v = _pre_attention(x, mod_x, n1, w_in_b, qg, kg, ones, l)
            oas, lses = [], []
            for g, (w, d) in enumerate(DIL_GROUPS):
                o, lse = _dilated_group(q[g], k[g], v[g], bias_a[(g, min(2 * QB, l_ // d))], d)
                oas.append(o)
                lses.append(lse)
            ob = _neighborhood(q[N_GROUPS], k[N_GROUPS], v[N_GROUPS], bias_b[l])
            x = _post_attention(x, mod_x, n1, n2, oas, lses, ob, w_gate_b, bg, w_up_a_b, w_up_b_b, w_o_b,
                                w_ff1_b, w_ff2_b, l)
        ys.append(x)
    return tuple(ys)
```

```python
import functools

import jax
import jax.numpy as jnp
import numpy as np
from jax import lax
from jax.experimental import pallas as pl
from jax.experimental.pallas import tpu as pltpu

D_MODEL = 1024
DEPTH = 2
HEAD_DIM = 64
DIL_GROUPS = ((128, 1), (512, 4), (2048, 16))
HEADS_PER_GROUP = 4
N_GROUPS = len(DIL_GROUPS)
N_HEADS_A = HEADS_PER_GROUP * N_GROUPS
N_HEADS_B = 8
N_HEADS = N_HEADS_A + N_HEADS_B
WIDTH_A = HEADS_PER_GROUP * HEAD_DIM
WIDTH_B = N_HEADS_B * HEAD_DIM
QKV_PART = N_HEADS * HEAD_DIM
N_BUCKETS = 32
MAX_DISTANCE = 1024
GRID_W = 64
KH_MAX = 8
KW = 16
D_FF = 4 * D_MODEL
N_MOD = 6
EPS = 1e-6
NEG = -1e30

LANE_TILE = 256
QB = 128
HALO_A = 64
KB_B = KH_MAX * GRID_W
ROWS_PER_STEP_B = 16
MAX_CHUNK_A = 1024
INTERLEAVE = 4
TOK_TILE_PRE = 512
TOK_TILE_POST = 256
FF_CHUNK = 1024
BF16_SUBLANES = 16
VREG_LANES = 128
LANE_HALVES = (slice(0, VREG_LANES), slice(VREG_LANES, LANE_TILE))
VMEM_LIMIT = 56 * 1024 * 1024

F32 = jnp.float32
BF16 = jnp.bfloat16


def _rms(x, g):
    return x * lax.rsqrt(jnp.mean(x * x, axis=-1, keepdims=True) + EPS) * g


def _const_spec(shape):
    return pl.BlockSpec(shape, lambda *_: (0,) * len(shape), pipeline_mode=pl.Buffered(1))


def _layer_spec(shape, layer):
    return pl.BlockSpec((None,) + shape, lambda *_: (layer,) + (0,) * len(shape), pipeline_mode=pl.Buffered(1))


def _mod_kernel(c_ref, w_ref, b_ref, o_ref):
    c = c_ref[...]
    s = c * jax.nn.sigmoid(c)
    o_ref[...] = jnp.dot(s, w_ref[...], preferred_element_type=F32) + b_ref[...]


def _modulation(c_all, w_mod, b_mod):
    nb = c_all.shape[0]
    ncol = N_MOD * D_MODEL
    blk = 1536
    return pl.pallas_call(
        _mod_kernel,
        grid=(DEPTH, ncol // blk),
        in_specs=[
            pl.BlockSpec((nb, D_MODEL), lambda l, j: (0, 0)),
            pl.BlockSpec((None, D_MODEL, blk), lambda l, j: (l, 0, j)),
            pl.BlockSpec((None, 1, blk), lambda l, j: (l, 0, j)),
        ],
        out_specs=pl.BlockSpec((None, nb, blk), lambda l, j: (l, 0, j)),
        out_shape=jax.ShapeDtypeStruct((DEPTH, nb, ncol), F32),
        compiler_params=pltpu.CompilerParams(vmem_limit_bytes=VMEM_LIMIT),
        name="modulation",
    )(c_all, w_mod, b_mod.reshape(DEPTH, 1, ncol))


def _pre_kernel(x_ref, mod_ref, n1_ref, win_ref, qg_ref, kg_ref, ones_ref, *refs):
    scr_refs = refs[-2:]
    outs = refs[:-2]
    t = x_ref.shape[0]
    x = x_ref[...]
    sh1 = mod_ref[0:1, :]
    sc1 = mod_ref[1:2, :]
    h = (_rms(x, n1_ref[...]) * (1.0 + sc1) + sh1).astype(BF16)
    ones = ones_ref[...]
    n_chunk = QKV_PART // LANE_TILE
    per_part = N_GROUPS + 1

    for part, g_ref in enumerate((qg_ref, kg_ref, None)):
        for c in range(n_chunk):
            col = part * QKV_PART + c * LANE_TILE
            y = jnp.dot(h, win_ref[:, col:col + LANE_TILE], preferred_element_type=F32)
            if g_ref is not None:
                ss = jnp.dot((y * y).astype(BF16), ones, preferred_element_type=F32)
                y = y * lax.rsqrt(ss * (1.0 / HEAD_DIM) + EPS) * g_ref[:, c * LANE_TILE:(c + 1) * LANE_TILE]
            if c < N_GROUPS:
                out = outs[part * per_part + c]
                d = DIL_GROUPS[c][1]
                if d == 1:
                    out[0] = y.astype(out.dtype)
                else:
                    for s_ref, lanes in zip(scr_refs, LANE_HALVES):
                        s_ref[...] = y[:, lanes]
                    for r in range(d):
                        for s_ref, lanes in zip(scr_refs, LANE_HALVES):
                            out[r, :, lanes] = s_ref[pl.ds(r, t // d, stride=d), :].astype(out.dtype)
            else:
                out = outs[part * per_part + N_GROUPS]
                cb = c - N_GROUPS
                out[:, cb * LANE_TILE:(cb + 1) * LANE_TILE] = y.astype(out.dtype)


def _pre_attention(x, mod, n1, w_in, qg, kg, ones, layer):
    b_, l_, _ = x.shape
    t = TOK_TILE_PRE
    per_b = l_ // t
    tok = lambda w: pl.BlockSpec((None, t, w), lambda i: (i // per_b, i % per_b, 0))
    out_specs, out_shape = [], []
    for _ in range(3):
        for _, d in DIL_GROUPS:
            assert t % (BF16_SUBLANES * d) == 0
            out_specs.append(pl.BlockSpec((None, d, t // d, LANE_TILE), lambda i: (i // per_b, 0, i % per_b, 0)))
            out_shape.append(jax.ShapeDtypeStruct((b_, d, l_ // d, LANE_TILE), BF16))
        out_specs.append(tok(WIDTH_B))
        out_shape.append(jax.ShapeDtypeStruct((b_, l_, WIDTH_B), BF16))
    outs = pl.pallas_call(
        _pre_kernel,
        grid=(b_ * per_b,),
        in_specs=[
            tok(D_MODEL),
            pl.BlockSpec((None, None, N_MOD, D_MODEL), lambda i: (layer, i // per_b, 0, 0)),
            _layer_spec((1, D_MODEL), layer),
            _layer_spec((D_MODEL, 3 * QKV_PART), layer),
            _layer_spec((1, QKV_PART), layer),
            _layer_spec((1, QKV_PART), layer),
            _const_spec((LANE_TILE, LANE_TILE)),
        ],
        out_specs=out_specs,
        out_shape=out_shape,
        scratch_shapes=[pltpu.VMEM((t, VREG_LANES), F32)] * 2,
        compiler_params=pltpu.CompilerParams(vmem_limit_bytes=VMEM_LIMIT),
        name="pre_attention",
    )(x, mod, n1, w_in, qg, kg, ones)
    per_part = N_GROUPS + 1
    return [outs[p * per_part:(p + 1) * per_part] for p in range(3)]


def _attn_core(q, k, v, bias):
    nq = q.shape[0]
    head_of_lane = lax.broadcasted_iota(jnp.int32, (nq, LANE_TILE), 1) // HEAD_DIM
    zero = jnp.zeros_like(q)
    qs = jnp.concatenate([jnp.where(head_of_lane == h, q, zero) for h in range(HEADS_PER_GROUP)], axis=0)
    s = lax.dot_general(qs, k, (((1,), (1,)), ((), ())), preferred_element_type=F32) + bias
    m = jnp.max(s, axis=-1, keepdims=True)
    p = jnp.exp(s - m)
    l = jnp.sum(p, axis=-1, keepdims=True)
    o4 = jnp.dot(p.astype(BF16), v, preferred_element_type=F32)
    o4 = o4 * (1.0 / l)
    lse4 = m + jnp.log(l)
    o = jnp.zeros((nq, LANE_TILE), F32)
    lse = jnp.zeros((nq, LANE_TILE), F32)
    for h in range(HEADS_PER_GROUP):
        sel = head_of_lane == h
        o = jnp.where(sel, o4[h * nq:(h + 1) * nq, :], o)
        lse = jnp.where(sel, lse4[h * nq:(h + 1) * nq, :], lse)
    return o, lse


def _toeplitz(f, nq, nk):
    p = nq + nk - 1
    assert f.shape[-1] == p and nq >= 2
    u = jnp.roll(f, -(nq - 1), axis=-1)
    flat = jnp.tile(u, nq)[..., :nq * (p - 1)]
    return flat.reshape(f.shape[:-1] + (nq, p - 1))[..., :nk]


def _attn_a_kernel(q_ref, k_ref, v_ref, bias_ref, o_ref, lse_ref, *, n, kb):
    i = pl.program_id(2)
    n_res, chunk, _ = q_ref.shape
    blocks_per_chunk = chunk // QB
    last_block = n // QB - 1

    def one_block(item):
        res, j = (0, item) if n_res == 1 else (item // blocks_per_chunk, item % blocks_per_chunk)
        blk = i * blocks_per_chunk + j
        start = pl.multiple_of(jnp.clip(blk * QB - HALO_A, 0, n - kb), HALO_A)
        var = jnp.where(blk == 0, 0, jnp.where(blk == last_block, 2, 1))
        rows = pl.ds(pl.multiple_of(j * QB, QB), QB)
        o, lse = _attn_core(q_ref[res, rows, :], k_ref[res, pl.ds(start, kb), :], v_ref[res, pl.ds(start, kb), :],
                            bias_ref[var])
        o_ref[res, rows, :] = o.astype(o_ref.dtype)
        lse_ref[res, rows, :] = lse

    def body(t, carry):
        for u in range(INTERLEAVE):
            one_block(t * INTERLEAVE + u)
        return carry

    lax.fori_loop(0, n_res * blocks_per_chunk // INTERLEAVE, body, 0)


def _dilated_group(q, k, v, bias, d):
    b_, d_, n, _ = q.shape
    assert d_ == d
    kb = bias.shape[-1]
    chunk = min(n, MAX_CHUNK_A)
    n_res = min(d, MAX_CHUNK_A // chunk)
    assert n % QB == 0 and n % chunk == 0 and kb == min(2 * QB, n) and (n == QB or n >= 2 * QB)
    assert d % n_res == 0 and (n_res * chunk // QB) % INTERLEAVE == 0
    qspec = pl.BlockSpec((None, n_res, chunk, LANE_TILE), lambda b, r, i: (b, r, i, 0))
    kvspec = pl.BlockSpec((None, n_res, n, LANE_TILE), lambda b, r, i: (b, r, 0, 0))
    return pl.pallas_call(
        functools.partial(_attn_a_kernel, n=n, kb=kb),
        grid=(b_, d // n_res, n // chunk),
        in_specs=[qspec, kvspec, kvspec, pl.BlockSpec(bias.shape, lambda b, r, i: (0, 0, 0))],
        out_specs=[qspec, qspec],
        out_shape=[jax.ShapeDtypeStruct(q.shape, BF16), jax.ShapeDtypeStruct(q.shape, F32)],
        compiler_params=pltpu.CompilerParams(vmem_limit_bytes=VMEM_LIMIT),
        name=f"dilated_attention_d{d}",
    )(q, k, v, bias)


def _t5_bucket(rel):
    nb = N_BUCKETS // 2
    max_exact = nb // 2
    ret = (rel > 0).astype(np.int32) * nb
    n = np.abs(rel)
    large = max_exact + (np.log(np.maximum(n, 1) / max_exact) / np.log(MAX_DISTANCE / max_exact) * (nb - max_exact)).astype(np.int32)
    large = np.minimum(large, nb - 1)
    return (ret + np.where(n < max_exact, n, large)).astype(np.int32)


def _bias_a(table, g, w, d, n):
    half = w // (2 * d)
    assert half == HALO_A
    kb = min(2 * QB, n)
    sl = slice(g * HEADS_PER_GROUP, (g + 1) * HEADS_PER_GROUP)
    out = []
    for off in (0, -HALO_A, QB - kb):
        rel = np.arange(QB + kb - 1) - (QB - 1) + off
        f = jnp.where((np.abs(rel) <= half)[None, :], table[_t5_bucket(d * rel)][:, sl].T.astype(F32), NEG)
        out.append(_toeplitz(f, QB, kb).reshape(HEADS_PER_GROUP * QB, kb))
    return jnp.stack(out, axis=0)


def _attn_b_kernel(q_ref, k_ref, v_ref, bias_ref, o_ref, *, rows):
    i = pl.program_id(2)

    def one_row(a):
        r = i * ROWS_PER_STEP_B + a
        band = jnp.clip(r - KH_MAX // 2, 0, rows - KH_MAX)
        start = pl.multiple_of(band * GRID_W, GRID_W)
        qrows = pl.ds(pl.multiple_of(a * GRID_W, GRID_W), GRID_W)
        o, _ = _attn_core(q_ref[qrows, :], k_ref[pl.ds(start, KB_B), :], v_ref[pl.ds(start, KB_B), :],
                          bias_ref[r - band])
        o_ref[qrows, :] = o.astype(o_ref.dtype)

    def body(t, carry):
        for u in range(INTERLEAVE):
            one_row(t * INTERLEAVE + u)
        return carry

    lax.fori_loop(0, ROWS_PER_STEP_B // INTERLEAVE, body, 0)


def _neighborhood(qb, kb, vb, bias):
    b_, l_, _ = qb.shape
    rows = l_ // GRID_W
    assert rows % ROWS_PER_STEP_B == 0 and rows >= KH_MAX
    halves = WIDTH_B // LANE_TILE
    qspec = pl.BlockSpec((None, ROWS_PER_STEP_B * GRID_W, LANE_TILE), lambda hf, b, i: (b, i, hf))
    kvspec = pl.BlockSpec((None, l_, LANE_TILE), lambda hf, b, i: (b, 0, hf))
    bspec = pl.BlockSpec((KH_MAX, None, HEADS_PER_GROUP * GRID_W, KB_B), lambda hf, b, i: (0, hf, 0, 0))
    return pl.pallas_call(
        functools.partial(_attn_b_kernel, rows=rows),
        grid=(halves, b_, rows // ROWS_PER_STEP_B),
        in_specs=[qspec, kvspec, kvspec, bspec],
        out_specs=qspec,
        out_shape=jax.ShapeDtypeStruct((b_, l_, WIDTH_B), BF16),
        compiler_params=pltpu.CompilerParams(vmem_limit_bytes=VMEM_LIMIT),
        name="neighborhood_attention",
    )(qb, kb, vb, bias)


def _bias_b(rpb):
    cols = np.arange(GRID_W)
    c_start = np.clip(cols - KW // 2, 0, GRID_W - KW)
    col_ok = (cols[None, :] >= c_start[:, None]) & (cols[None, :] < c_start[:, None] + KW)
    assert np.all(np.abs(cols[None, :] - cols[:, None])[col_ok] <= KW - 1)
    pad = GRID_W - KW
    toep = _toeplitz(jnp.pad(rpb.astype(F32), ((0, 0), (0, 0), (pad, pad))), GRID_W, GRID_W)
    toep = jnp.where(col_ok[None, None], toep, NEG)
    out = []
    for v in range(KH_MAX):
        bias = jnp.stack([toep[:, kr - v + KH_MAX - 1] for kr in range(KH_MAX)], axis=2)
        out.append(bias.reshape(WIDTH_B // LANE_TILE, HEADS_PER_GROUP * GRID_W, KB_B))
    return jnp.stack(out, axis=0)


def _post_kernel(x_ref, mod_ref, n1_ref, n2_ref, oa0_ref, oa1_ref, oa2_ref, l0_ref, l1_ref, l2_ref, ob_ref,
                 wg_ref, bg_ref, wua_ref, wub_ref, wo_ref, w1_ref, w2_ref, out_ref, *scr):
    t = x_ref.shape[0]
    x = x_ref[...]
    sh1, sc1, g1 = mod_ref[0:1, :], mod_ref[1:2, :], mod_ref[2:3, :]
    sh2, sc2, g2 = mod_ref[3:4, :], mod_ref[4:5, :], mod_ref[5:6, :]
    h = (_rms(x, n1_ref[...]) * (1.0 + sc1) + sh1).astype(BF16)
    gate = jax.nn.sigmoid(jnp.dot(h, wg_ref[...], preferred_element_type=F32) + bg_ref[...])

    def token_order(ref, scr_refs):
        d = ref.shape[0]
        if d == 1:
            return ref[0].astype(F32)
        for r in range(d):
            for s_ref, lanes in zip(scr_refs, LANE_HALVES):
                s_ref[pl.ds(r, t // d, stride=d), :] = ref[r, :, lanes].astype(F32)
        return jnp.concatenate([s_ref[...] for s_ref in scr_refs], axis=-1)

    oa0, oa1, oa2 = token_order(oa0_ref, None), token_order(oa1_ref, scr[0:2]), token_order(oa2_ref, scr[2:4])
    l0, l1, l2 = token_order(l0_ref, None), token_order(l1_ref, scr[4:6]), token_order(l2_ref, scr[6:8])
    mx = jnp.maximum(jnp.maximum(l0, l1), l2)
    e0, e1, e2 = jnp.exp(l0 - mx), jnp.exp(l1 - mx), jnp.exp(l2 - mx)
    inv = 1.0 / (e0 + e1 + e2)
    oa = (e0 * inv) * oa0 + (e1 * inv) * oa1 + (e2 * inv) * oa2

    ua = jnp.dot(oa.astype(BF16), wua_ref[...], preferred_element_type=F32)
    ub = jnp.dot(ob_ref[...], wub_ref[...], preferred_element_type=F32)
    mixed = gate[:, :D_MODEL] * ua + gate[:, D_MODEL:] * ub
    x1 = x + g1 * jnp.dot(mixed.astype(BF16), wo_ref[...], preferred_element_type=F32)

    h2 = (_rms(x1, n2_ref[...]) * (1.0 + sc2) + sh2).astype(BF16)
    acc = jnp.zeros_like(x1)
    for c in range(D_FF // FF_CHUNK):
        f = jnp.dot(h2, w1_ref[:, c * FF_CHUNK:(c + 1) * FF_CHUNK], preferred_element_type=F32)
        f = jnp.square(jnp.maximum(f, 0.0)).astype(BF16)
        acc = acc + jnp.dot(f, w2_ref[c * FF_CHUNK:(c + 1) * FF_CHUNK, :], preferred_element_type=F32)
    out_ref[...] = x1 + g2 * acc


def _post_attention(x, mod, n1, n2, oas, lses, ob, wg, bg, wua, wub, wo, w1, w2, layer):
    b_, l_, _ = x.shape
    t = TOK_TILE_POST
    per_b = l_ // t
    tok = lambda w: pl.BlockSpec((None, t, w), lambda i: (i // per_b, i % per_b, 0))
    dil_specs = []
    for _, d in DIL_GROUPS:
        assert t % (BF16_SUBLANES * d) == 0
        dil_specs.append(pl.BlockSpec((None, d, t // d, LANE_TILE), lambda i: (i // per_b, 0, i % per_b, 0)))
    return pl.pallas_call(
        _post_kernel,
        grid=(b_ * per_b,),
        in_specs=[
            tok(D_MODEL),
            pl.BlockSpec((None, None, N_MOD, D_MODEL), lambda i: (layer, i // per_b, 0, 0)),
            _layer_spec((1, D_MODEL), layer), _layer_spec((1, D_MODEL), layer),
            *dil_specs, *dil_specs,
            tok(WIDTH_B),
            _layer_spec((D_MODEL, 2 * D_MODEL), layer), _layer_spec((1, 2 * D_MODEL), layer),
            _layer_spec((WIDTH_A, D_MODEL), layer), _layer_spec((WIDTH_B, D_MODEL), layer),
            _layer_spec((D_MODEL, D_MODEL), layer),
            _layer_spec((D_MODEL, D_FF), layer), _layer_spec((D_FF, D_MODEL), layer),
        ],
        out_specs=tok(D_MODEL),
        out_shape=jax.ShapeDtypeStruct(x.shape, F32),
        scratch_shapes=[pltpu.VMEM((t, VREG_LANES), F32)] * 8,
        compiler_params=pltpu.CompilerParams(vmem_limit_bytes=VMEM_LIMIT),
        name="post_attention_mlp",
    )(x, mod, n1, n2, *oas, *lses, ob, wg, bg, wua, wub, wo, w1, w2)


def _head_sum_matrix():
    head = np.arange(LANE_TILE) // HEAD_DIM
    return jnp.asarray((head[:, None] == head[None, :]).astype(np.float32), dtype=BF16)


def kernel(x_prompt, x_sample, c_prompt, c_sample, norm1_g, norm2_g, w_mod, b_mod, w_in, q_norm_g, k_norm_g,
           rel_bias, rpb, w_gate, b_gate, w_up_a, w_up_b, w_o, w_ff1, w_ff2):
    n_prompt = c_prompt.shape[0]
    mod_all = _modulation(jnp.concatenate([c_prompt, c_sample], axis=0), w_mod, b_mod)
    mod_all = mod_all.reshape(DEPTH, -1, N_MOD, D_MODEL)
    ones = _head_sum_matrix()
    scale = HEAD_DIM ** -0.5
    qg = (q_norm_g.reshape(DEPTH, 1, QKV_PART) * scale).astype(F32)
    kg = k_norm_g.reshape(DEPTH, 1, QKV_PART).astype(F32)
    n1 = norm1_g.reshape(DEPTH, 1, D_MODEL)
    n2 = norm2_g.reshape(DEPTH, 1, D_MODEL)
    bg = b_gate.reshape(DEPTH, 1, 2 * D_MODEL)
    w_in_b, w_gate_b = w_in.astype(BF16), w_gate.astype(BF16)
    w_up_a_b, w_up_b_b, w_o_b = w_up_a.astype(BF16), w_up_b.astype(BF16), w_o.astype(BF16)
    w_ff1_b, w_ff2_b = w_ff1.astype(BF16), w_ff2.astype(BF16)

    xs = (x_prompt, x_sample)
    bias_b = [_bias_b(rpb[l]) for l in range(DEPTH)]
    bias_a = {}
    for x in xs:
        for g, (w, d) in enumerate(DIL_GROUPS):
            n = x.shape[1] // d
            if (g, min(2 * QB, n)) not in bias_a:
                bias_a[(g, min(2 * QB, n))] = _bias_a(rel_bias, g, w, d, n)

    ys = []
    for x, mod_x in ((x_prompt, mod_all[:, :n_prompt]), (x_sample, mod_all[:, n_prompt:])):
        l_ = x.shape[1]
        for l in range(DEPTH):
            q, k, v = _pre_attention(x, mod_x, n1, w_in_b, qg, kg, ones, l)
            oas, lses = [], []
            for g, (w, d) in enumerate(DIL_GROUPS):
                o, lse = _dilated_group(q[g], k[g], v[g], bias_a[(g, min(2 * QB, l_ // d))], d)
                oas.append(o)
                lses.append(lse)
            ob = _neighborhood(q[N_GROUPS], k[N_GROUPS], v[N_GROUPS], bias_b[l])
            x = _post_attention(x, mod_x, n1, n2, oas, lses, ob, w_gate_b, bg, w_up_a_b, w_up_b_b, w_o_b,
                                w_ff1_b, w_ff2_b, l)
        ys.append(x)
    return tuple(ys)
```

```python
import functools

import jax
import jax.numpy as jnp
import numpy as np
from jax import lax
from jax.experimental import pallas as pl
from jax.experimental.pallas import tpu as pltpu

D_MODEL = 1024
DEPTH = 2
HEAD_DIM = 64
DIL_GROUPS = ((128, 1), (512, 4), (2048, 16))
HEADS_PER_GROUP = 4
N_GROUPS = len(DIL_GROUPS)
N_HEADS_A = HEADS_PER_GROUP * N_GROUPS
N_HEADS_B = 8
N_HEADS = N_HEADS_A + N_HEADS_B
WIDTH_A = HEADS_PER_GROUP * HEAD_DIM
WIDTH_B = N_HEADS_B * HEAD_DIM
QKV_PART = N_HEADS * HEAD_DIM
N_BUCKETS = 32
MAX_DISTANCE = 1024
GRID_W = 64
KH_MAX = 8
KW = 16
D_FF = 4 * D_MODEL
N_MOD = 6
EPS = 1e-6
NEG = -1e30

LANE_TILE = 256
QB = 128
HALO_A = 64
KB_B = KH_MAX * GRID_W
ROWS_PER_STEP_B = 16
MAX_CHUNK_A = 1024
INTERLEAVE = 4
TOK_TILE_PRE = 512
TOK_TILE_POST = 256
FF_CHUNK = 1024
BF16_SUBLANES = 16
VREG_LANES = 128
LANE_HALVES = (slice(0, VREG_LANES), slice(VREG_LANES, LANE_TILE))
VMEM_LIMIT = 56 * 1024 * 1024

F32 = jnp.float32
BF16 = jnp.bfloat16


def _rms(x, g):
    return x * lax.rsqrt(jnp.mean(x * x, axis=-1, keepdims=True) + EPS) * g


def _const_spec(shape):
    return pl.BlockSpec(shape, lambda *_: (0,) * len(shape), pipeline_mode=pl.Buffered(1))


def _layer_spec(shape, layer):
    return pl.BlockSpec((None,) + shape, lambda *_: (layer,) + (0,) * len(shape), pipeline_mode=pl.Buffered(1))


def _mod_kernel(c_ref, w_ref, b_ref, o_ref):
    c = c_ref[...]
    s = c * jax.nn.sigmoid(c)
    o_ref[...] = jnp.dot(s, w_ref[...], preferred_element_type=F32) + b_ref[...]


def _modulation(c_all, w_mod, b_mod):
    nb = c_all.shape[0]
    ncol = N_MOD * D_MODEL
    blk = 1536
    return pl.pallas_call(
        _mod_kernel,
        grid=(DEPTH, ncol // blk),
        in_specs=[
            pl.BlockSpec((nb, D_MODEL), lambda l, j: (0, 0)),
            pl.BlockSpec((None, D_MODEL, blk), lambda l, j: (l, 0, j)),
            pl.BlockSpec((None, 1, blk), lambda l, j: (l, 0, j)),
        ],
        out_specs=pl.BlockSpec((None, nb, blk), lambda l, j: (l, 0, j)),
        out_shape=jax.ShapeDtypeStruct((DEPTH, nb, ncol), F32),
        compiler_params=pltpu.CompilerParams(vmem_limit_bytes=VMEM_LIMIT),
        name="modulation",
    )(c_all, w_mod, b_mod.reshape(DEPTH, 1, ncol))


def _pre_kernel(x_ref, mod_ref, n1_ref, win_ref, qg_ref, kg_ref, ones_ref, *refs):
    scr_refs = refs[-2:]
    outs = refs[:-2]
    t = x_ref.shape[0]
    x = x_ref[...]
    sh1 = mod_ref[0:1, :]
    sc1 = mod_ref[1:2, :]
    h = (_rms(x, n1_ref[...]) * (1.0 + sc1) + sh1).astype(BF16)
    ones = ones_ref[...]
    n_chunk = QKV_PART // LANE_TILE
    per_part = N_GROUPS + 1

    for part, g_ref in enumerate((qg_ref, kg_ref, None)):
        y_part = jnp.dot(h, win_ref[:, part * QKV_PART:(part + 1) * QKV_PART], preferred_element_type=F32)
        for c in range(n_chunk):
            y = y_part[:, c * LANE_TILE:(c + 1) * LANE_TILE]
            if g_ref is not None:
                ss = jnp.dot((y * y).astype(BF16), ones, preferred_element_type=F32)
                y = y * lax.rsqrt(ss * (1.0 / HEAD_DIM) + EPS) * g_ref[:, c * LANE_TILE:(c + 1) * LANE_TILE]
            if c < N_GROUPS:
                out = outs[part * per_part + c]
                d = DIL_GROUPS[c][1]
                if d == 1:
                    out[0] = y.astype(out.dtype)
                else:
                    for s_ref, lanes in zip(scr_refs, LANE_HALVES):
                        s_ref[...] = y[:, lanes]
                    for r in range(d):
                        for s_ref, lanes in zip(scr_refs, LANE_HALVES):
                            out[r, :, lanes] = s_ref[pl.ds(r, t // d, stride=d), :].astype(out.dtype)
            else:
                out = outs[part * per_part + N_GROUPS]
                cb = c - N_GROUPS
                out[:, cb * LANE_TILE:(cb + 1) * LANE_TILE] = y.astype(out.dtype)


def _pre_attention(x, mod, n1, w_in, qg, kg, ones, layer):
    b_, l_, _ = x.shape
    t = TOK_TILE_PRE
    per_b = l_ // t
    tok = lambda w: pl.BlockSpec((None, t, w), lambda i: (i // per_b, i % per_b, 0))
    out_specs, out_shape = [], []
    for _ in range(3):
        for _, d in DIL_GROUPS:
            assert t % (BF16_SUBLANES * d) == 0
            out_specs.append(pl.BlockSpec((None, d, t // d, LANE_TILE), lambda i: (i // per_b, 0, i % per_b, 0)))
            out_shape.append(jax.ShapeDtypeStruct((b_, d, l_ // d, LANE_TILE), BF16))
        out_specs.append(tok(WIDTH_B))
        out_shape.append(jax.ShapeDtypeStruct((b_, l_, WIDTH_B), BF16))
    outs = pl.pallas_call(
        _pre_kernel,
        grid=(b_ * per_b,),
        in_specs=[
            tok(D_MODEL),
            pl.BlockSpec((None, None, N_MOD, D_MODEL), lambda i: (layer, i // per_b, 0, 0)),
            _layer_spec((1, D_MODEL), layer),
            _layer_spec((D_MODEL, 3 * QKV_PART), layer),
            _layer_spec((1, QKV_PART), layer),
            _layer_spec((1, QKV_PART), layer),
            _const_spec((LANE_TILE, LANE_TILE)),
        ],
        out_specs=out_specs,
        out_shape=out_shape,
        scratch_shapes=[pltpu.VMEM((t, VREG_LANES), F32)] * 2,
        compiler_params=pltpu.CompilerParams(vmem_limit_bytes=VMEM_LIMIT),
        name="pre_attention",
    )(x, mod, n1, w_in, qg, kg, ones)
    per_part = N_GROUPS + 1
    return [outs[p * per_part:(p + 1) * per_part] for p in range(3)]


def _attn_core(q, k, v, bias):
    nq = q.shape[0]
    head_of_lane = lax.broadcasted_iota(jnp.int32, (nq, LANE_TILE), 1) // HEAD_DIM
    zero = jnp.zeros_like(q)
    qs = jnp.concatenate([jnp.where(head_of_lane == h, q, zero) for h in range(HEADS_PER_GROUP)], axis=0)
    s = lax.dot_general(qs, k, (((1,), (1,)), ((), ())), preferred_element_type=F32) + bias
    m = jnp.max(s, axis=-1, keepdims=True)
    p = jnp.exp(s - m)
    l = jnp.sum(p, axis=-1, keepdims=True)
    o4 = jnp.dot(p.astype(BF16), v, preferred_element_type=F32)
    o4 = o4 * (1.0 / l)
    lse4 = m + jnp.log(l)
    o = jnp.zeros((nq, LANE_TILE), F32)
    lse = jnp.zeros((nq, LANE_TILE), F32)
    for h in range(HEADS_PER_GROUP):
        sel = head_of_lane == h
        o = jnp.where(sel, o4[h * nq:(h + 1) * nq, :], o)
        lse = jnp.where(sel, lse4[h * nq:(h + 1) * nq, :], lse)
    return o, lse


def _toeplitz(f, nq, nk):
    p = nq + nk - 1
    assert f.shape[-1] == p and nq >= 2
    u = jnp.roll(f, -(nq - 1), axis=-1)
    flat = jnp.tile(u, nq)[..., :nq * (p - 1)]
    return flat.reshape(f.shape[:-1] + (nq, p - 1))[..., :nk]


def _attn_a_kernel(q_ref, k_ref, v_ref, bias_ref, o_ref, lse_ref, *, n, kb):
    i = pl.program_id(2)
    n_res, chunk, _ = q_ref.shape
    blocks_per_chunk = chunk // QB
    last_block = n // QB - 1

    def one_block(item):
        res, j = (0, item) if n_res == 1 else (item // blocks_per_chunk, item % blocks_per_chunk)
        blk = i * blocks_per_chunk + j
        start = pl.multiple_of(jnp.clip(blk * QB - HALO_A, 0, n - kb), HALO_A)
        var = jnp.where(blk == 0, 0, jnp.where(blk == last_block, 2, 1))
        rows = pl.ds(pl.multiple_of(j * QB, QB), QB)
        o, lse = _attn_core(q_ref[res, rows, :], k_ref[res, pl.ds(start, kb), :], v_ref[res, pl.ds(start, kb), :],
                            bias_ref[var])
        o_ref[res, rows, :] = o.astype(o_ref.dtype)
        lse_ref[res, rows, :] = lse

    def body(t, carry):
        for u in range(INTERLEAVE):
            one_block(t * INTERLEAVE + u)
        return carry

    lax.fori_loop(0, n_res * blocks_per_chunk // INTERLEAVE, body, 0)


def _dilated_group(q, k, v, bias, d):
    b_, d_, n, _ = q.shape
    assert d_ == d
    kb = bias.shape[-1]
    chunk = min(n, MAX_CHUNK_A)
    n_res = min(d, MAX_CHUNK_A // chunk)
    assert n % QB == 0 and n % chunk == 0 and kb == min(2 * QB, n) and (n == QB or n >= 2 * QB)
    assert d % n_res == 0 and (n_res * chunk // QB) % INTERLEAVE == 0
    qspec = pl.BlockSpec((None, n_res, chunk, LANE_TILE), lambda b, r, i: (b, r, i, 0))
    kvspec = pl.BlockSpec((None, n_res, n, LANE_TILE), lambda b, r, i: (b, r, 0, 0))
    return pl.pallas_call(
        functools.partial(_attn_a_kernel, n=n, kb=kb),
        grid=(b_, d // n_res, n // chunk),
        in_specs=[qspec, kvspec, kvspec, pl.BlockSpec(bias.shape, lambda b, r, i: (0, 0, 0))],
        out_specs=[qspec, qspec],
        out_shape=[jax.ShapeDtypeStruct(q.shape, BF16), jax.ShapeDtypeStruct(q.shape, F32)],
        compiler_params=pltpu.CompilerParams(vmem_limit_bytes=VMEM_LIMIT),
        name=f"dilated_attention_d{d}",
    )(q, k, v, bias)


def _t5_bucket(rel):
    nb = N_BUCKETS // 2
    max_exact = nb // 2
    ret = (rel > 0).astype(np.int32) * nb
    n = np.abs(rel)
    large = max_exact + (np.log(np.maximum(n, 1) / max_exact) / np.log(MAX_DISTANCE / max_exact) * (nb - max_exact)).astype(np.int32)
    large = np.minimum(large, nb - 1)
    return (ret + np.where(n < max_exact, n, large)).astype(np.int32)


def _bias_a(table, g, w, d, n):
    half = w // (2 * d)
    assert half == HALO_A
    kb = min(2 * QB, n)
    sl = slice(g * HEADS_PER_GROUP, (g + 1) * HEADS_PER_GROUP)
    out = []
    for off in (0, -HALO_A, QB - kb):
        rel = np.arange(QB + kb - 1) - (QB - 1) + off
        f = jnp.where((np.abs(rel) <= half)[None, :], table[_t5_bucket(d * rel)][:, sl].T.astype(F32), NEG)
        out.append(_toeplitz(f, QB, kb).reshape(HEADS_PER_GROUP * QB, kb))
    return jnp.stack(out, axis=0)


def _attn_b_kernel(q_ref, k_ref, v_ref, bias_ref, o_ref, *, rows):
    i = pl.program_id(2)

    def one_row(a):
        r = i * ROWS_PER_STEP_B + a
        band = jnp.clip(r - KH_MAX // 2, 0, rows - KH_MAX)
        start = pl.multiple_of(band * GRID_W, GRID_W)
        qrows = pl.ds(pl.multiple_of(a * GRID_W, GRID_W), GRID_W)
        o, _ = _attn_core(q_ref[qrows, :], k_ref[pl.ds(start, KB_B), :], v_ref[pl.ds(start, KB_B), :],
                          bias_ref[r - band])
        o_ref[qrows, :] = o.astype(o_ref.dtype)

    def body(t, carry):
        for u in range(INTERLEAVE):
            one_row(t * INTERLEAVE + u)
        return carry

    lax.fori_loop(0, ROWS_PER_STEP_B // INTERLEAVE, body, 0)


def _neighborhood(qb, kb, vb, bias):
    b_, l_, _ = qb.shape
    rows = l_ // GRID_W
    assert rows % ROWS_PER_STEP_B == 0 and rows >= KH_MAX
    halves = WIDTH_B // LANE_TILE
    qspec = pl.BlockSpec((None, ROWS_PER_STEP_B * GRID_W, LANE_TILE), lambda hf, b, i: (b, i, hf))
    kvspec = pl.BlockSpec((None, l_, LANE_TILE), lambda hf, b, i: (b, 0, hf))
    bspec = pl.BlockSpec((KH_MAX, None, HEADS_PER_GROUP * GRID_W, KB_B), lambda hf, b, i: (0, hf, 0, 0))
    return pl.pallas_call(
        functools.partial(_attn_b_kernel, rows=rows),
        grid=(halves, b_, rows // ROWS_PER_STEP_B),
        in_specs=[qspec, kvspec, kvspec, bspec],
        out_specs=qspec,
        out_shape=jax.ShapeDtypeStruct((b_, l_, WIDTH_B), BF16),
        compiler_params=pltpu.CompilerParams(vmem_limit_bytes=VMEM_LIMIT),
        name="neighborhood_attention",
    )(qb, kb, vb, bias)


def _bias_b(rpb):
    cols = np.arange(GRID_W)
    c_start = np.clip(cols - KW // 2, 0, GRID_W - KW)
    col_ok = (cols[None, :] >= c_start[:, None]) & (cols[None, :] < c_start[:, None] + KW)
    assert np.all(np.abs(cols[None, :] - cols[:, None])[col_ok] <= KW - 1)
    pad = GRID_W - KW
    toep = _toeplitz(jnp.pad(rpb.astype(F32), ((0, 0), (0, 0), (pad, pad))), GRID_W, GRID_W)
    toep = jnp.where(col_ok[None, None], toep, NEG)
    out = []
    for v in range(KH_MAX):
        bias = jnp.stack([toep[:, kr - v + KH_MAX - 1] for kr in range(KH_MAX)], axis=2)
        out.append(bias.reshape(WIDTH_B // LANE_TILE, HEADS_PER_GROUP * GRID_W, KB_B))
    return jnp.stack(out, axis=0)


def _post_kernel(x_ref, mod_ref, n1_ref, n2_ref, oa0_ref, oa1_ref, oa2_ref, l0_ref, l1_ref, l2_ref, ob_ref,
                 wg_ref, bg_ref, wua_ref, wub_ref, wo_ref, w1_ref, w2_ref, out_ref, *scr):
    t = x_ref.shape[0]
    x = x_ref[...]
    sh1, sc1, g1 = mod_ref[0:1, :], mod_ref[1:2, :], mod_ref[2:3, :]
    sh2, sc2, g2 = mod_ref[3:4, :], mod_ref[4:5, :], mod_ref[5:6, :]
    h = (_rms(x, n1_ref[...]) * (1.0 + sc1) + sh1).astype(BF16)
    gate = jax.nn.sigmoid(jnp.dot(h, wg_ref[...], preferred_element_type=F32) + bg_ref[...])

    def token_order(ref, scr_refs):
        d = ref.shape[0]
        if d == 1:
            return ref[0].astype(F32)
        for r in range(d):
            for s_ref, lanes in zip(scr_refs, LANE_HALVES):
                s_ref[pl.ds(r, t // d, stride=d), :] = ref[r, :, lanes].astype(F32)
        return jnp.concatenate([s_ref[...] for s_ref in scr_refs], axis=-1)

    oa0, oa1, oa2 = token_order(oa0_ref, None), token_order(oa1_ref, scr[0:2]), token_order(oa2_ref, scr[2:4])
    l0, l1, l2 = token_order(l0_ref, None), token_order(l1_ref, scr[4:6]), token_order(l2_ref, scr[6:8])
    mx = jnp.maximum(jnp.maximum(l0, l1), l2)
    e0, e1, e2 = jnp.exp(l0 - mx), jnp.exp(l1 - mx), jnp.exp(l2 - mx)
    inv = 1.0 / (e0 + e1 + e2)
    oa = (e0 * inv) * oa0 + (e1 * inv) * oa1 + (e2 * inv) * oa2

    ua = jnp.dot(oa.astype(BF16), wua_ref[...], preferred_element_type=F32)
    ub = jnp.dot(ob_ref[...], wub_ref[...], preferred_element_type=F32)
    mixed = gate[:, :D_MODEL] * ua + gate[:, D_MODEL:] * ub
    x1 = x + g1 * jnp.dot(mixed.astype(BF16), wo_ref[...], preferred_element_type=F32)

    h2 = (_rms(x1, n2_ref[...]) * (1.0 + sc2) + sh2).astype(BF16)
    acc = jnp.zeros_like(x1)
    for c in range(D_FF // FF_CHUNK):
        f = jnp.dot(h2, w1_ref[:, c * FF_CHUNK:(c + 1) * FF_CHUNK], preferred_element_type=F32)
        f = jnp.square(jnp.maximum(f, 0.0)).astype(BF16)
        acc = acc + jnp.dot(f, w2_ref[c * FF_CHUNK:(c + 1) * FF_CHUNK, :], preferred_element_type=F32)
    out_ref[...] = x1 + g2 * acc


def _post_attention(x, mod, n1, n2, oas, lses, ob, wg, bg, wua, wub, wo, w1, w2, layer):
    b_, l_, _ = x.shape
    t = TOK_TILE_POST
    per_b = l_ // t
    tok = lambda w: pl.BlockSpec((None, t, w), lambda i: (i // per_b, i % per_b, 0))
    dil_specs = []
    for _, d in DIL_GROUPS:
        assert t % (BF16_SUBLANES * d) == 0
        dil_specs.append(pl.BlockSpec((None, d, t // d, LANE_TILE), lambda i: (i // per_b, 0, i % per_b, 0)))
    return pl.pallas_call(
        _post_kernel,
        grid=(b_ * per_b,),
        in_specs=[
            tok(D_MODEL),
            pl.BlockSpec((None, None, N_MOD, D_MODEL), lambda i: (layer, i // per_b, 0, 0)),
            _layer_spec((1, D_MODEL), layer), _layer_spec((1, D_MODEL), layer),
            *dil_specs, *dil_specs,
            tok(WIDTH_B),
            _layer_spec((D_MODEL, 2 * D_MODEL), layer), _layer_spec((1, 2 * D_MODEL), layer),
            _layer_spec((WIDTH_A, D_MODEL), layer), _layer_spec((WIDTH_B, D_MODEL), layer),
            _layer_spec((D_MODEL, D_MODEL), layer),
            _layer_spec((D_MODEL, D_FF), layer), _layer_spec((D_FF, D_MODEL), layer),
        ],
        out_specs=tok(D_MODEL),
        out_shape=jax.ShapeDtypeStruct(x.shape, F32),
        scratch_shapes=[pltpu.VMEM((t, VREG_LANES), F32)] * 8,
        compiler_params=pltpu.CompilerParams(vmem_limit_bytes=VMEM_LIMIT),
        name="post_attention_mlp",
    )(x, mod, n1, n2, *oas, *lses, ob, wg, bg, wua, wub, wo, w1, w2)


def _head_sum_matrix():
    head = np.arange(LANE_TILE) // HEAD_DIM
    return jnp.asarray((head[:, None] == head[None, :]).astype(np.float32), dtype=BF16)


def kernel(x_prompt, x_sample, c_prompt, c_sample, norm1_g, norm2_g, w_mod, b_mod, w_in, q_norm_g, k_norm_g,
           rel_bias, rpb, w_gate, b_gate, w_up_a, w_up_b, w_o, w_ff1, w_ff2):
    n_prompt = c_prompt.shape[0]
    mod_all = _modulation(jnp.concatenate([c_prompt, c_sample], axis=0), w_mod, b_mod)
    mod_all = mod_all.reshape(DEPTH, -1, N_MOD, D_MODEL)
    ones = _head_sum_matrix()
    scale = HEAD_DIM ** -0.5
    qg = (q_norm_g.reshape(DEPTH, 1, QKV_PART) * scale).astype(F32)
    kg = k_norm_g.reshape(DEPTH, 1, QKV_PART).astype(F32)
    n1 = norm1_g.reshape(DEPTH, 1, D_MODEL)
    n2 = norm2_g.reshape(DEPTH, 1, D_MODEL)
    bg = b_gate.reshape(DEPTH, 1, 2 * D_MODEL)
    w_in_b, w_gate_b = w_in.astype(BF16), w_gate.astype(BF16)
    w_up_a_b, w_up_b_b, w_o_b = w_up_a.astype(BF16), w_up_b.astype(BF16), w_o.astype(BF16)
    w_ff1_b, w_ff2_b = w_ff1.astype(BF16), w_ff2.astype(BF16)

    xs = (x_prompt, x_sample)
    bias_b = [_bias_b(rpb[l]) for l in range(DEPTH)]
    bias_a = {}
    for x in xs:
        for g, (w, d) in enumerate(DIL_GROUPS):
            n = x.shape[1] // d
            if (g, min(2 * QB, n)) not in bias_a:
                bias_a[(g, min(2 * QB, n))] = _bias_a(rel_bias, g, w, d, n)

    ys = []
    for x, mod_x in ((x_prompt, mod_all[:, :n_prompt]), (x_sample, mod_all[:, n_prompt:])):
        l_ = x.shape[1]
        for l in range(DEPTH):
            q, k, v = _pre_attention(x, mod_x, n1, w_in_b, qg, kg, ones, l)
            oas, lses = [], []
            for g, (w, d) in enumerate(DIL_GROUPS):
                o, lse = _dilated_group(q[g], k[g], v[g], bias_a[(g, min(2 * QB, l_ // d))], d)
                oas.append(o)
                lses.append(lse)
            ob = _neighborhood(q[N_GROUPS], k[N_GROUPS], v[N_GROUPS], bias_b[l])
            x = _post_attention(x, mod_x, n1, n2, oas, lses, ob, w_gate_b, bg, w_up_a_b, w_up_b_b, w_o_b,
                                w_ff1_b, w_ff2_b, l)
        ys.append(x)
    return tuple(ys)
```

```python
import functools

import jax
import jax.numpy as jnp
import numpy as np
from jax import lax
from jax.experimental import pallas as pl
from jax.experimental.pallas import tpu as pltpu

D_MODEL = 1024
DEPTH = 2
HEAD_DIM = 64
DIL_GROUPS = ((128, 1), (512, 4), (2048, 16))
HEADS_PER_GROUP = 4
N_GROUPS = len(DIL_GROUPS)
N_HEADS_A = HEADS_PER_GROUP * N_GROUPS
N_HEADS_B = 8
N_HEADS = N_HEADS_A + N_HEADS_B
WIDTH_A = HEADS_PER_GROUP * HEAD_DIM
WIDTH_B = N_HEADS_B * HEAD_DIM
QKV_PART = N_HEADS * HEAD_DIM
N_BUCKETS = 32
MAX_DISTANCE = 1024
GRID_W = 64
KH_MAX = 8
KW = 16
D_FF = 4 * D_MODEL
N_MOD = 6
EPS = 1e-6
NEG = -1e30
LOG2E = 1.4426950408889634
LN2 = 0.6931471805599453

LANE_TILE = 256
QB = 128
HALO_A = 64
KB_B = KH_MAX * GRID_W
ROWS_PER_STEP_B = 16
MAX_CHUNK_A = 1024
INTERLEAVE_B = 8
INTERLEAVE = 8
TOK_TILE_PRE = 512
TOK_TILE_POST = 512
FF_CHUNK = 1024
BF16_SUBLANES = 16
VREG_LANES = 128
LANE_HALVES = (slice(0, VREG_LANES), slice(VREG_LANES, LANE_TILE))
VMEM_LIMIT = 56 * 1024 * 1024

F32 = jnp.float32
BF16 = jnp.bfloat16


def _rms(x, g):
    return x * lax.rsqrt(jnp.mean(x * x, axis=-1, keepdims=True) + EPS) * g


def _const_spec(shape):
    return pl.BlockSpec(shape, lambda *_: (0,) * len(shape), pipeline_mode=pl.Buffered(1))


def _layer_spec(shape, layer):
    return pl.BlockSpec((None,) + shape, lambda *_: (layer,) + (0,) * len(shape), pipeline_mode=pl.Buffered(1))


def _mod_kernel(c_ref, w_ref, b_ref, o_ref):
    c = c_ref[...]
    s = c * jax.nn.sigmoid(c)
    o_ref[...] = jnp.dot(s, w_ref[...], preferred_element_type=F32) + b_ref[...]


def _modulation(c_all, w_mod, b_mod):
    nb = c_all.shape[0]
    ncol = N_MOD * D_MODEL
    blk = 1536
    return pl.pallas_call(
        _mod_kernel,
        grid=(DEPTH, ncol // blk),
        in_specs=[
            pl.BlockSpec((nb, D_MODEL), lambda l, j: (0, 0)),
            pl.BlockSpec((None, D_MODEL, blk), lambda l, j: (l, 0, j)),
            pl.BlockSpec((None, 1, blk), lambda l, j: (l, 0, j)),
        ],
        out_specs=pl.BlockSpec((None, nb, blk), lambda l, j: (l, 0, j)),
        out_shape=jax.ShapeDtypeStruct((DEPTH, nb, ncol), F32),
        compiler_params=pltpu.CompilerParams(vmem_limit_bytes=VMEM_LIMIT),
        name="modulation",
    )(c_all, w_mod, b_mod.reshape(DEPTH, 1, ncol))


def _pre_kernel(x_ref, mod_ref, n1_ref, win_ref, qg_ref, kg_ref, ones_ref, *refs):
    scr_refs = refs[-2:]
    outs = refs[:-2]
    t = x_ref.shape[0]
    x = x_ref[...]
    sh1 = mod_ref[0:1, :]
    sc1 = mod_ref[1:2, :]
    h = (_rms(x, n1_ref[...]) * (1.0 + sc1) + sh1).astype(BF16)
    ones = ones_ref[...]
    n_chunk = QKV_PART // LANE_TILE
    per_part = N_GROUPS + 1

    for part, g_ref in enumerate((qg_ref, kg_ref, None)):
        y_part = jnp.dot(h, win_ref[:, part * QKV_PART:(part + 1) * QKV_PART], preferred_element_type=F32)
        for c in range(n_chunk):
            y = y_part[:, c * LANE_TILE:(c + 1) * LANE_TILE]
            if g_ref is not None:
                ss = jnp.dot((y * y).astype(BF16), ones, preferred_element_type=F32)
                y = y * lax.rsqrt(ss * (1.0 / HEAD_DIM) + EPS) * g_ref[:, c * LANE_TILE:(c + 1) * LANE_TILE]
            if c < N_GROUPS:
                out = outs[part * per_part + c]
                d = DIL_GROUPS[c][1]
                if d == 1:
                    out[0] = y.astype(out.dtype)
                else:
                    for s_ref, lanes in zip(scr_refs, LANE_HALVES):
                        s_ref[...] = y[:, lanes]
                    for r in range(d):
                        for s_ref, lanes in zip(scr_refs, LANE_HALVES):
                            out[r, :, lanes] = s_ref[pl.ds(r, t // d, stride=d), :].astype(out.dtype)
            else:
                out = outs[part * per_part + N_GROUPS]
                cb = c - N_GROUPS
                out[:, cb * LANE_TILE:(cb + 1) * LANE_TILE] = y.astype(out.dtype)


def _pre_attention(x, mod, n1, w_in, qg, kg, ones, layer):
    b_, l_, _ = x.shape
    t = TOK_TILE_PRE
    per_b = l_ // t
    tok = lambda w: pl.BlockSpec((None, t, w), lambda i: (i // per_b, i % per_b, 0))
    out_specs, out_shape = [], []
    for _ in range(3):
        for _, d in DIL_GROUPS:
            assert t % (BF16_SUBLANES * d) == 0
            out_specs.append(pl.BlockSpec((None, d, t // d, LANE_TILE), lambda i: (i // per_b, 0, i % per_b, 0)))
            out_shape.append(jax.ShapeDtypeStruct((b_, d, l_ // d, LANE_TILE), BF16))
        out_specs.append(tok(WIDTH_B))
        out_shape.append(jax.ShapeDtypeStruct((b_, l_, WIDTH_B), BF16))
    outs = pl.pallas_call(
        _pre_kernel,
        grid=(b_ * per_b,),
        in_specs=[
            tok(D_MODEL),
            pl.BlockSpec((None, None, N_MOD, D_MODEL), lambda i: (layer, i // per_b, 0, 0)),
            _layer_spec((1, D_MODEL), layer),
            _layer_spec((D_MODEL, 3 * QKV_PART), layer),
            _layer_spec((1, QKV_PART), layer),
            _layer_spec((1, QKV_PART), layer),
            _const_spec((LANE_TILE, LANE_TILE)),
        ],
        out_specs=out_specs,
        out_shape=out_shape,
        scratch_shapes=[pltpu.VMEM((t, VREG_LANES), F32)] * 2,
        compiler_params=pltpu.CompilerParams(vmem_limit_bytes=VMEM_LIMIT),
        name="pre_attention",
    )(x, mod, n1, w_in, qg, kg, ones)
    per_part = N_GROUPS + 1
    return [outs[p * per_part:(p + 1) * per_part] for p in range(3)]


def _attn_core(q, k, v, bias):
    nq = q.shape[0]
    head_of_lane = lax.broadcasted_iota(jnp.int32, (nq, LANE_TILE), 1) // HEAD_DIM
    zero = jnp.zeros_like(q)
    qs = jnp.concatenate([jnp.where(head_of_lane == h, q, zero) for h in range(HEADS_PER_GROUP)], axis=0)
    s = lax.dot_general(qs, k, (((1,), (1,)), ((), ())), preferred_element_type=F32) + bias
    m = jnp.max(s, axis=-1, keepdims=True)
    p = jnp.exp2(s - m)
    l = jnp.sum(p, axis=-1, keepdims=True)
    o4 = jnp.dot(p.astype(BF16), v, preferred_element_type=F32)

    def unstack(x4):
        x = jnp.broadcast_to(x4[:nq, :], (nq, LANE_TILE))
        for h in range(1, HEADS_PER_GROUP):
            x = jnp.where(head_of_lane == h, x4[h * nq:(h + 1) * nq, :], x)
        return x

    l_dense = unstack(l)
    o = unstack(o4) * (1.0 / l_dense)
    lse = (unstack(m) + jnp.log2(l_dense)) * LN2
    return o, lse


def _toeplitz(f, nq, nk):
    p = nq + nk - 1
    assert f.shape[-1] == p and nq >= 2
    u = jnp.roll(f, -(nq - 1), axis=-1)
    flat = jnp.tile(u, nq)[..., :nq * (p - 1)]
    return flat.reshape(f.shape[:-1] + (nq, p - 1))[..., :nk]


def _attn_a_kernel(q_ref, k_ref, v_ref, bias_ref, o_ref, lse_ref, *, n, kb):
    i = pl.program_id(2)
    n_res, chunk, _ = q_ref.shape
    blocks_per_chunk = chunk // QB
    last_block = n // QB - 1

    def one_block(item):
        res, j = (0, item) if n_res == 1 else (item // blocks_per_chunk, item % blocks_per_chunk)
        blk = i * blocks_per_chunk + j
        start = pl.multiple_of(jnp.clip(blk * QB - HALO_A, 0, n - kb), HALO_A)
        var = jnp.where(blk == 0, 0, jnp.where(blk == last_block, 2, 1))
        rows = pl.ds(pl.multiple_of(j * QB, QB), QB)
        o, lse = _attn_core(q_ref[res, rows, :], k_ref[res, pl.ds(start, kb), :], v_ref[res, pl.ds(start, kb), :],
                            bias_ref[var])
        o_ref[res, rows, :] = o.astype(o_ref.dtype)
        lse_ref[res, rows, :] = lse

    def body(t, carry):
        for u in range(INTERLEAVE):
            one_block(t * INTERLEAVE + u)
        return carry

    lax.fori_loop(0, n_res * blocks_per_chunk // INTERLEAVE, body, 0)


def _dilated_group(q, k, v, bias, d):
    b_, d_, n, _ = q.shape
    assert d_ == d
    kb = bias.shape[-1]
    chunk = min(n, MAX_CHUNK_A)
    n_res = min(d, MAX_CHUNK_A // chunk)
    assert n % QB == 0 and n % chunk == 0 and kb == min(2 * QB, n) and (n == QB or n >= 2 * QB)
    assert d % n_res == 0 and (n_res * chunk // QB) % INTERLEAVE == 0
    qspec = pl.BlockSpec((None, n_res, chunk, LANE_TILE), lambda b, r, i: (b, r, i, 0))
    kvspec = pl.BlockSpec((None, n_res, n, LANE_TILE), lambda b, r, i: (b, r, 0, 0))
    return pl.pallas_call(
        functools.partial(_attn_a_kernel, n=n, kb=kb),
        grid=(b_, d // n_res, n // chunk),
        in_specs=[qspec, kvspec, kvspec, pl.BlockSpec(bias.shape, lambda b, r, i: (0, 0, 0))],
        out_specs=[qspec, qspec],
        out_shape=[jax.ShapeDtypeStruct(q.shape, BF16), jax.ShapeDtypeStruct(q.shape, F32)],
        compiler_params=pltpu.CompilerParams(vmem_limit_bytes=VMEM_LIMIT),
        name=f"dilated_attention_d{d}",
    )(q, k, v, bias)


def _t5_bucket(rel):
    nb = N_BUCKETS // 2
    max_exact = nb // 2
    ret = (rel > 0).astype(np.int32) * nb
    n = np.abs(rel)
    large = max_exact + (np.log(np.maximum(n, 1) / max_exact) / np.log(MAX_DISTANCE / max_exact) * (nb - max_exact)).astype(np.int32)
    large = np.minimum(large, nb - 1)
    return (ret + np.where(n < max_exact, n, large)).astype(np.int32)


def _bias_a(table, g, w, d, n):
    half = w // (2 * d)
    assert half == HALO_A
    kb = min(2 * QB, n)
    sl = slice(g * HEADS_PER_GROUP, (g + 1) * HEADS_PER_GROUP)
    out = []
    for off in (0, -HALO_A, QB - kb):
        rel = np.arange(QB + kb - 1) - (QB - 1) + off
        f = jnp.where((np.abs(rel) <= half)[None, :], table[_t5_bucket(d * rel)][:, sl].T.astype(F32) * LOG2E, NEG)
        out.append(_toeplitz(f, QB, kb).reshape(HEADS_PER_GROUP * QB, kb))
    return jnp.stack(out, axis=0)


def _attn_b_kernel(q_ref, k_ref, v_ref, bias_ref, o_ref, *, rows):
    i = pl.program_id(2)

    def one_row(a):
        r = i * ROWS_PER_STEP_B + a
        band = jnp.clip(r - KH_MAX // 2, 0, rows - KH_MAX)
        start = pl.multiple_of(band * GRID_W, GRID_W)
        qrows = pl.ds(pl.multiple_of(a * GRID_W, GRID_W), GRID_W)
        o, _ = _attn_core(q_ref[qrows, :], k_ref[pl.ds(start, KB_B), :], v_ref[pl.ds(start, KB_B), :],
                          bias_ref[r - band])
        o_ref[qrows, :] = o.astype(o_ref.dtype)

    def body(t, carry):
        for u in range(INTERLEAVE_B):
            one_row(t * INTERLEAVE_B + u)
        return carry

    lax.fori_loop(0, ROWS_PER_STEP_B // INTERLEAVE_B, body, 0)


def _neighborhood(qb, kb, vb, bias):
    b_, l_, _ = qb.shape
    rows = l_ // GRID_W
    assert rows % ROWS_PER_STEP_B == 0 and rows >= KH_MAX
    halves = WIDTH_B // LANE_TILE
    qspec = pl.BlockSpec((None, ROWS_PER_STEP_B * GRID_W, LANE_TILE), lambda hf, b, i: (b, i, hf))
    kvspec = pl.BlockSpec((None, l_, LANE_TILE), lambda hf, b, i: (b, 0, hf))
    bspec = pl.BlockSpec((KH_MAX, None, HEADS_PER_GROUP * GRID_W, KB_B), lambda hf, b, i: (0, hf, 0, 0))
    return pl.pallas_call(
        functools.partial(_attn_b_kernel, rows=rows),
        grid=(halves, b_, rows // ROWS_PER_STEP_B),
        in_specs=[qspec, kvspec, kvspec, bspec],
        out_specs=qspec,
        out_shape=jax.ShapeDtypeStruct((b_, l_, WIDTH_B), BF16),
        compiler_params=pltpu.CompilerParams(vmem_limit_bytes=VMEM_LIMIT),
        name="neighborhood_attention",
    )(qb, kb, vb, bias)


def _bias_b(rpb):
    cols = np.arange(GRID_W)
    c_start = np.clip(cols - KW // 2, 0, GRID_W - KW)
    col_ok = (cols[None, :] >= c_start[:, None]) & (cols[None, :] < c_start[:, None] + KW)
    assert np.all(np.abs(cols[None, :] - cols[:, None])[col_ok] <= KW - 1)
    pad = GRID_W - KW
    toep = _toeplitz(jnp.pad(rpb.astype(F32) * LOG2E, ((0, 0), (0, 0), (pad, pad))), GRID_W, GRID_W)
    toep = jnp.where(col_ok[None, None], toep, NEG)
    out = []
    for v in range(KH_MAX):
        bias = jnp.stack([toep[:, kr - v + KH_MAX - 1] for kr in range(KH_MAX)], axis=2)
        out.append(bias.reshape(WIDTH_B // LANE_TILE, HEADS_PER_GROUP * GRID_W, KB_B))
    return jnp.stack(out, axis=0)


def _post_kernel(x_ref, mod_ref, n1_ref, n2_ref, oa0_ref, oa1_ref, oa2_ref, l0_ref, l1_ref, l2_ref, ob_ref,
                 wg_ref, bg_ref, wua_ref, wub_ref, wo_ref, w1_ref, w2_ref, out_ref, *scr):
    t = x_ref.shape[0]
    x = x_ref[...]
    sh1, sc1, g1 = mod_ref[0:1, :], mod_ref[1:2, :], mod_ref[2:3, :]
    sh2, sc2, g2 = mod_ref[3:4, :], mod_ref[4:5, :], mod_ref[5:6, :]
    h = (_rms(x, n1_ref[...]) * (1.0 + sc1) + sh1).astype(BF16)
    gate = jax.nn.sigmoid(jnp.dot(h, wg_ref[...], preferred_element_type=F32) + bg_ref[...])

    def token_order(ref, scr_refs):
        d = ref.shape[0]
        if d == 1:
            return ref[0].astype(F32)
        for r in range(d):
            for s_ref, lanes in zip(scr_refs, LANE_HALVES):
                s_ref[pl.ds(r, t // d, stride=d), :] = ref[r, :, lanes].astype(F32)
        return jnp.concatenate([s_ref[...] for s_ref in scr_refs], axis=-1)

    oa0, oa1, oa2 = token_order(oa0_ref, None), token_order(oa1_ref, scr[0:2]), token_order(oa2_ref, scr[2:4])
    l0, l1, l2 = token_order(l0_ref, None), token_order(l1_ref, scr[4:6]), token_order(l2_ref, scr[6:8])
    mx = jnp.maximum(jnp.maximum(l0, l1), l2)
    e0, e1, e2 = jnp.exp(l0 - mx), jnp.exp(l1 - mx), jnp.exp(l2 - mx)
    inv = 1.0 / (e0 + e1 + e2)
    oa = (e0 * inv) * oa0 + (e1 * inv) * oa1 + (e2 * inv) * oa2

    ua = jnp.dot(oa.astype(BF16), wua_ref[...], preferred_element_type=F32)
    ub = jnp.dot(ob_ref[...], wub_ref[...], preferred_element_type=F32)
    mixed = gate[:, :D_MODEL] * ua + gate[:, D_MODEL:] * ub
    x1 = x + g1 * jnp.dot(mixed.astype(BF16), wo_ref[...], preferred_element_type=F32)

    h2 = (_rms(x1, n2_ref[...]) * (1.0 + sc2) + sh2).astype(BF16)
    acc = jnp.zeros_like(x1)
    for c in range(D_FF // FF_CHUNK):
        f = jnp.dot(h2, w1_ref[:, c * FF_CHUNK:(c + 1) * FF_CHUNK], preferred_element_type=F32)
        f = jnp.square(jnp.maximum(f, 0.0)).astype(BF16)
        acc = acc + jnp.dot(f, w2_ref[c * FF_CHUNK:(c + 1) * FF_CHUNK, :], preferred_element_type=F32)
    out_ref[...] = x1 + g2 * acc


def _post_attention(x, mod, n1, n2, oas, lses, ob, wg, bg, wua, wub, wo, w1, w2, layer):
    b_, l_, _ = x.shape
    t = TOK_TILE_POST
    per_b = l_ // t
    tok = lambda w: pl.BlockSpec((None, t, w), lambda i: (i // per_b, i % per_b, 0))
    dil_specs = []
    for _, d in DIL_GROUPS:
        assert t % (BF16_SUBLANES * d) == 0
        dil_specs.append(pl.BlockSpec((None, d, t // d, LANE_TILE), lambda i: (i // per_b, 0, i % per_b, 0)))
    return pl.pallas_call(
        _post_kernel,
        grid=(b_ * per_b,),
        in_specs=[
            tok(D_MODEL),
            pl.BlockSpec((None, None, N_MOD, D_MODEL), lambda i: (layer, i // per_b, 0, 0)),
            _layer_spec((1, D_MODEL), layer), _layer_spec((1, D_MODEL), layer),
            *dil_specs, *dil_specs,
            tok(WIDTH_B),
            _layer_spec((D_MODEL, 2 * D_MODEL), layer), _layer_spec((1, 2 * D_MODEL), layer),
            _layer_spec((WIDTH_A, D_MODEL), layer), _layer_spec((WIDTH_B, D_MODEL), layer),
            _layer_spec((D_MODEL, D_MODEL), layer),
            _layer_spec((D_MODEL, D_FF), layer), _layer_spec((D_FF, D_MODEL), layer),
        ],
        out_specs=tok(D_MODEL),
        out_shape=jax.ShapeDtypeStruct(x.shape, F32),
        scratch_shapes=[pltpu.VMEM((t, VREG_LANES), F32)] * 8,
        compiler_params=pltpu.CompilerParams(vmem_limit_bytes=VMEM_LIMIT),
        name="post_attention_mlp",
    )(x, mod, n1, n2, *oas, *lses, ob, wg, bg, wua, wub, wo, w1, w2)


def _head_sum_matrix():
    head = np.arange(LANE_TILE) // HEAD_DIM
    return jnp.asarray((head[:, None] == head[None, :]).astype(np.float32), dtype=BF16)


def kernel(x_prompt, x_sample, c_prompt, c_sample, norm1_g, norm2_g, w_mod, b_mod, w_in, q_norm_g, k_norm_g,
           rel_bias, rpb, w_gate, b_gate, w_up_a, w_up_b, w_o, w_ff1, w_ff2):
    n_prompt = c_prompt.shape[0]
    mod_all = _modulation(jnp.concatenate([c_prompt, c_sample], axis=0), w_mod, b_mod)
    mod_all = mod_all.reshape(DEPTH, -1, N_MOD, D_MODEL)
    ones = _head_sum_matrix()
    qg = (q_norm_g.reshape(DEPTH, 1, QKV_PART) * (HEAD_DIM ** -0.5 * LOG2E)).astype(F32)
    kg = k_norm_g.reshape(DEPTH, 1, QKV_PART).astype(F32)
    n1 = norm1_g.reshape(DEPTH, 1, D_MODEL)
    n2 = norm2_g.reshape(DEPTH, 1, D_MODEL)
    bg = b_gate.reshape(DEPTH, 1, 2 * D_MODEL)
    w_in_b, w_gate_b = w_in.astype(BF16), w_gate.astype(BF16)
    w_up_a_b, w_up_b_b, w_o_b = w_up_a.astype(BF16), w_up_b.astype(BF16), w_o.astype(BF16)
    w_ff1_b, w_ff2_b = w_ff1.astype(BF16), w_ff2.astype(BF16)

    xs = (x_prompt, x_sample)
    bias_b = [_bias_b(rpb[l]) for l in range(DEPTH)]
    bias_a = {}
    for x in xs:
        for g, (w, d) in enumerate(DIL_GROUPS):
            n = x.shape[1] // d
            if (g, min(2 * QB, n)) not in bias_a:
                bias_a[(g, min(2 * QB, n))] = _bias_a(rel_bias, g, w, d, n)

    ys = []
    for x, mod_x in ((x_prompt, mod_all[:, :n_prompt]), (x_sample, mod_all[:, n_prompt:])):
        l_ = x.shape[1]
        for l in range(DEPTH):
            q, k, v = _pre_attention(x, mod_x, n1, w_in_b, qg, kg, ones, l)
            oas, lses = [], []
            for g, (w, d) in enumerate(DIL_GROUPS):
                o, lse = _dilated_group(q[g], k[g], v[g], bias_a[(g, min(2 * QB, l_ // d))], d)
                oas.append(o)
                lses.append(lse)
            ob = _neighborhood(q[N_GROUPS], k[N_GROUPS], v[N_GROUPS], bias_b[l])
            x = _post_attention(x, mod_x, n1, n2, oas, lses, ob, w_gate_b, bg, w_up_a_b, w_up_b_b, w_o_b,
                                w_ff1_b, w_ff2_b, l)
        ys.append(x)
    return tuple(ys)
```

```python
import functools

import jax
import jax.numpy as jnp
import numpy as np
from jax import lax
from jax.experimental import pallas as pl
from jax.experimental.pallas import tpu as pltpu

D_MODEL = 1024
DEPTH = 2
HEAD_DIM = 64
DIL_GROUPS = ((128, 1), (512, 4), (2048, 16))
HEADS_PER_GROUP = 4
N_GROUPS = len(DIL_GROUPS)
N_HEADS_A = HEADS_PER_GROUP * N_GROUPS
N_HEADS_B = 8
N_HEADS = N_HEADS_A + N_HEADS_B
WIDTH_A = HEADS_PER_GROUP * HEAD_DIM
WIDTH_B = N_HEADS_B * HEAD_DIM
QKV_PART = N_HEADS * HEAD_DIM
N_BUCKETS = 32
MAX_DISTANCE = 1024
GRID_W = 64
KH_MAX = 8
KW = 16
D_FF = 4 * D_MODEL
N_MOD = 6
EPS = 1e-6
NEG = -1e30
LOG2E = 1.4426950408889634
LN2 = 0.6931471805599453

LANE_TILE = 256
QB = 128
HALO_A = 64
KB_B = KH_MAX * GRID_W
ROWS_PER_STEP_B = 32
MAX_CHUNK_A = 2048
INTERLEAVE_B = 16
INTERLEAVE = 16
TOK_TILE_PRE = 512
TOK_TILE_POST = 512
FF_CHUNK = 1024
BF16_SUBLANES = 16
VREG_LANES = 128
LANE_HALVES = (slice(0, VREG_LANES), slice(VREG_LANES, LANE_TILE))
VMEM_LIMIT = 56 * 1024 * 1024

F32 = jnp.float32
BF16 = jnp.bfloat16


def _rms(x, g):
    return x * lax.rsqrt(jnp.mean(x * x, axis=-1, keepdims=True) + EPS) * g


def _const_spec(shape):
    return pl.BlockSpec(shape, lambda *_: (0,) * len(shape), pipeline_mode=pl.Buffered(1))


def _layer_spec(shape, layer):
    return pl.BlockSpec((None,) + shape, lambda *_: (layer,) + (0,) * len(shape), pipeline_mode=pl.Buffered(1))


def _mod_kernel(c_ref, w_ref, b_ref, o_ref):
    c = c_ref[...]
    s = c * jax.nn.sigmoid(c)
    o_ref[...] = jnp.dot(s, w_ref[...], preferred_element_type=F32) + b_ref[...]


def _modulation(c_all, w_mod, b_mod):
    nb = c_all.shape[0]
    ncol = N_MOD * D_MODEL
    blk = 1536
    return pl.pallas_call(
        _mod_kernel,
        grid=(DEPTH, ncol // blk),
        in_specs=[
            pl.BlockSpec((nb, D_MODEL), lambda l, j: (0, 0)),
            pl.BlockSpec((None, D_MODEL, blk), lambda l, j: (l, 0, j)),
            pl.BlockSpec((None, 1, blk), lambda l, j: (l, 0, j)),
        ],
        out_specs=pl.BlockSpec((None, nb, blk), lambda l, j: (l, 0, j)),
        out_shape=jax.ShapeDtypeStruct((DEPTH, nb, ncol), F32),
        compiler_params=pltpu.CompilerParams(vmem_limit_bytes=VMEM_LIMIT),
        name="modulation",
    )(c_all, w_mod, b_mod.reshape(DEPTH, 1, ncol))


def _pre_kernel(x_ref, mod_ref, n1_ref, win_ref, qg_ref, kg_ref, ones_ref, *refs):
    scr_refs = refs[-2:]
    outs = refs[:-2]
    t = x_ref.shape[0]
    x = x_ref[...]
    sh1 = mod_ref[0:1, :]
    sc1 = mod_ref[1:2, :]
    h = (_rms(x, n1_ref[...]) * (1.0 + sc1) + sh1).astype(BF16)
    ones = ones_ref[...]
    n_chunk = QKV_PART // LANE_TILE
    per_part = N_GROUPS + 1

    for part, g_ref in enumerate((qg_ref, kg_ref, None)):
        y_part = jnp.dot(h, win_ref[:, part * QKV_PART:(part + 1) * QKV_PART], preferred_element_type=F32)
        for c in range(n_chunk):
            y = y_part[:, c * LANE_TILE:(c + 1) * LANE_TILE]
            if g_ref is not None:
                ss = jnp.dot((y * y).astype(BF16), ones, preferred_element_type=F32)
                y = y * lax.rsqrt(ss * (1.0 / HEAD_DIM) + EPS) * g_ref[:, c * LANE_TILE:(c + 1) * LANE_TILE]
            if c < N_GROUPS:
                out = outs[part * per_part + c]
                d = DIL_GROUPS[c][1]
                if d == 1:
                    out[0] = y.astype(out.dtype)
                else:
                    for s_ref, lanes in zip(scr_refs, LANE_HALVES):
                        s_ref[...] = y[:, lanes]
                    for r in range(d):
                        for s_ref, lanes in zip(scr_refs, LANE_HALVES):
                            out[r, :, lanes] = s_ref[pl.ds(r, t // d, stride=d), :].astype(out.dtype)
            else:
                out = outs[part * per_part + N_GROUPS]
                cb = c - N_GROUPS
                out[:, cb * LANE_TILE:(cb + 1) * LANE_TILE] = y.astype(out.dtype)


def _pre_attention(x, mod, n1, w_in, qg, kg, ones, layer):
    b_, l_, _ = x.shape
    t = TOK_TILE_PRE
    per_b = l_ // t
    tok = lambda w: pl.BlockSpec((None, t, w), lambda i: (i // per_b, i % per_b, 0))
    out_specs, out_shape = [], []
    for _ in range(3):
        for _, d in DIL_GROUPS:
            assert t % (BF16_SUBLANES * d) == 0
            out_specs.append(pl.BlockSpec((None, d, t // d, LANE_TILE), lambda i: (i // per_b, 0, i % per_b, 0)))
            out_shape.append(jax.ShapeDtypeStruct((b_, d, l_ // d, LANE_TILE), BF16))
        out_specs.append(tok(WIDTH_B))
        out_shape.append(jax.ShapeDtypeStruct((b_, l_, WIDTH_B), BF16))
    outs = pl.pallas_call(
        _pre_kernel,
        grid=(b_ * per_b,),
        in_specs=[
            tok(D_MODEL),
            pl.BlockSpec((None, None, N_MOD, D_MODEL), lambda i: (layer, i // per_b, 0, 0)),
            _layer_spec((1, D_MODEL), layer),
            _layer_spec((D_MODEL, 3 * QKV_PART), layer),
            _layer_spec((1, QKV_PART), layer),
            _layer_spec((1, QKV_PART), layer),
            _const_spec((LANE_TILE, LANE_TILE)),
        ],
        out_specs=out_specs,
        out_shape=out_shape,
        scratch_shapes=[pltpu.VMEM((t, VREG_LANES), F32)] * 2,
        compiler_params=pltpu.CompilerParams(vmem_limit_bytes=VMEM_LIMIT),
        name="pre_attention",
    )(x, mod, n1, w_in, qg, kg, ones)
    per_part = N_GROUPS + 1
    return [outs[p * per_part:(p + 1) * per_part] for p in range(3)]


def _attn_core(q, k, v, bias):
    nq = q.shape[0]
    head_of_lane = lax.broadcasted_iota(jnp.int32, (nq, LANE_TILE), 1) // HEAD_DIM
    zero = jnp.zeros_like(q)
    qs = jnp.concatenate([jnp.where(head_of_lane == h, q, zero) for h in range(HEADS_PER_GROUP)], axis=0)
    s = lax.dot_general(qs, k, (((1,), (1,)), ((), ())), preferred_element_type=F32) + bias
    m = jnp.max(s, axis=-1, keepdims=True)
    p = jnp.exp2(s - m)
    l = jnp.sum(p, axis=-1, keepdims=True)
    o4 = jnp.dot(p.astype(BF16), v, preferred_element_type=F32)

    def unstack(x4):
        x = jnp.broadcast_to(x4[:nq, :], (nq, LANE_TILE))
        for h in range(1, HEADS_PER_GROUP):
            x = jnp.where(head_of_lane == h, x4[h * nq:(h + 1) * nq, :], x)
        return x

    l_dense = unstack(l)
    o = unstack(o4) * (1.0 / l_dense)
    lse = (unstack(m) + jnp.log2(l_dense)) * LN2
    return o, lse


def _toeplitz(f, nq, nk):
    p = nq + nk - 1
    assert f.shape[-1] == p and nq >= 2
    u = jnp.roll(f, -(nq - 1), axis=-1)
    flat = jnp.tile(u, nq)[..., :nq * (p - 1)]
    return flat.reshape(f.shape[:-1] + (nq, p - 1))[..., :nk]


def _attn_a_kernel(q_ref, k_ref, v_ref, bias_ref, o_ref, lse_ref, *, n, kb):
    i = pl.program_id(2)
    n_res, chunk, _ = q_ref.shape
    blocks_per_chunk = chunk // QB
    last_block = n // QB - 1

    def one_block(item):
        res, j = (0, item) if n_res == 1 else (item // blocks_per_chunk, item % blocks_per_chunk)
        blk = i * blocks_per_chunk + j
        start = pl.multiple_of(jnp.clip(blk * QB - HALO_A, 0, n - kb), HALO_A)
        var = jnp.where(blk == 0, 0, jnp.where(blk == last_block, 2, 1))
        rows = pl.ds(pl.multiple_of(j * QB, QB), QB)
        o, lse = _attn_core(q_ref[res, rows, :], k_ref[res, pl.ds(start, kb), :], v_ref[res, pl.ds(start, kb), :],
                            bias_ref[var])
        o_ref[res, rows, :] = o.astype(o_ref.dtype)
        lse_ref[res, rows, :] = lse

    def body(t, carry):
        for u in range(INTERLEAVE):
            one_block(t * INTERLEAVE + u)
        return carry

    lax.fori_loop(0, n_res * blocks_per_chunk // INTERLEAVE, body, 0)


def _dilated_group(q, k, v, bias, d):
    b_, d_, n, _ = q.shape
    assert d_ == d
    kb = bias.shape[-1]
    chunk = min(n, MAX_CHUNK_A)
    n_res = min(d, MAX_CHUNK_A // chunk)
    assert n % QB == 0 and n % chunk == 0 and kb == min(2 * QB, n) and (n == QB or n >= 2 * QB)
    assert d % n_res == 0 and (n_res * chunk // QB) % INTERLEAVE == 0
    qspec = pl.BlockSpec((None, n_res, chunk, LANE_TILE), lambda b, r, i: (b, r, i, 0))
    kvspec = pl.BlockSpec((None, n_res, n, LANE_TILE), lambda b, r, i: (b, r, 0, 0))
    return pl.pallas_call(
        functools.partial(_attn_a_kernel, n=n, kb=kb),
        grid=(b_, d // n_res, n // chunk),
        in_specs=[qspec, kvspec, kvspec, pl.BlockSpec(bias.shape, lambda b, r, i: (0, 0, 0))],
        out_specs=[qspec, qspec],
        out_shape=[jax.ShapeDtypeStruct(q.shape, BF16), jax.ShapeDtypeStruct(q.shape, F32)],
        compiler_params=pltpu.CompilerParams(vmem_limit_bytes=VMEM_LIMIT),
        name=f"dilated_attention_d{d}",
    )(q, k, v, bias)


def _t5_bucket(rel):
    nb = N_BUCKETS // 2
    max_exact = nb // 2
    ret = (rel > 0).astype(np.int32) * nb
    n = np.abs(rel)
    large = max_exact + (np.log(np.maximum(n, 1) / max_exact) / np.log(MAX_DISTANCE / max_exact) * (nb - max_exact)).astype(np.int32)
    large = np.minimum(large, nb - 1)
    return (ret + np.where(n < max_exact, n, large)).astype(np.int32)


def _bias_a(table, g, w, d, n):
    half = w // (2 * d)
    assert half == HALO_A
    kb = min(2 * QB, n)
    sl = slice(g * HEADS_PER_GROUP, (g + 1) * HEADS_PER_GROUP)
    out = []
    for off in (0, -HALO_A, QB - kb):
        rel = np.arange(QB + kb - 1) - (QB - 1) + off
        f = jnp.where((np.abs(rel) <= half)[None, :], table[_t5_bucket(d * rel)][:, sl].T.astype(F32) * LOG2E, NEG)
        out.append(_toeplitz(f, QB, kb).reshape(HEADS_PER_GROUP * QB, kb))
    return jnp.stack(out, axis=0)


def _attn_b_kernel(q_ref, k_ref, v_ref, bias_ref, o_ref, *, rows):
    i = pl.program_id(2)

    def one_row(a):
        r = i * ROWS_PER_STEP_B + a
        band = jnp.clip(r - KH_MAX // 2, 0, rows - KH_MAX)
        start = pl.multiple_of(band * GRID_W, GRID_W)
        qrows = pl.ds(pl.multiple_of(a * GRID_W, GRID_W), GRID_W)
        o, _ = _attn_core(q_ref[qrows, :], k_ref[pl.ds(start, KB_B), :], v_ref[pl.ds(start, KB_B), :],
                          bias_ref[r - band])
        o_ref[qrows, :] = o.astype(o_ref.dtype)

    def body(t, carry):
        for u in range(INTERLEAVE_B):
            one_row(t * INTERLEAVE_B + u)
        return carry

    lax.fori_loop(0, ROWS_PER_STEP_B // INTERLEAVE_B, body, 0)


def _neighborhood(qb, kb, vb, bias):
    b_, l_, _ = qb.shape
    rows = l_ // GRID_W
    assert rows % ROWS_PER_STEP_B == 0 and rows >= KH_MAX
    halves = WIDTH_B // LANE_TILE
    qspec = pl.BlockSpec((None, ROWS_PER_STEP_B * GRID_W, LANE_TILE), lambda hf, b, i: (b, i, hf))
    kvspec = pl.BlockSpec((None, l_, LANE_TILE), lambda hf, b, i: (b, 0, hf))
    bspec = pl.BlockSpec((KH_MAX, None, HEADS_PER_GROUP * GRID_W, KB_B), lambda hf, b, i: (0, hf, 0, 0))
    return pl.pallas_call(
        functools.partial(_attn_b_kernel, rows=rows),
        grid=(halves, b_, rows // ROWS_PER_STEP_B),
        in_specs=[qspec, kvspec, kvspec, bspec],
        out_specs=qspec,
        out_shape=jax.ShapeDtypeStruct((b_, l_, WIDTH_B), BF16),
        compiler_params=pltpu.CompilerParams(vmem_limit_bytes=VMEM_LIMIT),
        name="neighborhood_attention",
    )(qb, kb, vb, bias)


def _bias_b(rpb):
    cols = np.arange(GRID_W)
    c_start = np.clip(cols - KW // 2, 0, GRID_W - KW)
    col_ok = (cols[None, :] >= c_start[:, None]) & (cols[None, :] < c_start[:, None] + KW)
    assert np.all(np.abs(cols[None, :] - cols[:, None])[col_ok] <= KW - 1)
    pad = GRID_W - KW
    toep = _toeplitz(jnp.pad(rpb.astype(F32) * LOG2E, ((0, 0), (0, 0), (pad, pad))), GRID_W, GRID_W)
    toep = jnp.where(col_ok[None, None], toep, NEG)
    out = []
    for v in range(KH_MAX):
        bias = jnp.stack([toep[:, kr - v + KH_MAX - 1] for kr in range(KH_MAX)], axis=2)
        out.append(bias.reshape(WIDTH_B // LANE_TILE, HEADS_PER_GROUP * GRID_W, KB_B))
    return jnp.stack(out, axis=0)


def _post_kernel(x_ref, mod_ref, n1_ref, n2_ref, oa0_ref, oa1_ref, oa2_ref, l0_ref, l1_ref, l2_ref, ob_ref,
                 wg_ref, bg_ref, wua_ref, wub_ref, wo_ref, w1_ref, w2_ref, out_ref, *scr):
    t = x_ref.shape[0]
    x = x_ref[...]
    sh1, sc1, g1 = mod_ref[0:1, :], mod_ref[1:2, :], mod_ref[2:3, :]
    sh2, sc2, g2 = mod_ref[3:4, :], mod_ref[4:5, :], mod_ref[5:6, :]
    h = (_rms(x, n1_ref[...]) * (1.0 + sc1) + sh1).astype(BF16)
    gate = jax.nn.sigmoid(jnp.dot(h, wg_ref[...], preferred_element_type=F32) + bg_ref[...])

    def token_order(ref, scr_refs):
        d = ref.shape[0]
        if d == 1:
            return ref[0].astype(F32)
        for r in range(d):
            for s_ref, lanes in zip(scr_refs, LANE_HALVES):
                s_ref[pl.ds(r, t // d, stride=d), :] = ref[r, :, lanes].astype(F32)
        return jnp.concatenate([s_ref[...] for s_ref in scr_refs], axis=-1)

    oa0, oa1, oa2 = token_order(oa0_ref, None), token_order(oa1_ref, scr[0:2]), token_order(oa2_ref, scr[2:4])
    l0, l1, l2 = token_order(l0_ref, None), token_order(l1_ref, scr[4:6]), token_order(l2_ref, scr[6:8])
    mx = jnp.maximum(jnp.maximum(l0, l1), l2)
    e0, e1, e2 = jnp.exp(l0 - mx), jnp.exp(l1 - mx), jnp.exp(l2 - mx)
    inv = 1.0 / (e0 + e1 + e2)
    oa = (e0 * inv) * oa0 + (e1 * inv) * oa1 + (e2 * inv) * oa2

    ua = jnp.dot(oa.astype(BF16), wua_ref[...], preferred_element_type=F32)
    ub = jnp.dot(ob_ref[...], wub_ref[...], preferred_element_type=F32)
    mixed = gate[:, :D_MODEL] * ua + gate[:, D_MODEL:] * ub
    x1 = x + g1 * jnp.dot(mixed.astype(BF16), wo_ref[...], preferred_element_type=F32)

    h2 = (_rms(x1, n2_ref[...]) * (1.0 + sc2) + sh2).astype(BF16)
    acc = jnp.zeros_like(x1)
    for c in range(D_FF // FF_CHUNK):
        f = jnp.dot(h2, w1_ref[:, c * FF_CHUNK:(c + 1) * FF_CHUNK], preferred_element_type=F32)
        f = jnp.square(jnp.maximum(f, 0.0)).astype(BF16)
        acc = acc + jnp.dot(f, w2_ref[c * FF_CHUNK:(c + 1) * FF_CHUNK, :], preferred_element_type=F32)
    out_ref[...] = x1 + g2 * acc


def _post_attention(x, mod, n1, n2, oas, lses, ob, wg, bg, wua, wub, wo, w1, w2, layer):
    b_, l_, _ = x.shape
    t = TOK_TILE_POST
    per_b = l_ // t
    tok = lambda w: pl.BlockSpec((None, t, w), lambda i: (i // per_b, i % per_b, 0))
    dil_specs = []
    for _, d in DIL_GROUPS:
        assert t % (BF16_SUBLANES * d) == 0
        dil_specs.append(pl.BlockSpec((None, d, t // d, LANE_TILE), lambda i: (i // per_b, 0, i % per_b, 0)))
    return pl.pallas_call(
        _post_kernel,
        grid=(b_ * per_b,),
        in_specs=[
            tok(D_MODEL),
            pl.BlockSpec((None, None, N_MOD, D_MODEL), lambda i: (layer, i // per_b, 0, 0)),
            _layer_spec((1, D_MODEL), layer), _layer_spec((1, D_MODEL), layer),
            *dil_specs, *dil_specs,
            tok(WIDTH_B),
            _layer_spec((D_MODEL, 2 * D_MODEL), layer), _layer_spec((1, 2 * D_MODEL), layer),
            _layer_spec((WIDTH_A, D_MODEL), layer), _layer_spec((WIDTH_B, D_MODEL), layer),
            _layer_spec((D_MODEL, D_MODEL), layer),
            _layer_spec((D_MODEL, D_FF), layer), _layer_spec((D_FF, D_MODEL), layer),
        ],
        out_specs=tok(D_MODEL),
        out_shape=jax.ShapeDtypeStruct(x.shape, F32),
        scratch_shapes=[pltpu.VMEM((t, VREG_LANES), F32)] * 8,
        compiler_params=pltpu.CompilerParams(vmem_limit_bytes=VMEM_LIMIT),
        name="post_attention_mlp",
    )(x, mod, n1, n2, *oas, *lses, ob, wg, bg, wua, wub, wo, w1, w2)


def _head_sum_matrix():
    head = np.arange(LANE_TILE) // HEAD_DIM
    return jnp.asarray((head[:, None] == head[None, :]).astype(np.float32), dtype=BF16)


def kernel(x_prompt, x_sample, c_prompt, c_sample, norm1_g, norm2_g, w_mod, b_mod, w_in, q_norm_g, k_norm_g,
           rel_bias, rpb, w_gate, b_gate, w_up_a, w_up_b, w_o, w_ff1, w_ff2):
    n_prompt = c_prompt.shape[0]
    mod_all = _modulation(jnp.concatenate([c_prompt, c_sample], axis=0), w_mod, b_mod)
    mod_all = mod_all.reshape(DEPTH, -1, N_MOD, D_MODEL)
    ones = _head_sum_matrix()
    qg = (q_norm_g.reshape(DEPTH, 1, QKV_PART) * (HEAD_DIM ** -0.5 * LOG2E)).astype(F32)
    kg = k_norm_g.reshape(DEPTH, 1, QKV_PART).astype(F32)
    n1 = norm1_g.reshape(DEPTH, 1, D_MODEL)
    n2 = norm2_g.reshape(DEPTH, 1, D_MODEL)
    bg = b_gate.reshape(DEPTH, 1, 2 * D_MODEL)
    w_in_b, w_gate_b = w_in.astype(BF16), w_gate.astype(BF16)
    w_up_a_b, w_up_b_b, w_o_b = w_up_a.astype(BF16), w_up_b.astype(BF16), w_o.astype(BF16)
    w_ff1_b, w_ff2_b = w_ff1.astype(BF16), w_ff2.astype(BF16)

    xs = (x_prompt, x_sample)
    bias_b = [_bias_b(rpb[l]) for l in range(DEPTH)]
    bias_a = {}
    for x in xs:
        for g, (w, d) in enumerate(DIL_GROUPS):
            n = x.shape[1] // d
            if (g, min(2 * QB, n)) not in bias_a:
                bias_a[(g, min(2 * QB, n))] = _bias_a(rel_bias, g, w, d, n)

    ys = []
    for x, mod_x in ((x_prompt, mod_all[:, :n_prompt]), (x_sample, mod_all[:, n_prompt:])):
        l_ = x.shape[1]
        for l in range(DEPTH):
            q, k, v = _pre_attention(x, mod_x, n1, w_in_b, qg, kg, ones, l)
            oas, lses = [], []
            for g, (w, d) in enumerate(DIL_GROUPS):
                o, lse = _dilated_group(q[g], k[g], v[g], bias_a[(g, min(2 * QB, l_ // d))], d)
                oas.append(o)
                lses.append(lse)
            ob = _neighborhood(q[N_GROUPS], k[N_GROUPS], v[N_GROUPS], bias_b[l])
            x = _post_attention(x, mod_x, n1, n2, oas, lses, ob, w_gate_b, bg, w_up_a_b, w_up_b_b, w_o_b,
                                w_ff1_b, w_ff2_b, l)
        ys.append(x)
    return tuple(ys)
```

```python
import functools

import jax
import jax.numpy as jnp
import numpy as np
from jax import lax
from jax.experimental import pallas as pl
from jax.experimental.pallas import tpu as pltpu

D_MODEL = 1024
DEPTH = 2
HEAD_DIM = 64
DIL_GROUPS = ((128, 1), (512, 4), (2048, 16))
HEADS_PER_GROUP = 4
N_GROUPS = len(DIL_GROUPS)
N_HEADS_A = HEADS_PER_GROUP * N_GROUPS
N_HEADS_B = 8
N_HEADS = N_HEADS_A + N_HEADS_B
WIDTH_A = HEADS_PER_GROUP * HEAD_DIM
WIDTH_B = N_HEADS_B * HEAD_DIM
QKV_PART = N_HEADS * HEAD_DIM
N_BUCKETS = 32
MAX_DISTANCE = 1024
GRID_W = 64
KH_MAX = 8
KW = 16
D_FF = 4 * D_MODEL
N_MOD = 6
EPS = 1e-6
NEG = -1e30
LOG2E = 1.4426950408889634
LN2 = 0.6931471805599453

LANE_TILE = 256
QB = 128
HALO_A = 64
KB_B = KH_MAX * GRID_W
ROWS_PER_STEP_B = 32
MAX_CHUNK_A = 2048
INTERLEAVE_B = 16
INTERLEAVE = 16
TOK_TILE_PRE = 1024
TOK_TILE_POST = 512
FF_CHUNK = 1024
BF16_SUBLANES = 16
VREG_LANES = 128
LANE_HALVES = (slice(0, VREG_LANES), slice(VREG_LANES, LANE_TILE))
VMEM_LIMIT = 56 * 1024 * 1024

F32 = jnp.float32
BF16 = jnp.bfloat16


def _rms(x, g):
    return x * lax.rsqrt(jnp.mean(x * x, axis=-1, keepdims=True) + EPS) * g


def _const_spec(shape):
    return pl.BlockSpec(shape, lambda *_: (0,) * len(shape), pipeline_mode=pl.Buffered(1))


def _layer_spec(shape, layer):
    return pl.BlockSpec((None,) + shape, lambda *_: (layer,) + (0,) * len(shape), pipeline_mode=pl.Buffered(1))


def _mod_kernel(c_ref, w_ref, b_ref, o_ref):
    c = c_ref[...]
    s = c * jax.nn.sigmoid(c)
    o_ref[...] = jnp.dot(s, w_ref[...], preferred_element_type=F32) + b_ref[...]


def _modulation(c_all, w_mod, b_mod):
    nb = c_all.shape[0]
    ncol = N_MOD * D_MODEL
    blk = 1536
    return pl.pallas_call(
        _mod_kernel,
        grid=(DEPTH, ncol // blk),
        in_specs=[
            pl.BlockSpec((nb, D_MODEL), lambda l, j: (0, 0)),
            pl.BlockSpec((None, D_MODEL, blk), lambda l, j: (l, 0, j)),
            pl.BlockSpec((None, 1, blk), lambda l, j: (l, 0, j)),
        ],
        out_specs=pl.BlockSpec((None, nb, blk), lambda l, j: (l, 0, j)),
        out_shape=jax.ShapeDtypeStruct((DEPTH, nb, ncol), F32),
        compiler_params=pltpu.CompilerParams(vmem_limit_bytes=VMEM_LIMIT),
        name="modulation",
    )(c_all, w_mod, b_mod.reshape(DEPTH, 1, ncol))


def _pre_kernel(x_ref, mod_ref, n1_ref, win_ref, qg_ref, kg_ref, ones_ref, *refs):
    scr_refs = refs[-2:]
    outs = refs[:-2]
    t = x_ref.shape[0]
    x = x_ref[...]
    sh1 = mod_ref[0:1, :]
    sc1 = mod_ref[1:2, :]
    h = (_rms(x, n1_ref[...]) * (1.0 + sc1) + sh1).astype(BF16)
    ones = ones_ref[...]
    n_chunk = QKV_PART // LANE_TILE
    per_part = N_GROUPS + 1

    def project(part):
        return jnp.dot(h, win_ref[:, part * QKV_PART:(part + 1) * QKV_PART], preferred_element_type=F32)

    def finish(part, g_ref, y_part):
        for c in range(n_chunk):
            y = y_part[:, c * LANE_TILE:(c + 1) * LANE_TILE]
            if g_ref is not None:
                ss = jnp.dot((y * y).astype(BF16), ones, preferred_element_type=F32)
                y = y * lax.rsqrt(ss * (1.0 / HEAD_DIM) + EPS) * g_ref[:, c * LANE_TILE:(c + 1) * LANE_TILE]
            if c < N_GROUPS:
                out = outs[part * per_part + c]
                d = DIL_GROUPS[c][1]
                if d == 1:
                    out[0] = y.astype(out.dtype)
                else:
                    for s_ref, lanes in zip(scr_refs, LANE_HALVES):
                        s_ref[...] = y[:, lanes]
                    for r in range(d):
                        for s_ref, lanes in zip(scr_refs, LANE_HALVES):
                            out[r, :, lanes] = s_ref[pl.ds(r, t // d, stride=d), :].astype(out.dtype)
            else:
                out = outs[part * per_part + N_GROUPS]
                cb = c - N_GROUPS
                out[:, cb * LANE_TILE:(cb + 1) * LANE_TILE] = y.astype(out.dtype)

    for part, g_ref in enumerate((qg_ref, kg_ref, None)):
        finish(part, g_ref, project(part))


def _pre_attention(x, mod, n1, w_in, qg, kg, ones, layer):
    b_, l_, _ = x.shape
    t = TOK_TILE_PRE
    per_b = l_ // t
    tok = lambda w: pl.BlockSpec((None, t, w), lambda i: (i // per_b, i % per_b, 0))
    out_specs, out_shape = [], []
    for _ in range(3):
        for _, d in DIL_GROUPS:
            assert t % (BF16_SUBLANES * d) == 0
            out_specs.append(pl.BlockSpec((None, d, t // d, LANE_TILE), lambda i: (i // per_b, 0, i % per_b, 0)))
            out_shape.append(jax.ShapeDtypeStruct((b_, d, l_ // d, LANE_TILE), BF16))
        out_specs.append(tok(WIDTH_B))
        out_shape.append(jax.ShapeDtypeStruct((b_, l_, WIDTH_B), BF16))
    outs = pl.pallas_call(
        _pre_kernel,
        grid=(b_ * per_b,),
        in_specs=[
            tok(D_MODEL),
            pl.BlockSpec((None, None, N_MOD, D_MODEL), lambda i: (layer, i // per_b, 0, 0)),
            _layer_spec((1, D_MODEL), layer),
            _layer_spec((D_MODEL, 3 * QKV_PART), layer),
            _layer_spec((1, QKV_PART), layer),
            _layer_spec((1, QKV_PART), layer),
            _const_spec((LANE_TILE, LANE_TILE)),
        ],
        out_specs=out_specs,
        out_shape=out_shape,
        scratch_shapes=[pltpu.VMEM((t, VREG_LANES), F32)] * 2,
        compiler_params=pltpu.CompilerParams(vmem_limit_bytes=VMEM_LIMIT),
        name="pre_attention",
    )(x, mod, n1, w_in, qg, kg, ones)
    per_part = N_GROUPS + 1
    return [outs[p * per_part:(p + 1) * per_part] for p in range(3)]


def _attn_core(q, k, v, bias):
    nq = q.shape[0]
    head_of_lane = lax.broadcasted_iota(jnp.int32, (nq, LANE_TILE), 1) // HEAD_DIM
    zero = jnp.zeros_like(q)
    qs = jnp.concatenate([jnp.where(head_of_lane == h, q, zero) for h in range(HEADS_PER_GROUP)], axis=0)
    s = lax.dot_general(qs, k, (((1,), (1,)), ((), ())), preferred_element_type=F32) + bias
    m = jnp.max(s, axis=-1, keepdims=True)
    p = jnp.exp2(s - m)
    l = jnp.sum(p, axis=-1, keepdims=True)
    o4 = jnp.dot(p.astype(BF16), v, preferred_element_type=F32)

    def unstack(x4):
        x = jnp.broadcast_to(x4[:nq, :], (nq, LANE_TILE))
        for h in range(1, HEADS_PER_GROUP):
            x = jnp.where(head_of_lane == h, x4[h * nq:(h + 1) * nq, :], x)
        return x

    l_dense = unstack(l)
    o = unstack(o4) * (1.0 / l_dense)
    lse = unstack(m) * LN2 + jnp.log(l_dense)
    return o, lse


def _toeplitz(f, nq, nk):
    p = nq + nk - 1
    assert f.shape[-1] == p and nq >= 2
    u = jnp.roll(f, -(nq - 1), axis=-1)
    flat = jnp.tile(u, nq)[..., :nq * (p - 1)]
    return flat.reshape(f.shape[:-1] + (nq, p - 1))[..., :nk]


def _attn_a_kernel(q_ref, k_ref, v_ref, bias_ref, o_ref, lse_ref, *, n, kb):
    i = pl.program_id(2)
    n_res, chunk, _ = q_ref.shape
    blocks_per_chunk = chunk // QB
    last_block = n // QB - 1

    def one_block(item):
        res, j = (0, item) if n_res == 1 else (item // blocks_per_chunk, item % blocks_per_chunk)
        blk = i * blocks_per_chunk + j
        start = pl.multiple_of(jnp.clip(blk * QB - HALO_A, 0, n - kb), HALO_A)
        var = jnp.where(blk == 0, 0, jnp.where(blk == last_block, 2, 1))
        rows = pl.ds(pl.multiple_of(j * QB, QB), QB)
        o, lse = _attn_core(q_ref[res, rows, :], k_ref[res, pl.ds(start, kb), :], v_ref[res, pl.ds(start, kb), :],
                            bias_ref[var])
        o_ref[res, rows, :] = o.astype(o_ref.dtype)
        lse_ref[res, rows, :] = lse

    def body(t, carry):
        for u in range(INTERLEAVE):
            one_block(t * INTERLEAVE + u)
        return carry

    lax.fori_loop(0, n_res * blocks_per_chunk // INTERLEAVE, body, 0)


def _dilated_group(q, k, v, bias, d):
    b_, d_, n, _ = q.shape
    assert d_ == d
    kb = bias.shape[-1]
    chunk = min(n, MAX_CHUNK_A)
    n_res = min(d, MAX_CHUNK_A // chunk)
    assert n % QB == 0 and n % chunk == 0 and kb == min(2 * QB, n) and (n == QB or n >= 2 * QB)
    assert d % n_res == 0 and (n_res * chunk // QB) % INTERLEAVE == 0
    qspec = pl.BlockSpec((None, n_res, chunk, LANE_TILE), lambda b, r, i: (b, r, i, 0))
    kvspec = pl.BlockSpec((None, n_res, n, LANE_TILE), lambda b, r, i: (b, r, 0, 0))
    return pl.pallas_call(
        functools.partial(_attn_a_kernel, n=n, kb=kb),
        grid=(b_, d // n_res, n // chunk),
        in_specs=[qspec, kvspec, kvspec, pl.BlockSpec(bias.shape, lambda b, r, i: (0, 0, 0))],
        out_specs=[qspec, qspec],
        out_shape=[jax.ShapeDtypeStruct(q.shape, BF16), jax.ShapeDtypeStruct(q.shape, F32)],
        compiler_params=pltpu.CompilerParams(vmem_limit_bytes=VMEM_LIMIT),
        name=f"dilated_attention_d{d}",
    )(q, k, v, bias)


def _t5_bucket(rel):
    nb = N_BUCKETS // 2
    max_exact = nb // 2
    ret = (rel > 0).astype(np.int32) * nb
    n = np.abs(rel)
    large = max_exact + (np.log(np.maximum(n, 1) / max_exact) / np.log(MAX_DISTANCE / max_exact) * (nb - max_exact)).astype(np.int32)
    large = np.minimum(large, nb - 1)
    return (ret + np.where(n < max_exact, n, large)).astype(np.int32)


def _bias_a(table, g, w, d, n):
    half = w // (2 * d)
    assert half == HALO_A
    kb = min(2 * QB, n)
    sl = slice(g * HEADS_PER_GROUP, (g + 1) * HEADS_PER_GROUP)
    offs = np.array((0, -HALO_A, QB - kb))
    rel = np.arange(QB + kb - 1)[None, :] - (QB - 1) + offs[:, None]
    f = jnp.transpose(table[_t5_bucket(d * rel)][:, :, sl], (0, 2, 1)).astype(F32) * LOG2E
    f = jnp.where((np.abs(rel) <= half)[:, None, :], f, NEG)
    return _toeplitz(f, QB, kb).reshape(3, HEADS_PER_GROUP * QB, kb)


def _attn_b_kernel(q_ref, k_ref, v_ref, bias_ref, o_ref, *, rows):
    i = pl.program_id(2)

    def one_row(a):
        r = i * ROWS_PER_STEP_B + a
        band = jnp.clip(r - KH_MAX // 2, 0, rows - KH_MAX)
        start = pl.multiple_of(band * GRID_W, GRID_W)
        qrows = pl.ds(pl.multiple_of(a * GRID_W, GRID_W), GRID_W)
        o, _ = _attn_core(q_ref[qrows, :], k_ref[pl.ds(start, KB_B), :], v_ref[pl.ds(start, KB_B), :],
                          bias_ref[r - band])
        o_ref[qrows, :] = o.astype(o_ref.dtype)

    def body(t, carry):
        for u in range(INTERLEAVE_B):
            one_row(t * INTERLEAVE_B + u)
        return carry

    lax.fori_loop(0, ROWS_PER_STEP_B // INTERLEAVE_B, body, 0)


def _neighborhood(qb, kb, vb, bias):
    b_, l_, _ = qb.shape
    rows = l_ // GRID_W
    assert rows % ROWS_PER_STEP_B == 0 and rows >= KH_MAX
    halves = WIDTH_B // LANE_TILE
    qspec = pl.BlockSpec((None, ROWS_PER_STEP_B * GRID_W, LANE_TILE), lambda hf, b, i: (b, i, hf))
    kvspec = pl.BlockSpec((None, l_, LANE_TILE), lambda hf, b, i: (b, 0, hf))
    bspec = pl.BlockSpec((KH_MAX, None, HEADS_PER_GROUP * GRID_W, KB_B), lambda hf, b, i: (0, hf, 0, 0))
    return pl.pallas_call(
        functools.partial(_attn_b_kernel, rows=rows),
        grid=(halves, b_, rows // ROWS_PER_STEP_B),
        in_specs=[qspec, kvspec, kvspec, bspec],
        out_specs=qspec,
        out_shape=jax.ShapeDtypeStruct((b_, l_, WIDTH_B), BF16),
        compiler_params=pltpu.CompilerParams(vmem_limit_bytes=VMEM_LIMIT),
        name="neighborhood_attention",
    )(qb, kb, vb, bias)


def _bias_b(rpb):
    cols = np.arange(GRID_W)
    c_start = np.clip(cols - KW // 2, 0, GRID_W - KW)
    col_ok = (cols[None, :] >= c_start[:, None]) & (cols[None, :] < c_start[:, None] + KW)
    assert np.all(np.abs(cols[None, :] - cols[:, None])[col_ok] <= KW - 1)
    pad = GRID_W - KW
    toep = _toeplitz(jnp.pad(rpb.astype(F32) * LOG2E, ((0, 0), (0, 0), (pad, pad))), GRID_W, GRID_W)
    toep = jnp.transpose(jnp.where(col_ok[None, None], toep, NEG), (0, 2, 1, 3))
    out = [toep[:, :, KH_MAX - 1 - v:2 * KH_MAX - 1 - v].reshape(WIDTH_B // LANE_TILE, HEADS_PER_GROUP * GRID_W, KB_B)
           for v in range(KH_MAX)]
    return jnp.stack(out, axis=0)


def _post_kernel(x_ref, mod_ref, n1_ref, n2_ref, oa0_ref, oa1_ref, oa2_ref, l0_ref, l1_ref, l2_ref, ob_ref,
                 wg_ref, bg_ref, wua_ref, wub_ref, wo_ref, w1_ref, w2_ref, out_ref, *scr):
    t = x_ref.shape[0]
    x = x_ref[...]
    sh1, sc1, g1 = mod_ref[0:1, :], mod_ref[1:2, :], mod_ref[2:3, :]
    sh2, sc2, g2 = mod_ref[3:4, :], mod_ref[4:5, :], mod_ref[5:6, :]
    h = (_rms(x, n1_ref[...]) * (1.0 + sc1) + sh1).astype(BF16)
    gate = jax.nn.sigmoid(jnp.dot(h, wg_ref[...], preferred_element_type=F32) + bg_ref[...])

    def token_order(ref, scr_refs):
        d = ref.shape[0]
        if d == 1:
            return ref[0].astype(F32)
        for r in range(d):
            for s_ref, lanes in zip(scr_refs, LANE_HALVES):
                s_ref[pl.ds(r, t // d, stride=d), :] = ref[r, :, lanes].astype(F32)
        return jnp.concatenate([s_ref[...] for s_ref in scr_refs], axis=-1)

    oa0, oa1, oa2 = token_order(oa0_ref, None), token_order(oa1_ref, scr[0:2]), token_order(oa2_ref, scr[2:4])
    l0, l1, l2 = token_order(l0_ref, None), token_order(l1_ref, scr[4:6]), token_order(l2_ref, scr[6:8])
    mx = jnp.maximum(jnp.maximum(l0, l1), l2)
    e0, e1, e2 = jnp.exp(l0 - mx), jnp.exp(l1 - mx), jnp.exp(l2 - mx)
    inv = 1.0 / (e0 + e1 + e2)
    oa = (e0 * inv) * oa0 + (e1 * inv) * oa1 + (e2 * inv) * oa2

    ua = jnp.dot(oa.astype(BF16), wua_ref[...], preferred_element_type=F32)
    ub = jnp.dot(ob_ref[...], wub_ref[...], preferred_element_type=F32)
    mixed = gate[:, :D_MODEL] * ua + gate[:, D_MODEL:] * ub
    x1 = x + g1 * jnp.dot(mixed.astype(BF16), wo_ref[...], preferred_element_type=F32)

    h2 = (_rms(x1, n2_ref[...]) * (1.0 + sc2) + sh2).astype(BF16)
    acc = jnp.zeros_like(x1)
    for c in range(D_FF // FF_CHUNK):
        f = jnp.dot(h2, w1_ref[:, c * FF_CHUNK:(c + 1) * FF_CHUNK], preferred_element_type=F32)
        f = jnp.square(jnp.maximum(f, 0.0)).astype(BF16)
        acc = acc + jnp.dot(f, w2_ref[c * FF_CHUNK:(c + 1) * FF_CHUNK, :], preferred_element_type=F32)
    out_ref[...] = x1 + g2 * acc


def _post_attention(x, mod, n1, n2, oas, lses, ob, wg, bg, wua, wub, wo, w1, w2, layer):
    b_, l_, _ = x.shape
    t = TOK_TILE_POST
    per_b = l_ // t
    tok = lambda w: pl.BlockSpec((None, t, w), lambda i: (i // per_b, i % per_b, 0))
    dil_specs = []
    for _, d in DIL_GROUPS:
        assert t % (BF16_SUBLANES * d) == 0
        dil_specs.append(pl.BlockSpec((None, d, t // d, LANE_TILE), lambda i: (i // per_b, 0, i % per_b, 0)))
    return pl.pallas_call(
        _post_kernel,
        grid=(b_ * per_b,),
        in_specs=[
            tok(D_MODEL),
            pl.BlockSpec((None, None, N_MOD, D_MODEL), lambda i: (layer, i // per_b, 0, 0)),
            _layer_spec((1, D_MODEL), layer), _layer_spec((1, D_MODEL), layer),
            *dil_specs, *dil_specs,
            tok(WIDTH_B),
            _layer_spec((D_MODEL, 2 * D_MODEL), layer), _layer_spec((1, 2 * D_MODEL), layer),
            _layer_spec((WIDTH_A, D_MODEL), layer), _layer_spec((WIDTH_B, D_MODEL), layer),
            _layer_spec((D_MODEL, D_MODEL), layer),
            _layer_spec((D_MODEL, D_FF), layer), _layer_spec((D_FF, D_MODEL), layer),
        ],
        out_specs=tok(D_MODEL),
        out_shape=jax.ShapeDtypeStruct(x.shape, F32),
        scratch_shapes=[pltpu.VMEM((t, VREG_LANES), F32)] * 8,
        compiler_params=pltpu.CompilerParams(vmem_limit_bytes=VMEM_LIMIT),
        name="post_attention_mlp",
    )(x, mod, n1, n2, *oas, *lses, ob, wg, bg, wua, wub, wo, w1, w2)


def _head_sum_matrix():
    head = np.arange(LANE_TILE) // HEAD_DIM
    return jnp.asarray((head[:, None] == head[None, :]).astype(np.float32), dtype=BF16)


def kernel(x_prompt, x_sample, c_prompt, c_sample, norm1_g, norm2_g, w_mod, b_mod, w_in, q_norm_g, k_norm_g,
           rel_bias, rpb, w_gate, b_gate, w_up_a, w_up_b, w_o, w_ff1, w_ff2):
    n_prompt = c_prompt.shape[0]
    mod_all = _modulation(jnp.concatenate([c_prompt, c_sample], axis=0), w_mod, b_mod)
    mod_all = mod_all.reshape(DEPTH, -1, N_MOD, D_MODEL)
    ones = _head_sum_matrix()
    qg = (q_norm_g.reshape(DEPTH, 1, QKV_PART) * (HEAD_DIM ** -0.5 * LOG2E)).astype(F32)
    kg = k_norm_g.reshape(DEPTH, 1, QKV_PART).astype(F32)
    n1 = norm1_g.reshape(DEPTH, 1, D_MODEL)
    n2 = norm2_g.reshape(DEPTH, 1, D_MODEL)
    bg = b_gate.reshape(DEPTH, 1, 2 * D_MODEL)
    w_in_b, w_gate_b = w_in.astype(BF16), w_gate.astype(BF16)
    w_up_a_b, w_up_b_b, w_o_b = w_up_a.astype(BF16), w_up_b.astype(BF16), w_o.astype(BF16)
    w_ff1_b, w_ff2_b = w_ff1.astype(BF16), w_ff2.astype(BF16)

    xs = (x_prompt, x_sample)
    bias_b = [_bias_b(rpb[l]) for l in range(DEPTH)]
    bias_a = {}
    for x in xs:
        for g, (w, d) in enumerate(DIL_GROUPS):
            n = x.shape[1] // d
            if (g, min(2 * QB, n)) not in bias_a:
                bias_a[(g, min(2 * QB, n))] = _bias_a(rel_bias, g, w, d, n)

    ys = []
    for x, mod_x in ((x_prompt, mod_all[:, :n_prompt]), (x_sample, mod_all[:, n_prompt:])):
        l_ = x.shape[1]
        for l in range(DEPTH):
            q, k, v = _pre_attention(x, mod_x, n1, w_in_b, qg, kg, ones, l)
            oas, lses = [], []
            for g, (w, d) in enumerate(DIL_GROUPS):
                o, lse = _dilated_group(q[g], k[g], v[g], bias_a[(g, min(2 * QB, l_ // d))], d)
                oas.append(o)
                lses.append(lse)
            ob = _neighborhood(q[N_GROUPS], k[N_GROUPS], v[N_GROUPS], bias_b[l])
            x = _post_attention(x, mod_x, n1, n2, oas, lses, ob, w_gate_b, bg, w_up_a_b, w_up_b_b, w_o_b,
                                w_ff1_b, w_ff2_b, l)
        ys.append(x)
    return tuple(ys)
```

```python
import functools

import jax
import jax.numpy as jnp
import numpy as np
from jax import lax
from jax.experimental import pallas as pl
from jax.experimental.pallas import tpu as pltpu

D_MODEL = 1024
DEPTH = 2
HEAD_DIM = 64
DIL_GROUPS = ((128, 1), (512, 4), (2048, 16))
HEADS_PER_GROUP = 4
N_GROUPS = len(DIL_GROUPS)
N_HEADS_A = HEADS_PER_GROUP * N_GROUPS
N_HEADS_B = 8
N_HEADS = N_HEADS_A + N_HEADS_B
WIDTH_A = HEADS_PER_GROUP * HEAD_DIM
WIDTH_B = N_HEADS_B * HEAD_DIM
QKV_PART = N_HEADS * HEAD_DIM
N_BUCKETS = 32
MAX_DISTANCE = 1024
GRID_W = 64
KH_MAX = 8
KW = 16
D_FF = 4 * D_MODEL
N_MOD = 6
EPS = 1e-6
NEG = -1e30
LOG2E = 1.4426950408889634
LN2 = 0.6931471805599453

LANE_TILE = 256
QB = 128
HALO_A = 64
KB_B = KH_MAX * GRID_W
ROWS_PER_STEP_B = 32
MAX_CHUNK_A = 2048
INTERLEAVE_B = 32
INTERLEAVE = 16
TOK_TILE_PRE = 1024
TOK_TILE_POST = 512
FF_CHUNK = 1024
BF16_SUBLANES = 16
VREG_LANES = 128
LANE_HALVES = (slice(0, VREG_LANES), slice(VREG_LANES, LANE_TILE))
MOD_COL_BLOCK = 1536
VMEM_LIMIT = 56 * 1024 * 1024

F32 = jnp.float32
BF16 = jnp.bfloat16


def _rms(x, g):
    return x * lax.rsqrt(jnp.mean(x * x, axis=-1, keepdims=True) + EPS) * g


def _const_spec(shape):
    return pl.BlockSpec(shape, lambda *_: (0,) * len(shape), pipeline_mode=pl.Buffered(1))


def _layer_spec(shape, layer):
    return pl.BlockSpec((None,) + shape, lambda *_: (layer,) + (0,) * len(shape), pipeline_mode=pl.Buffered(1))


def _mod_kernel(c_ref, w_ref, b_ref, o_ref):
    c = c_ref[...]
    s = c * jax.nn.sigmoid(c)
    o_ref[...] = jnp.dot(s, w_ref[...], preferred_element_type=F32) + b_ref[...]


def _modulation(c_all, w_mod, b_mod):
    nb = c_all.shape[0]
    ncol = N_MOD * D_MODEL
    blk = MOD_COL_BLOCK
    assert ncol % blk == 0
    return pl.pallas_call(
        _mod_kernel,
        grid=(DEPTH, ncol // blk),
        in_specs=[
            pl.BlockSpec((nb, D_MODEL), lambda l, j: (0, 0)),
            pl.BlockSpec((None, D_MODEL, blk), lambda l, j: (l, 0, j)),
            pl.BlockSpec((None, 1, blk), lambda l, j: (l, 0, j)),
        ],
        out_specs=pl.BlockSpec((None, nb, blk), lambda l, j: (l, 0, j)),
        out_shape=jax.ShapeDtypeStruct((DEPTH, nb, ncol), F32),
        compiler_params=pltpu.CompilerParams(vmem_limit_bytes=VMEM_LIMIT),
        name="modulation",
    )(c_all, w_mod, b_mod.reshape(DEPTH, 1, ncol))


def _pre_kernel(x_ref, mod_ref, n1_ref, win_ref, qg_ref, kg_ref, ones_ref, *refs):
    scr_refs = refs[-2:]
    outs = refs[:-2]
    t = x_ref.shape[0]
    x = x_ref[...]
    sh1 = mod_ref[0:1, :]
    sc1 = mod_ref[1:2, :]
    h = (_rms(x, n1_ref[...]) * (1.0 + sc1) + sh1).astype(BF16)
    ones = ones_ref[...]
    n_chunk = QKV_PART // LANE_TILE
    per_part = N_GROUPS + 1

    def project(part):
        return jnp.dot(h, win_ref[:, part * QKV_PART:(part + 1) * QKV_PART], preferred_element_type=F32)

    def finish(part, g_ref, y_part):
        for c in range(n_chunk):
            y = y_part[:, c * LANE_TILE:(c + 1) * LANE_TILE]
            if g_ref is not None:
                ss = jnp.dot((y * y).astype(BF16), ones, preferred_element_type=F32)
                y = y * lax.rsqrt(ss * (1.0 / HEAD_DIM) + EPS) * g_ref[:, c * LANE_TILE:(c + 1) * LANE_TILE]
            if c < N_GROUPS:
                out = outs[part * per_part + c]
                d = DIL_GROUPS[c][1]
                if d == 1:
                    out[0] = y.astype(out.dtype)
                else:
                    for s_ref, lanes in zip(scr_refs, LANE_HALVES):
                        s_ref[...] = y[:, lanes]
                    for r in range(d):
                        for s_ref, lanes in zip(scr_refs, LANE_HALVES):
                            out[r, :, lanes] = s_ref[pl.ds(r, t // d, stride=d), :].astype(out.dtype)
            else:
                out = outs[part * per_part + N_GROUPS]
                cb = c - N_GROUPS
                out[:, cb * LANE_TILE:(cb + 1) * LANE_TILE] = y.astype(out.dtype)

    for part, g_ref in enumerate((qg_ref, kg_ref, None)):
        finish(part, g_ref, project(part))


def _pre_attention(x, mod, n1, w_in, qg, kg, ones, layer):
    b_, l_, _ = x.shape
    t = TOK_TILE_PRE
    per_b = l_ // t
    tok = lambda w: pl.BlockSpec((None, t, w), lambda i: (i // per_b, i % per_b, 0))
    out_specs, out_shape = [], []
    for _ in range(3):
        for _, d in DIL_GROUPS:
            assert t % (BF16_SUBLANES * d) == 0
            out_specs.append(pl.BlockSpec((None, d, t // d, LANE_TILE), lambda i: (i // per_b, 0, i % per_b, 0)))
            out_shape.append(jax.ShapeDtypeStruct((b_, d, l_ // d, LANE_TILE), BF16))
        out_specs.append(tok(WIDTH_B))
        out_shape.append(jax.ShapeDtypeStruct((b_, l_, WIDTH_B), BF16))
    outs = pl.pallas_call(
        _pre_kernel,
        grid=(b_ * per_b,),
        in_specs=[
            tok(D_MODEL),
            pl.BlockSpec((None, None, N_MOD, D_MODEL), lambda i: (layer, i // per_b, 0, 0)),
            _layer_spec((1, D_MODEL), layer),
            _layer_spec((D_MODEL, 3 * QKV_PART), layer),
            _layer_spec((1, QKV_PART), layer),
            _layer_spec((1, QKV_PART), layer),
            _const_spec((LANE_TILE, LANE_TILE)),
        ],
        out_specs=out_specs,
        out_shape=out_shape,
        scratch_shapes=[pltpu.VMEM((t, VREG_LANES), F32)] * 2,
        compiler_params=pltpu.CompilerParams(vmem_limit_bytes=VMEM_LIMIT),
        name="pre_attention",
    )(x, mod, n1, w_in, qg, kg, ones)
    per_part = N_GROUPS + 1
    return [outs[p * per_part:(p + 1) * per_part] for p in range(3)]


def _attn_core(q, k, v, bias):
    nq = q.shape[0]
    head_of_lane = lax.broadcasted_iota(jnp.int32, (nq, LANE_TILE), 1) // HEAD_DIM
    zero = jnp.zeros_like(q)
    qs = jnp.concatenate([jnp.where(head_of_lane == h, q, zero) for h in range(HEADS_PER_GROUP)], axis=0)
    s = lax.dot_general(qs, k, (((1,), (1,)), ((), ())), preferred_element_type=F32) + bias
    m = jnp.max(s, axis=-1, keepdims=True)
    p = jnp.exp2(s - m)
    l = jnp.sum(p, axis=-1, keepdims=True)
    o4 = jnp.dot(p.astype(BF16), v, preferred_element_type=F32)

    def unstack(x4):
        x = jnp.broadcast_to(x4[:nq, :], (nq, LANE_TILE))
        for h in range(1, HEADS_PER_GROUP):
            x = jnp.where(head_of_lane == h, x4[h * nq:(h + 1) * nq, :], x)
        return x

    l_dense = unstack(l)
    o = unstack(o4) * (1.0 / l_dense)
    lse = unstack(m) * LN2 + jnp.log(l_dense)
    return o, lse


def _toeplitz(f, nq, nk):
    p = nq + nk - 1
    assert f.shape[-1] == p and nq >= 2
    u = jnp.roll(f, -(nq - 1), axis=-1)
    flat = jnp.tile(u, nq)[..., :nq * (p - 1)]
    return flat.reshape(f.shape[:-1] + (nq, p - 1))[..., :nk]


def _attn_a_kernel(q_ref, k_ref, v_ref, bias_ref, o_ref, lse_ref, *, n, kb):
    i = pl.program_id(2)
    n_res, chunk, _ = q_ref.shape
    blocks_per_chunk = chunk // QB
    last_block = n // QB - 1

    def one_block(item):
        res, j = (0, item) if n_res == 1 else (item // blocks_per_chunk, item % blocks_per_chunk)
        blk = i * blocks_per_chunk + j
        start = pl.multiple_of(jnp.clip(blk * QB - HALO_A, 0, n - kb), HALO_A)
        var = jnp.where(blk == 0, 0, jnp.where(blk == last_block, 2, 1))
        rows = pl.ds(pl.multiple_of(j * QB, QB), QB)
        o, lse = _attn_core(q_ref[res, rows, :], k_ref[res, pl.ds(start, kb), :], v_ref[res, pl.ds(start, kb), :],
                            bias_ref[var])
        o_ref[res, rows, :] = o.astype(o_ref.dtype)
        lse_ref[res, rows, :] = lse

    def body(t, carry):
        for u in range(INTERLEAVE):
            one_block(t * INTERLEAVE + u)
        return carry

    lax.fori_loop(0, n_res * blocks_per_chunk // INTERLEAVE, body, 0)


def _dilated_group(q, k, v, bias, g):
    b_, d, n, _ = q.shape
    assert d == DIL_GROUPS[g][1]
    kb = bias.shape[-1]
    chunk = min(n, MAX_CHUNK_A)
    n_res = min(d, MAX_CHUNK_A // chunk)
    assert n % QB == 0 and n % chunk == 0 and kb == min(2 * QB, n) and (n == QB or n >= 2 * QB)
    assert d % n_res == 0 and (n_res * chunk // QB) % INTERLEAVE == 0
    qspec = pl.BlockSpec((None, n_res, chunk, LANE_TILE), lambda b, r, i: (b, r, i, 0))
    kvspec = pl.BlockSpec((None, n_res, n, LANE_TILE), lambda b, r, i: (b, r, 0, 0))
    return pl.pallas_call(
        functools.partial(_attn_a_kernel, n=n, kb=kb),
        grid=(b_, d // n_res, n // chunk),
        in_specs=[qspec, kvspec, kvspec, pl.BlockSpec((None,) + bias.shape[1:], lambda b, r, i: (g, 0, 0, 0))],
        out_specs=[qspec, qspec],
        out_shape=[jax.ShapeDtypeStruct(q.shape, BF16), jax.ShapeDtypeStruct(q.shape, F32)],
        compiler_params=pltpu.CompilerParams(vmem_limit_bytes=VMEM_LIMIT),
        name=f"dilated_attention_d{d}",
    )(q, k, v, bias)


def _t5_bucket(rel):
    nb = N_BUCKETS // 2
    max_exact = nb // 2
    ret = (rel > 0).astype(np.int32) * nb
    n = np.abs(rel)
    large = max_exact + (np.log(np.maximum(n, 1) / max_exact) / np.log(MAX_DISTANCE / max_exact) * (nb - max_exact)).astype(np.int32)
    large = np.minimum(large, nb - 1)
    return (ret + np.where(n < max_exact, n, large)).astype(np.int32)


def _bias_a(table, kb):
    assert all(w // (2 * d) == HALO_A for w, d in DIL_GROUPS)
    offs = np.array((0, -HALO_A, QB - kb))
    rel = np.arange(QB + kb - 1)[None, :] - (QB - 1) + offs[:, None]
    bucket = np.stack([_t5_bucket(d * rel) for _, d in DIL_GROUPS])
    heads = np.arange(N_HEADS_A).reshape(N_GROUPS, HEADS_PER_GROUP)
    f = table[bucket[:, :, None, :], heads[:, None, :, None]].astype(F32) * LOG2E
    f = jnp.where((np.abs(rel) <= HALO_A)[None, :, None, :], f, NEG)
    return _toeplitz(f, QB, kb).reshape(N_GROUPS, 3, HEADS_PER_GROUP * QB, kb)


def _attn_b_kernel(q_ref, k_ref, v_ref, bias_ref, o_ref, *, rows):
    i = pl.program_id(2)

    def one_row(a):
        r = i * ROWS_PER_STEP_B + a
        band = jnp.clip(r - KH_MAX // 2, 0, rows - KH_MAX)
        start = pl.multiple_of(band * GRID_W, GRID_W)
        qrows = pl.ds(pl.multiple_of(a * GRID_W, GRID_W), GRID_W)
        o, _ = _attn_core(q_ref[qrows, :], k_ref[pl.ds(start, KB_B), :], v_ref[pl.ds(start, KB_B), :],
                          bias_ref[r - band])
        o_ref[qrows, :] = o.astype(o_ref.dtype)

    def body(t, carry):
        for u in range(INTERLEAVE_B):
            one_row(t * INTERLEAVE_B + u)
        return carry

    lax.fori_loop(0, ROWS_PER_STEP_B // INTERLEAVE_B, body, 0)


def _neighborhood(qb, kb, vb, bias, layer):
    b_, l_, _ = qb.shape
    rows = l_ // GRID_W
    assert rows % ROWS_PER_STEP_B == 0 and rows >= KH_MAX
    halves = WIDTH_B // LANE_TILE
    qspec = pl.BlockSpec((None, ROWS_PER_STEP_B * GRID_W, LANE_TILE), lambda hf, b, i: (b, i, hf))
    kvspec = pl.BlockSpec((None, l_, LANE_TILE), lambda hf, b, i: (b, 0, hf))
    bspec = pl.BlockSpec((None, KH_MAX, None, HEADS_PER_GROUP * GRID_W, KB_B), lambda hf, b, i: (layer, 0, hf, 0, 0))
    return pl.pallas_call(
        functools.partial(_attn_b_kernel, rows=rows),
        grid=(halves, b_, rows // ROWS_PER_STEP_B),
        in_specs=[qspec, kvspec, kvspec, bspec],
        out_specs=qspec,
        out_shape=jax.ShapeDtypeStruct((b_, l_, WIDTH_B), BF16),
        compiler_params=pltpu.CompilerParams(vmem_limit_bytes=VMEM_LIMIT),
        name="neighborhood_attention",
    )(qb, kb, vb, bias)


def _bias_b(rpb):
    cols = np.arange(GRID_W)
    c_start = np.clip(cols - KW // 2, 0, GRID_W - KW)
    col_ok = (cols[None, :] >= c_start[:, None]) & (cols[None, :] < c_start[:, None] + KW)
    assert np.all(np.abs(cols[None, :] - cols[:, None])[col_ok] <= KW - 1)
    pad = GRID_W - KW
    toep = _toeplitz(jnp.pad(rpb.astype(F32) * LOG2E, ((0, 0), (0, 0), (0, 0), (pad, pad))), GRID_W, GRID_W)
    toep = jnp.transpose(jnp.where(col_ok, toep, NEG), (0, 1, 3, 2, 4))
    shape = (DEPTH, WIDTH_B // LANE_TILE, HEADS_PER_GROUP * GRID_W, KB_B)
    return jnp.stack([toep[:, :, :, KH_MAX - 1 - v:2 * KH_MAX - 1 - v].reshape(shape) for v in range(KH_MAX)], axis=1)


def _post_kernel(x_ref, mod_ref, n1_ref, n2_ref, oa0_ref, oa1_ref, oa2_ref, l0_ref, l1_ref, l2_ref, ob_ref,
                 wg_ref, bg_ref, wua_ref, wub_ref, wo_ref, w1_ref, w2_ref, out_ref, *scr):
    t = x_ref.shape[0]
    x = x_ref[...]
    sh1, sc1, g1 = mod_ref[0:1, :], mod_ref[1:2, :], mod_ref[2:3, :]
    sh2, sc2, g2 = mod_ref[3:4, :], mod_ref[4:5, :], mod_ref[5:6, :]
    h = (_rms(x, n1_ref[...]) * (1.0 + sc1) + sh1).astype(BF16)
    gate = jax.nn.sigmoid(jnp.dot(h, wg_ref[...], preferred_element_type=F32) + bg_ref[...])

    def token_order(ref, scr_refs):
        d = ref.shape[0]
        if d == 1:
            return ref[0].astype(F32)
        for r in range(d):
            for s_ref, lanes in zip(scr_refs, LANE_HALVES):
                s_ref[pl.ds(r, t // d, stride=d), :] = ref[r, :, lanes].astype(F32)
        return jnp.concatenate([s_ref[...] for s_ref in scr_refs], axis=-1)

    oa0, oa1, oa2 = token_order(oa0_ref, None), token_order(oa1_ref, scr[0:2]), token_order(oa2_ref, scr[2:4])
    l0, l1, l2 = token_order(l0_ref, None), token_order(l1_ref, scr[4:6]), token_order(l2_ref, scr[6:8])
    mx = jnp.maximum(jnp.maximum(l0, l1), l2)
    e0, e1, e2 = jnp.exp(l0 - mx), jnp.exp(l1 - mx), jnp.exp(l2 - mx)
    inv = 1.0 / (e0 + e1 + e2)
    oa = (e0 * inv) * oa0 + (e1 * inv) * oa1 + (e2 * inv) * oa2

    ua = jnp.dot(oa.astype(BF16), wua_ref[...], preferred_element_type=F32)
    ub = jnp.dot(ob_ref[...], wub_ref[...], preferred_element_type=F32)
    mixed = gate[:, :D_MODEL] * ua + gate[:, D_MODEL:] * ub
    x1 = x + g1 * jnp.dot(mixed.astype(BF16), wo_ref[...], preferred_element_type=F32)

    h2 = (_rms(x1, n2_ref[...]) * (1.0 + sc2) + sh2).astype(BF16)
    acc = jnp.zeros_like(x1)
    for c in range(D_FF // FF_CHUNK):
        f = jnp.dot(h2, w1_ref[:, c * FF_CHUNK:(c + 1) * FF_CHUNK], preferred_element_type=F32)
        f = jnp.square(jnp.maximum(f, 0.0)).astype(BF16)
        acc = acc + jnp.dot(f, w2_ref[c * FF_CHUNK:(c + 1) * FF_CHUNK, :], preferred_element_type=F32)
    out_ref[...] = x1 + g2 * acc


def _post_attention(x, mod, n1, n2, oas, lses, ob, wg, bg, wua, wub, wo, w1, w2, layer):
    b_, l_, _ = x.shape
    t = TOK_TILE_POST
    per_b = l_ // t
    tok = lambda w: pl.BlockSpec((None, t, w), lambda i: (i // per_b, i % per_b, 0))
    dil_specs = []
    for _, d in DIL_GROUPS:
        assert t % (BF16_SUBLANES * d) == 0
        dil_specs.append(pl.BlockSpec((None, d, t // d, LANE_TILE), lambda i: (i // per_b, 0, i % per_b, 0)))
    return pl.pallas_call(
        _post_kernel,
        grid=(b_ * per_b,),
        in_specs=[
            tok(D_MODEL),
            pl.BlockSpec((None, None, N_MOD, D_MODEL), lambda i: (layer, i // per_b, 0, 0)),
            _layer_spec((1, D_MODEL), layer), _layer_spec((1, D_MODEL), layer),
            *dil_specs, *dil_specs,
            tok(WIDTH_B),
            _layer_spec((D_MODEL, 2 * D_MODEL), layer), _layer_spec((1, 2 * D_MODEL), layer),
            _layer_spec((WIDTH_A, D_MODEL), layer), _layer_spec((WIDTH_B, D_MODEL), layer),
            _layer_spec((D_MODEL, D_MODEL), layer),
            _layer_spec((D_MODEL, D_FF), layer), _layer_spec((D_FF, D_MODEL), layer),
        ],
        out_specs=tok(D_MODEL),
        out_shape=jax.ShapeDtypeStruct(x.shape, F32),
        scratch_shapes=[pltpu.VMEM((t, VREG_LANES), F32)] * 8,
        compiler_params=pltpu.CompilerParams(vmem_limit_bytes=VMEM_LIMIT),
        name="post_attention_mlp",
    )(x, mod, n1, n2, *oas, *lses, ob, wg, bg, wua, wub, wo, w1, w2)


def _head_sum_matrix():
    head = np.arange(LANE_TILE) // HEAD_DIM
    return jnp.asarray((head[:, None] == head[None, :]).astype(np.float32), dtype=BF16)


def kernel(x_prompt, x_sample, c_prompt, c_sample, norm1_g, norm2_g, w_mod, b_mod, w_in, q_norm_g, k_norm_g,
           rel_bias, rpb, w_gate, b_gate, w_up_a, w_up_b, w_o, w_ff1, w_ff2):
    n_prompt = c_prompt.shape[0]
    mod_all = _modulation(jnp.concatenate([c_prompt, c_sample], axis=0), w_mod, b_mod)
    mod_all = mod_all.reshape(DEPTH, -1, N_MOD, D_MODEL)
    ones = _head_sum_matrix()
    qg = (q_norm_g.reshape(DEPTH, 1, QKV_PART) * (HEAD_DIM ** -0.5 * LOG2E)).astype(F32)
    kg = k_norm_g.reshape(DEPTH, 1, QKV_PART).astype(F32)
    n1 = norm1_g.reshape(DEPTH, 1, D_MODEL)
    n2 = norm2_g.reshape(DEPTH, 1, D_MODEL)
    bg = b_gate.reshape(DEPTH, 1, 2 * D_MODEL)
    w_in_b, w_gate_b = w_in.astype(BF16), w_gate.astype(BF16)
    w_up_a_b, w_up_b_b, w_o_b = w_up_a.astype(BF16), w_up_b.astype(BF16), w_o.astype(BF16)
    w_ff1_b, w_ff2_b = w_ff1.astype(BF16), w_ff2.astype(BF16)

    xs = (x_prompt, x_sample)
    bias_b = _bias_b(rpb)
    key_blocks = {min(2 * QB, x.shape[1] // d) for x in xs for _, d in DIL_GROUPS}
    bias_a = {kb: _bias_a(rel_bias, kb) for kb in sorted(key_blocks)}

    ys = []
    for x, mod_x in ((x_prompt, mod_all[:, :n_prompt]), (x_sample, mod_all[:, n_prompt:])):
        l_ = x.shape[1]
        for l in range(DEPTH):
            q, k, v = _pre_attention(x, mod_x, n1, w_in_b, qg, kg, ones, l)
            oas, lses = [], []
            for g, (_, d) in enumerate(DIL_GROUPS):
                o, lse = _dilated_group(q[g], k[g], v[g], bias_a[min(2 * QB, l_ // d)], g)
                oas.append(o)
                lses.append(lse)
            ob = _neighborhood(q[N_GROUPS], k[N_GROUPS], v[N_GROUPS], bias_b, l)
            x = _post_attention(x, mod_x, n1, n2, oas, lses, ob, w_gate_b, bg, w_up_a_b, w_up_b_b, w_o_b,
                                w_ff1_b, w_ff2_b, l)
        ys.append(x)
    return tuple(ys)
```

```python
import functools

import jax
import jax.numpy as jnp
import numpy as np
from jax import lax
from jax.experimental import pallas as pl
from jax.experimental.pallas import tpu as pltpu

D_MODEL = 1024
DEPTH = 2
HEAD_DIM = 64
DIL_GROUPS = ((128, 1), (512, 4), (2048, 16))
HEADS_PER_GROUP = 4
N_GROUPS = len(DIL_GROUPS)
N_HEADS_A = HEADS_PER_GROUP * N_GROUPS
N_HEADS_B = 8
N_HEADS = N_HEADS_A + N_HEADS_B
WIDTH_A = HEADS_PER_GROUP * HEAD_DIM
WIDTH_B = N_HEADS_B * HEAD_DIM
QKV_PART = N_HEADS * HEAD_DIM
N_BUCKETS = 32
MAX_DISTANCE = 1024
GRID_W = 64
KH_MAX = 8
KW = 16
D_FF = 4 * D_MODEL
N_MOD = 6
EPS = 1e-6
NEG = -1e30
LOG2E = 1.4426950408889634
LN2 = 0.6931471805599453

LANE_TILE = 256
QB = 128
HALO_A = 64
KB_B = KH_MAX * GRID_W
ROWS_PER_STEP_B = 32
MAX_CHUNK_A = 2048
INTERLEAVE_B = 16
INTERLEAVE = 16
TOK_TILE_PRE = 1024
TOK_TILE_POST = 512
FF_CHUNK = 1024
BF16_SUBLANES = 16
VREG_LANES = 128
LANE_HALVES = (slice(0, VREG_LANES), slice(VREG_LANES, LANE_TILE))
MOD_COL_BLOCK = 1536
VMEM_LIMIT = 56 * 1024 * 1024

F32 = jnp.float32
BF16 = jnp.bfloat16


def _rms(x, g):
    return x * lax.rsqrt(jnp.mean(x * x, axis=-1, keepdims=True) + EPS) * g


def _const_spec(shape):
    return pl.BlockSpec(shape, lambda *_: (0,) * len(shape), pipeline_mode=pl.Buffered(1))


def _layer_spec(shape, layer):
    return pl.BlockSpec((None,) + shape, lambda *_: (layer,) + (0,) * len(shape), pipeline_mode=pl.Buffered(1))


def _mod_kernel(c_ref, w_ref, b_ref, o_ref):
    c = c_ref[...]
    s = c * jax.nn.sigmoid(c)
    o_ref[...] = jnp.dot(s, w_ref[...], preferred_element_type=F32) + b_ref[...]


def _modulation(c_all, w_mod, b_mod):
    nb = c_all.shape[0]
    ncol = N_MOD * D_MODEL
    blk = MOD_COL_BLOCK
    assert ncol % blk == 0
    return pl.pallas_call(
        _mod_kernel,
        grid=(DEPTH, ncol // blk),
        in_specs=[
            pl.BlockSpec((nb, D_MODEL), lambda l, j: (0, 0)),
            pl.BlockSpec((None, D_MODEL, blk), lambda l, j: (l, 0, j)),
            pl.BlockSpec((None, 1, blk), lambda l, j: (l, 0, j)),
        ],
        out_specs=pl.BlockSpec((None, nb, blk), lambda l, j: (l, 0, j)),
        out_shape=jax.ShapeDtypeStruct((DEPTH, nb, ncol), F32),
        compiler_params=pltpu.CompilerParams(vmem_limit_bytes=VMEM_LIMIT),
        name="modulation",
    )(c_all, w_mod, b_mod.reshape(DEPTH, 1, ncol))


def _pre_kernel(x_ref, mod_ref, n1_ref, win_ref, qg_ref, kg_ref, ones_ref, *refs):
    scr_refs = refs[-2:]
    outs = refs[:-2]
    t = x_ref.shape[0]
    x = x_ref[...]
    sh1 = mod_ref[0:1, :]
    sc1 = mod_ref[1:2, :]
    h = (_rms(x, n1_ref[...]) * (1.0 + sc1) + sh1).astype(BF16)
    ones = ones_ref[...]
    n_chunk = QKV_PART // LANE_TILE
    per_part = N_GROUPS + 1

    def project(part):
        return jnp.dot(h, win_ref[:, part * QKV_PART:(part + 1) * QKV_PART], preferred_element_type=F32)

    def finish(part, g_ref, y_part):
        for c in range(n_chunk):
            y = y_part[:, c * LANE_TILE:(c + 1) * LANE_TILE]
            if g_ref is not None:
                ss = jnp.dot((y * y).astype(BF16), ones, preferred_element_type=F32)
                y = y * lax.rsqrt(ss * (1.0 / HEAD_DIM) + EPS) * g_ref[:, c * LANE_TILE:(c + 1) * LANE_TILE]
            if c < N_GROUPS:
                out = outs[part * per_part + c]
                d = DIL_GROUPS[c][1]
                if d == 1:
                    out[0] = y.astype(out.dtype)
                else:
                    for s_ref, lanes in zip(scr_refs, LANE_HALVES):
                        s_ref[...] = y[:, lanes]
                    for r in range(d):
                        for s_ref, lanes in zip(scr_refs, LANE_HALVES):
                            out[r, :, lanes] = s_ref[pl.ds(r, t // d, stride=d), :].astype(out.dtype)
            else:
                out = outs[part * per_part + N_GROUPS]
                cb = c - N_GROUPS
                out[:, cb * LANE_TILE:(cb + 1) * LANE_TILE] = y.astype(out.dtype)

    for part, g_ref in enumerate((qg_ref, kg_ref, None)):
        finish(part, g_ref, project(part))


def _pre_attention(x, mod, n1, w_in, qg, kg, ones, layer):
    b_, l_, _ = x.shape
    t = TOK_TILE_PRE
    per_b = l_ // t
    tok = lambda w: pl.BlockSpec((None, t, w), lambda i: (i // per_b, i % per_b, 0))
    out_specs, out_shape = [], []
    for _ in range(3):
        for _, d in DIL_GROUPS:
            assert t % (BF16_SUBLANES * d) == 0
            out_specs.append(pl.BlockSpec((None, d, t // d, LANE_TILE), lambda i: (i // per_b, 0, i % per_b, 0)))
            out_shape.append(jax.ShapeDtypeStruct((b_, d, l_ // d, LANE_TILE), BF16))
        out_specs.append(tok(WIDTH_B))
        out_shape.append(jax.ShapeDtypeStruct((b_, l_, WIDTH_B), BF16))
    outs = pl.pallas_call(
        _pre_kernel,
        grid=(b_ * per_b,),
        in_specs=[
            tok(D_MODEL),
            pl.BlockSpec((None, None, N_MOD, D_MODEL), lambda i: (layer, i // per_b, 0, 0)),
            _layer_spec((1, D_MODEL), layer),
            _layer_spec((D_MODEL, 3 * QKV_PART), layer),
            _layer_spec((1, QKV_PART), layer),
            _layer_spec((1, QKV_PART), layer),
            _const_spec((LANE_TILE, LANE_TILE)),
        ],
        out_specs=out_specs,
        out_shape=out_shape,
        scratch_shapes=[pltpu.VMEM((t, VREG_LANES), F32)] * 2,
        compiler_params=pltpu.CompilerParams(vmem_limit_bytes=VMEM_LIMIT),
        name="pre_attention",
    )(x, mod, n1, w_in, qg, kg, ones)
    per_part = N_GROUPS + 1
    return [outs[p * per_part:(p + 1) * per_part] for p in range(3)]


def _attn_core(q, k, v, bias):
    nq = q.shape[0]
    head_of_lane = lax.broadcasted_iota(jnp.int32, (nq, LANE_TILE), 1) // HEAD_DIM
    zero = jnp.zeros_like(q)
    qs = jnp.concatenate([jnp.where(head_of_lane == h, q, zero) for h in range(HEADS_PER_GROUP)], axis=0)
    s = lax.dot_general(qs, k, (((1,), (1,)), ((), ())), preferred_element_type=F32) + bias
    m = jnp.max(s, axis=-1, keepdims=True)
    p = jnp.exp2(s - m)
    l = jnp.sum(p, axis=-1, keepdims=True)
    o4 = jnp.dot(p.astype(BF16), v, preferred_element_type=F32)

    def unstack(x4):
        x = jnp.broadcast_to(x4[:nq, :], (nq, LANE_TILE))
        for h in range(1, HEADS_PER_GROUP):
            x = jnp.where(head_of_lane == h, x4[h * nq:(h + 1) * nq, :], x)
        return x

    l_dense = unstack(l)
    o = unstack(o4) * (1.0 / l_dense)
    lse = unstack(m) * LN2 + jnp.log(l_dense)
    return o, lse


def _toeplitz(f, nq, nk):
    p = nq + nk - 1
    assert f.shape[-1] == p and nq >= 2
    u = jnp.roll(f, -(nq - 1), axis=-1)
    flat = jnp.tile(u, nq)[..., :nq * (p - 1)]
    return flat.reshape(f.shape[:-1] + (nq, p - 1))[..., :nk]


def _attn_a_kernel(q_ref, k_ref, v_ref, bias_ref, o_ref, lse_ref, *, n, kb):
    i = pl.program_id(2)
    n_res, chunk, _ = q_ref.shape
    blocks_per_chunk = chunk // QB
    last_block = n // QB - 1

    def one_block(item):
        res, j = (0, item) if n_res == 1 else (item // blocks_per_chunk, item % blocks_per_chunk)
        blk = i * blocks_per_chunk + j
        start = pl.multiple_of(jnp.clip(blk * QB - HALO_A, 0, n - kb), HALO_A)
        var = jnp.where(blk == 0, 0, jnp.where(blk == last_block, 2, 1))
        rows = pl.ds(pl.multiple_of(j * QB, QB), QB)
        o, lse = _attn_core(q_ref[res, rows, :], k_ref[res, pl.ds(start, kb), :], v_ref[res, pl.ds(start, kb), :],
                            bias_ref[var])
        o_ref[res, rows, :] = o.astype(o_ref.dtype)
        lse_ref[res, rows, :] = lse

    def body(t, carry):
        for u in range(INTERLEAVE):
            one_block(t * INTERLEAVE + u)
        return carry

    lax.fori_loop(0, n_res * blocks_per_chunk // INTERLEAVE, body, 0)


def _dilated_group(q, k, v, bias, d):
    b_, d_, n, _ = q.shape
    assert d_ == d
    kb = bias.shape[-1]
    chunk = min(n, MAX_CHUNK_A)
    n_res = min(d, MAX_CHUNK_A // chunk)
    assert n % QB == 0 and n % chunk == 0 and kb == min(2 * QB, n) and (n == QB or n >= 2 * QB)
    assert d % n_res == 0 and (n_res * chunk // QB) % INTERLEAVE == 0
    qspec = pl.BlockSpec((None, n_res, chunk, LANE_TILE), lambda b, r, i: (b, r, i, 0))
    kvspec = pl.BlockSpec((None, n_res, n, LANE_TILE), lambda b, r, i: (b, r, 0, 0))
    return pl.pallas_call(
        functools.partial(_attn_a_kernel, n=n, kb=kb),
        grid=(b_, d // n_res, n // chunk),
        in_specs=[qspec, kvspec, kvspec, pl.BlockSpec(bias.shape, lambda b, r, i: (0, 0, 0))],
        out_specs=[qspec, qspec],
        out_shape=[jax.ShapeDtypeStruct(q.shape, BF16), jax.ShapeDtypeStruct(q.shape, F32)],
        compiler_params=pltpu.CompilerParams(vmem_limit_bytes=VMEM_LIMIT),
        name=f"dilated_attention_d{d}",
    )(q, k, v, bias)


def _t5_bucket(rel):
    nb = N_BUCKETS // 2
    max_exact = nb // 2
    ret = (rel > 0).astype(np.int32) * nb
    n = np.abs(rel)
    large = max_exact + (np.log(np.maximum(n, 1) / max_exact) / np.log(MAX_DISTANCE / max_exact) * (nb - max_exact)).astype(np.int32)
    large = np.minimum(large, nb - 1)
    return (ret + np.where(n < max_exact, n, large)).astype(np.int32)


def _bias_a(table, g, w, d, n):
    half = w // (2 * d)
    assert half == HALO_A
    kb = min(2 * QB, n)
    sl = slice(g * HEADS_PER_GROUP, (g + 1) * HEADS_PER_GROUP)
    offs = np.array((0, -HALO_A, QB - kb))
    rel = np.arange(QB + kb - 1)[None, :] - (QB - 1) + offs[:, None]
    f = jnp.transpose(table[_t5_bucket(d * rel)][:, :, sl], (0, 2, 1)).astype(F32) * LOG2E
    f = jnp.where((np.abs(rel) <= half)[:, None, :], f, NEG)
    return _toeplitz(f, QB, kb).reshape(3, HEADS_PER_GROUP * QB, kb)


def _attn_b_kernel(q_ref, k_ref, v_ref, bias_ref, o_ref, *, rows):
    i = pl.program_id(2)

    def one_row(a):
        r = i * ROWS_PER_STEP_B + a
        band = jnp.clip(r - KH_MAX // 2, 0, rows - KH_MAX)
        start = pl.multiple_of(band * GRID_W, GRID_W)
        qrows = pl.ds(pl.multiple_of(a * GRID_W, GRID_W), GRID_W)
        o, _ = _attn_core(q_ref[qrows, :], k_ref[pl.ds(start, KB_B), :], v_ref[pl.ds(start, KB_B), :],
                          bias_ref[r - band])
        o_ref[qrows, :] = o.astype(o_ref.dtype)

    def body(t, carry):
        for u in range(INTERLEAVE_B):
            one_row(t * INTERLEAVE_B + u)
        return carry

    lax.fori_loop(0, ROWS_PER_STEP_B // INTERLEAVE_B, body, 0)


def _neighborhood(qb, kb, vb, bias, layer):
    b_, l_, _ = qb.shape
    rows = l_ // GRID_W
    assert rows % ROWS_PER_STEP_B == 0 and rows >= KH_MAX
    halves = WIDTH_B // LANE_TILE
    qspec = pl.BlockSpec((None, ROWS_PER_STEP_B * GRID_W, LANE_TILE), lambda hf, b, i: (b, i, hf))
    kvspec = pl.BlockSpec((None, l_, LANE_TILE), lambda hf, b, i: (b, 0, hf))
    bspec = pl.BlockSpec((None, KH_MAX, None, HEADS_PER_GROUP * GRID_W, KB_B), lambda hf, b, i: (layer, 0, hf, 0, 0))
    return pl.pallas_call(
        functools.partial(_attn_b_kernel, rows=rows),
        grid=(halves, b_, rows // ROWS_PER_STEP_B),
        in_specs=[qspec, kvspec, kvspec, bspec],
        out_specs=qspec,
        out_shape=jax.ShapeDtypeStruct((b_, l_, WIDTH_B), BF16),
        compiler_params=pltpu.CompilerParams(vmem_limit_bytes=VMEM_LIMIT),
        name="neighborhood_attention",
    )(qb, kb, vb, bias)


def _bias_b(rpb):
    cols = np.arange(GRID_W)
    c_start = np.clip(cols - KW // 2, 0, GRID_W - KW)
    col_ok = (cols[None, :] >= c_start[:, None]) & (cols[None, :] < c_start[:, None] + KW)
    assert np.all(np.abs(cols[None, :] - cols[:, None])[col_ok] <= KW - 1)
    pad = GRID_W - KW
    toep = _toeplitz(jnp.pad(rpb.astype(F32) * LOG2E, ((0, 0), (0, 0), (0, 0), (pad, pad))), GRID_W, GRID_W)
    toep = jnp.transpose(jnp.where(col_ok, toep, NEG), (0, 1, 3, 2, 4))
    shape = (DEPTH, WIDTH_B // LANE_TILE, HEADS_PER_GROUP * GRID_W, KB_B)
    return jnp.stack([toep[:, :, :, KH_MAX - 1 - v:2 * KH_MAX - 1 - v].reshape(shape) for v in range(KH_MAX)], axis=1)


def _post_kernel(x_ref, mod_ref, n1_ref, n2_ref, oa0_ref, oa1_ref, oa2_ref, l0_ref, l1_ref, l2_ref, ob_ref,
                 wg_ref, bg_ref, wua_ref, wub_ref, wo_ref, w1_ref, w2_ref, out_ref, *scr):
    t = x_ref.shape[0]
    x = x_ref[...]
    sh1, sc1, g1 = mod_ref[0:1, :], mod_ref[1:2, :], mod_ref[2:3, :]
    sh2, sc2, g2 = mod_ref[3:4, :], mod_ref[4:5, :], mod_ref[5:6, :]
    h = (_rms(x, n1_ref[...]) * (1.0 + sc1) + sh1).astype(BF16)
    gate = jax.nn.sigmoid(jnp.dot(h, wg_ref[...], preferred_element_type=F32) + bg_ref[...])

    def token_order(ref, scr_refs):
        d = ref.shape[0]
        if d == 1:
            return ref[0].astype(F32)
        for r in range(d):
            for s_ref, lanes in zip(scr_refs, LANE_HALVES):
                s_ref[pl.ds(r, t // d, stride=d), :] = ref[r, :, lanes].astype(F32)
        return jnp.concatenate([s_ref[...] for s_ref in scr_refs], axis=-1)

    oa0, oa1, oa2 = token_order(oa0_ref, None), token_order(oa1_ref, scr[0:2]), token_order(oa2_ref, scr[2:4])
    l0, l1, l2 = token_order(l0_ref, None), token_order(l1_ref, scr[4:6]), token_order(l2_ref, scr[6:8])
    mx = jnp.maximum(jnp.maximum(l0, l1), l2)
    e0, e1, e2 = jnp.exp(l0 - mx), jnp.exp(l1 - mx), jnp.exp(l2 - mx)
    inv = 1.0 / (e0 + e1 + e2)
    oa = (e0 * inv) * oa0 + (e1 * inv) * oa1 + (e2 * inv) * oa2

    ua = jnp.dot(oa.astype(BF16), wua_ref[...], preferred_element_type=F32)
    ub = jnp.dot(ob_ref[...], wub_ref[...], preferred_element_type=F32)
    mixed = gate[:, :D_MODEL] * ua + gate[:, D_MODEL:] * ub
    x1 = x + g1 * jnp.dot(mixed.astype(BF16), wo_ref[...], preferred_element_type=F32)

    h2 = (_rms(x1, n2_ref[...]) * (1.0 + sc2) + sh2).astype(BF16)
    acc = jnp.zeros_like(x1)
    for c in range(D_FF // FF_CHUNK):
        f = jnp.dot(h2, w1_ref[:, c * FF_CHUNK:(c + 1) * FF_CHUNK], preferred_element_type=F32)
        f = jnp.square(jnp.maximum(f, 0.0)).astype(BF16)
        acc = acc + jnp.dot(f, w2_ref[c * FF_CHUNK:(c + 1) * FF_CHUNK, :], preferred_element_type=F32)
    out_ref[...] = x1 + g2 * acc


def _post_attention(x, mod, n1, n2, oas, lses, ob, wg, bg, wua, wub, wo, w1, w2, layer):
    b_, l_, _ = x.shape
    t = TOK_TILE_POST
    per_b = l_ // t
    tok = lambda w: pl.BlockSpec((None, t, w), lambda i: (i // per_b, i % per_b, 0))
    dil_specs = []
    for _, d in DIL_GROUPS:
        assert t % (BF16_SUBLANES * d) == 0
        dil_specs.append(pl.BlockSpec((None, d, t // d, LANE_TILE), lambda i: (i // per_b, 0, i % per_b, 0)))
    return pl.pallas_call(
        _post_kernel,
        grid=(b_ * per_b,),
        in_specs=[
            tok(D_MODEL),
            pl.BlockSpec((None, None, N_MOD, D_MODEL), lambda i: (layer, i // per_b, 0, 0)),
            _layer_spec((1, D_MODEL), layer), _layer_spec((1, D_MODEL), layer),
            *dil_specs, *dil_specs,
            tok(WIDTH_B),
            _layer_spec((D_MODEL, 2 * D_MODEL), layer), _layer_spec((1, 2 * D_MODEL), layer),
            _layer_spec((WIDTH_A, D_MODEL), layer), _layer_spec((WIDTH_B, D_MODEL), layer),
            _layer_spec((D_MODEL, D_MODEL), layer),
            _layer_spec((D_MODEL, D_FF), layer), _layer_spec((D_FF, D_MODEL), layer),
        ],
        out_specs=tok(D_MODEL),
        out_shape=jax.ShapeDtypeStruct(x.shape, F32),
        scratch_shapes=[pltpu.VMEM((t, VREG_LANES), F32)] * 8,
        compiler_params=pltpu.CompilerParams(vmem_limit_bytes=VMEM_LIMIT),
        name="post_attention_mlp",
    )(x, mod, n1, n2, *oas, *lses, ob, wg, bg, wua, wub, wo, w1, w2)


def _head_sum_matrix():
    head = np.arange(LANE_TILE) // HEAD_DIM
    return jnp.asarray((head[:, None] == head[None, :]).astype(np.float32), dtype=BF16)


def kernel(x_prompt, x_sample, c_prompt, c_sample, norm1_g, norm2_g, w_mod, b_mod, w_in, q_norm_g, k_norm_g,
           rel_bias, rpb, w_gate, b_gate, w_up_a, w_up_b, w_o, w_ff1, w_ff2):
    n_prompt = c_prompt.shape[0]
    mod_all = _modulation(jnp.concatenate([c_prompt, c_sample], axis=0), w_mod, b_mod)
    mod_all = mod_all.reshape(DEPTH, -1, N_MOD, D_MODEL)
    ones = _head_sum_matrix()
    qg = (q_norm_g.reshape(DEPTH, 1, QKV_PART) * (HEAD_DIM ** -0.5 * LOG2E)).astype(F32)
    kg = k_norm_g.reshape(DEPTH, 1, QKV_PART).astype(F32)
    n1 = norm1_g.reshape(DEPTH, 1, D_MODEL)
    n2 = norm2_g.reshape(DEPTH, 1, D_MODEL)
    bg = b_gate.reshape(DEPTH, 1, 2 * D_MODEL)
    w_in_b, w_gate_b = w_in.astype(BF16), w_gate.astype(BF16)
    w_up_a_b, w_up_b_b, w_o_b = w_up_a.astype(BF16), w_up_b.astype(BF16), w_o.astype(BF16)
    w_ff1_b, w_ff2_b = w_ff1.astype(BF16), w_ff2.astype(BF16)

    xs = (x_prompt, x_sample)
    bias_b = _bias_b(rpb)
    bias_a = {}
    for x in xs:
        for g, (w, d) in enumerate(DIL_GROUPS):
            n = x.shape[1] // d
            if (g, min(2 * QB, n)) not in bias_a:
                bias_a[(g, min(2 * QB, n))] = _bias_a(rel_bias, g, w, d, n)

    ys = []
    for x, mod_x in ((x_prompt, mod_all[:, :n_prompt]), (x_sample, mod_all[:, n_prompt:])):
        l_ = x.shape[1]
        for l in range(DEPTH):
            q, k, v = _pre_attention(x, mod_x, n1, w_in_b, qg, kg, ones, l)
            oas, lses = [], []
            for g, (w, d) in enumerate(DIL_GROUPS):
                o, lse = _dilated_group(q[g], k[g], v[g], bias_a[(g, min(2 * QB, l_ // d))], d)
                oas.append(o)
                lses.append(lse)
            ob = _neighborhood(q[N_GROUPS], k[N_GROUPS], v[N_GROUPS], bias_b, l)
            x = _post_attention(x, mod_x, n1, n2, oas, lses, ob, w_gate_b, bg, w_up_a_b, w_up_b_b, w_o_b,
                                w_ff1_b, w_ff2_b, l)
        ys.append(x)
    return tuple(ys)
```

```python
import functools

import jax
import jax.numpy as jnp
import numpy as np
from jax import lax
from jax.experimental import pallas as pl
from jax.experimental.pallas import tpu as pltpu

D_MODEL = 1024
DEPTH = 2
HEAD_DIM = 64
DIL_GROUPS = ((128, 1), (512, 4), (2048, 16))
HEADS_PER_GROUP = 4
N_GROUPS = len(DIL_GROUPS)
N_HEADS_A = HEADS_PER_GROUP * N_GROUPS
N_HEADS_B = 8
N_HEADS = N_HEADS_A + N_HEADS_B
WIDTH_A = HEADS_PER_GROUP * HEAD_DIM
WIDTH_B = N_HEADS_B * HEAD_DIM
QKV_PART = N_HEADS * HEAD_DIM
N_BUCKETS = 32
MAX_DISTANCE = 1024
GRID_W = 64
KH_MAX = 8
KW = 16
D_FF = 4 * D_MODEL
N_MOD = 6
EPS = 1e-6
NEG = -1e30
LOG2E = 1.4426950408889634
LN2 = 0.6931471805599453

LANE_TILE = 256
QB = 128
SLAB = 16
HALO_A = 64
KB_B = KH_MAX * GRID_W
ROWS_PER_STEP_B = 32
MAX_CHUNK_A = 2048
INTERLEAVE_B = 16
INTERLEAVE = 16
TOK_TILE_PRE = 1024
TOK_TILE_POST = 512
FF_CHUNK = 1024
BF16_SUBLANES = 16
VREG_LANES = 128
LANE_HALVES = (slice(0, VREG_LANES), slice(VREG_LANES, LANE_TILE))
MOD_COL_BLOCK = 1536
VMEM_LIMIT = 56 * 1024 * 1024

F32 = jnp.float32
BF16 = jnp.bfloat16


def _rms(x, g):
    return x * lax.rsqrt(jnp.mean(x * x, axis=-1, keepdims=True) + EPS) * g


def _const_spec(shape):
    return pl.BlockSpec(shape, lambda *_: (0,) * len(shape), pipeline_mode=pl.Buffered(1))


def _layer_spec(shape, layer):
    return pl.BlockSpec((None,) + shape, lambda *_: (layer,) + (0,) * len(shape), pipeline_mode=pl.Buffered(1))


def _mod_kernel(c_ref, w_ref, b_ref, o_ref):
    c = c_ref[...]
    s = c * jax.nn.sigmoid(c)
    o_ref[...] = jnp.dot(s, w_ref[...], preferred_element_type=F32) + b_ref[...]


def _modulation(c_all, w_mod, b_mod):
    nb = c_all.shape[0]
    ncol = N_MOD * D_MODEL
    blk = MOD_COL_BLOCK
    assert ncol % blk == 0
    return pl.pallas_call(
        _mod_kernel,
        grid=(DEPTH, ncol // blk),
        in_specs=[
            pl.BlockSpec((nb, D_MODEL), lambda l, j: (0, 0)),
            pl.BlockSpec((None, D_MODEL, blk), lambda l, j: (l, 0, j)),
            pl.BlockSpec((None, 1, blk), lambda l, j: (l, 0, j)),
        ],
        out_specs=pl.BlockSpec((None, nb, blk), lambda l, j: (l, 0, j)),
        out_shape=jax.ShapeDtypeStruct((DEPTH, nb, ncol), F32),
        compiler_params=pltpu.CompilerParams(vmem_limit_bytes=VMEM_LIMIT),
        name="modulation",
    )(c_all, w_mod, b_mod.reshape(DEPTH, 1, ncol))


def _pre_kernel(x_ref, mod_ref, n1_ref, win_ref, qg_ref, kg_ref, ones_ref, *refs):
    scr_refs = refs[-2:]
    outs = refs[:-2]
    t = x_ref.shape[0]
    x = x_ref[...]
    sh1 = mod_ref[0:1, :]
    sc1 = mod_ref[1:2, :]
    h = (_rms(x, n1_ref[...]) * (1.0 + sc1) + sh1).astype(BF16)
    ones = ones_ref[...]
    n_chunk = QKV_PART // LANE_TILE
    per_part = N_GROUPS + 1

    def project(part):
        return jnp.dot(h, win_ref[:, part * QKV_PART:(part + 1) * QKV_PART], preferred_element_type=F32)

    def finish(part, g_ref, y_part):
        for c in range(n_chunk):
            y = y_part[:, c * LANE_TILE:(c + 1) * LANE_TILE]
            if g_ref is not None:
                ss = jnp.dot((y * y).astype(BF16), ones, preferred_element_type=F32)
                y = y * lax.rsqrt(ss * (1.0 / HEAD_DIM) + EPS) * g_ref[:, c * LANE_TILE:(c + 1) * LANE_TILE]
            if c < N_GROUPS:
                out = outs[part * per_part + c]
                d = DIL_GROUPS[c][1]
                if d == 1:
                    out[0] = y.astype(out.dtype)
                else:
                    for s_ref, lanes in zip(scr_refs, LANE_HALVES):
                        s_ref[...] = y[:, lanes]
                    for r in range(d):
                        for s_ref, lanes in zip(scr_refs, LANE_HALVES):
                            out[r, :, lanes] = s_ref[pl.ds(r, t // d, stride=d), :].astype(out.dtype)
            else:
                out = outs[part * per_part + N_GROUPS]
                cb = c - N_GROUPS
                out[:, cb * LANE_TILE:(cb + 1) * LANE_TILE] = y.astype(out.dtype)

    for part, g_ref in enumerate((qg_ref, kg_ref, None)):
        finish(part, g_ref, project(part))


def _pre_attention(x, mod, n1, w_in, qg, kg, ones, layer):
    b_, l_, _ = x.shape
    t = TOK_TILE_PRE
    per_b = l_ // t
    tok = lambda w: pl.BlockSpec((None, t, w), lambda i: (i // per_b, i % per_b, 0))
    out_specs, out_shape = [], []
    for _ in range(3):
        for _, d in DIL_GROUPS:
            assert t % (BF16_SUBLANES * d) == 0
            out_specs.append(pl.BlockSpec((None, d, t // d, LANE_TILE), lambda i: (i // per_b, 0, i % per_b, 0)))
            out_shape.append(jax.ShapeDtypeStruct((b_, d, l_ // d, LANE_TILE), BF16))
        out_specs.append(tok(WIDTH_B))
        out_shape.append(jax.ShapeDtypeStruct((b_, l_, WIDTH_B), BF16))
    outs = pl.pallas_call(
        _pre_kernel,
        grid=(b_ * per_b,),
        in_specs=[
            tok(D_MODEL),
            pl.BlockSpec((None, None, N_MOD, D_MODEL), lambda i: (layer, i // per_b, 0, 0)),
            _layer_spec((1, D_MODEL), layer),
            _layer_spec((D_MODEL, 3 * QKV_PART), layer),
            _layer_spec((1, QKV_PART), layer),
            _layer_spec((1, QKV_PART), layer),
            _const_spec((LANE_TILE, LANE_TILE)),
        ],
        out_specs=out_specs,
        out_shape=out_shape,
        scratch_shapes=[pltpu.VMEM((t, VREG_LANES), F32)] * 2,
        compiler_params=pltpu.CompilerParams(vmem_limit_bytes=VMEM_LIMIT),
        name="pre_attention",
    )(x, mod, n1, w_in, qg, kg, ones)
    per_part = N_GROUPS + 1
    return [outs[p * per_part:(p + 1) * per_part] for p in range(3)]


def _attn_core(q, k, v, bias):
    nq = q.shape[0]
    head_of_lane = lax.broadcasted_iota(jnp.int32, (SLAB, LANE_TILE), 1) // HEAD_DIM
    zero = jnp.zeros((SLAB, LANE_TILE), q.dtype)
    qs = jnp.concatenate([jnp.where(head_of_lane == h, q[j:j + SLAB, :], zero)
                          for j in range(0, nq, SLAB) for h in range(HEADS_PER_GROUP)], axis=0)
    s = lax.dot_general(qs, k, (((1,), (1,)), ((), ())), preferred_element_type=F32) + bias
    m = jnp.max(s, axis=-1, keepdims=True)
    p = jnp.exp2(s - m)
    l = jnp.sum(p, axis=-1, keepdims=True)
    o4 = jnp.dot(p.astype(BF16), v, preferred_element_type=F32)

    def unstack(x4):
        slabs = []
        for j in range(0, HEADS_PER_GROUP * nq, HEADS_PER_GROUP * SLAB):
            x = jnp.broadcast_to(x4[j:j + SLAB, :], (SLAB, LANE_TILE))
            for h in range(1, HEADS_PER_GROUP):
                x = jnp.where(head_of_lane == h, x4[j + h * SLAB:j + (h + 1) * SLAB, :], x)
            slabs.append(x)
        return jnp.concatenate(slabs, axis=0)

    l_dense = unstack(l)
    o = unstack(o4) * (1.0 / l_dense)
    lse = unstack(m) * LN2 + jnp.log(l_dense)
    return o, lse


def _slab_major(bias):
    lead, (rows, nk) = bias.shape[:-2], bias.shape[-2:]
    b = bias.reshape(lead + (HEADS_PER_GROUP, rows // (HEADS_PER_GROUP * SLAB), SLAB, nk))
    return jnp.swapaxes(b, -4, -3).reshape(bias.shape)


def _toeplitz(f, nq, nk):
    p = nq + nk - 1
    assert f.shape[-1] == p and nq >= 2
    u = jnp.roll(f, -(nq - 1), axis=-1)
    flat = jnp.tile(u, nq)[..., :nq * (p - 1)]
    return flat.reshape(f.shape[:-1] + (nq, p - 1))[..., :nk]


def _attn_a_kernel(q_ref, k_ref, v_ref, bias_ref, o_ref, lse_ref, *, n, kb):
    i = pl.program_id(2)
    n_res, chunk, _ = q_ref.shape
    blocks_per_chunk = chunk // QB
    last_block = n // QB - 1

    def one_block(item):
        res, j = (0, item) if n_res == 1 else (item // blocks_per_chunk, item % blocks_per_chunk)
        blk = i * blocks_per_chunk + j
        start = pl.multiple_of(jnp.clip(blk * QB - HALO_A, 0, n - kb), HALO_A)
        var = jnp.where(blk == 0, 0, jnp.where(blk == last_block, 2, 1))
        rows = pl.ds(pl.multiple_of(j * QB, QB), QB)
        o, lse = _attn_core(q_ref[res, rows, :], k_ref[res, pl.ds(start, kb), :], v_ref[res, pl.ds(start, kb), :],
                            bias_ref[var])
        o_ref[res, rows, :] = o.astype(o_ref.dtype)
        lse_ref[res, rows, :] = lse

    def body(t, carry):
        for u in range(INTERLEAVE):
            one_block(t * INTERLEAVE + u)
        return carry

    lax.fori_loop(0, n_res * blocks_per_chunk // INTERLEAVE, body, 0)


def _dilated_group(q, k, v, bias, d):
    b_, d_, n, _ = q.shape
    assert d_ == d
    kb = bias.shape[-1]
    chunk = min(n, MAX_CHUNK_A)
    n_res = min(d, MAX_CHUNK_A // chunk)
    assert n % QB == 0 and n % chunk == 0 and kb == min(2 * QB, n) and (n == QB or n >= 2 * QB)
    assert d % n_res == 0 and (n_res * chunk // QB) % INTERLEAVE == 0
    qspec = pl.BlockSpec((None, n_res, chunk, LANE_TILE), lambda b, r, i: (b, r, i, 0))
    kvspec = pl.BlockSpec((None, n_res, n, LANE_TILE), lambda b, r, i: (b, r, 0, 0))
    return pl.pallas_call(
        functools.partial(_attn_a_kernel, n=n, kb=kb),
        grid=(b_, d // n_res, n // chunk),
        in_specs=[qspec, kvspec, kvspec, pl.BlockSpec(bias.shape, lambda b, r, i: (0, 0, 0))],
        out_specs=[qspec, qspec],
        out_shape=[jax.ShapeDtypeStruct(q.shape, BF16), jax.ShapeDtypeStruct(q.shape, F32)],
        compiler_params=pltpu.CompilerParams(vmem_limit_bytes=VMEM_LIMIT),
        name=f"dilated_attention_d{d}",
    )(q, k, v, bias)


def _t5_bucket(rel):
    nb = N_BUCKETS // 2
    max_exact = nb // 2
    ret = (rel > 0).astype(np.int32) * nb
    n = np.abs(rel)
    large = max_exact + (np.log(np.maximum(n, 1) / max_exact) / np.log(MAX_DISTANCE / max_exact) * (nb - max_exact)).astype(np.int32)
    large = np.minimum(large, nb - 1)
    return (ret + np.where(n < max_exact, n, large)).astype(np.int32)


def _bias_a(table, g, w, d, n):
    half = w // (2 * d)
    assert half == HALO_A
    kb = min(2 * QB, n)
    sl = slice(g * HEADS_PER_GROUP, (g + 1) * HEADS_PER_GROUP)
    offs = np.array((0, -HALO_A, QB - kb))
    rel = np.arange(QB + kb - 1)[None, :] - (QB - 1) + offs[:, None]
    f = jnp.transpose(table[_t5_bucket(d * rel)][:, :, sl], (0, 2, 1)).astype(F32) * LOG2E
    f = jnp.where((np.abs(rel) <= half)[:, None, :], f, NEG)
    return _slab_major(_toeplitz(f, QB, kb).reshape(3, HEADS_PER_GROUP * QB, kb))


def _attn_b_kernel(q_ref, k_ref, v_ref, bias_ref, o_ref, *, rows):
    i = pl.program_id(2)

    def one_row(a):
        r = i * ROWS_PER_STEP_B + a
        band = jnp.clip(r - KH_MAX // 2, 0, rows - KH_MAX)
        start = pl.multiple_of(band * GRID_W, GRID_W)
        qrows = pl.ds(pl.multiple_of(a * GRID_W, GRID_W), GRID_W)
        o, _ = _attn_core(q_ref[qrows, :], k_ref[pl.ds(start, KB_B), :], v_ref[pl.ds(start, KB_B), :],
                          bias_ref[r - band])
        o_ref[qrows, :] = o.astype(o_ref.dtype)

    def body(t, carry):
        for u in range(INTERLEAVE_B):
            one_row(t * INTERLEAVE_B + u)
        return carry

    lax.fori_loop(0, ROWS_PER_STEP_B // INTERLEAVE_B, body, 0)


def _neighborhood(qb, kb, vb, bias, layer):
    b_, l_, _ = qb.shape
    rows = l_ // GRID_W
    assert rows % ROWS_PER_STEP_B == 0 and rows >= KH_MAX
    halves = WIDTH_B // LANE_TILE
    qspec = pl.BlockSpec((None, ROWS_PER_STEP_B * GRID_W, LANE_TILE), lambda hf, b, i: (b, i, hf))
    kvspec = pl.BlockSpec((None, l_, LANE_TILE), lambda hf, b, i: (b, 0, hf))
    bspec = pl.BlockSpec((None, KH_MAX, None, HEADS_PER_GROUP * GRID_W, KB_B), lambda hf, b, i: (layer, 0, hf, 0, 0))
    return pl.pallas_call(
        functools.partial(_attn_b_kernel, rows=rows),
        grid=(halves, b_, rows // ROWS_PER_STEP_B),
        in_specs=[qspec, kvspec, kvspec, bspec],
        out_specs=qspec,
        out_shape=jax.ShapeDtypeStruct((b_, l_, WIDTH_B), BF16),
        compiler_params=pltpu.CompilerParams(vmem_limit_bytes=VMEM_LIMIT),
        name="neighborhood_attention",
    )(qb, kb, vb, bias)


def _bias_b(rpb):
    cols = np.arange(GRID_W)
    c_start = np.clip(cols - KW // 2, 0, GRID_W - KW)
    col_ok = (cols[None, :] >= c_start[:, None]) & (cols[None, :] < c_start[:, None] + KW)
    assert np.all(np.abs(cols[None, :] - cols[:, None])[col_ok] <= KW - 1)
    pad = GRID_W - KW
    toep = _toeplitz(jnp.pad(rpb.astype(F32) * LOG2E, ((0, 0), (0, 0), (0, 0), (pad, pad))), GRID_W, GRID_W)
    toep = jnp.transpose(jnp.where(col_ok, toep, NEG), (0, 1, 3, 2, 4))
    shape = (DEPTH, WIDTH_B // LANE_TILE, HEADS_PER_GROUP * GRID_W, KB_B)
    return _slab_major(
        jnp.stack([toep[:, :, :, KH_MAX - 1 - v:2 * KH_MAX - 1 - v].reshape(shape) for v in range(KH_MAX)], axis=1))


def _post_kernel(x_ref, mod_ref, n1_ref, n2_ref, oa0_ref, oa1_ref, oa2_ref, l0_ref, l1_ref, l2_ref, ob_ref,
                 wg_ref, bg_ref, wua_ref, wub_ref, wo_ref, w1_ref, w2_ref, out_ref, *scr):
    t = x_ref.shape[0]
    x = x_ref[...]
    sh1, sc1, g1 = mod_ref[0:1, :], mod_ref[1:2, :], mod_ref[2:3, :]
    sh2, sc2, g2 = mod_ref[3:4, :], mod_ref[4:5, :], mod_ref[5:6, :]
    h = (_rms(x, n1_ref[...]) * (1.0 + sc1) + sh1).astype(BF16)
    gate = jax.nn.sigmoid(jnp.dot(h, wg_ref[...], preferred_element_type=F32) + bg_ref[...])

    def token_order(ref, scr_refs):
        d = ref.shape[0]
        if d == 1:
            return ref[0].astype(F32)
        for r in range(d):
            for s_ref, lanes in zip(scr_refs, LANE_HALVES):
                s_ref[pl.ds(r, t // d, stride=d), :] = ref[r, :, lanes].astype(F32)
        return jnp.concatenate([s_ref[...] for s_ref in scr_refs], axis=-1)

    oa0, oa1, oa2 = token_order(oa0_ref, None), token_order(oa1_ref, scr[0:2]), token_order(oa2_ref, scr[2:4])
    l0, l1, l2 = token_order(l0_ref, None), token_order(l1_ref, scr[4:6]), token_order(l2_ref, scr[6:8])
    mx = jnp.maximum(jnp.maximum(l0, l1), l2)
    e0, e1, e2 = jnp.exp(l0 - mx), jnp.exp(l1 - mx), jnp.exp(l2 - mx)
    inv = 1.0 / (e0 + e1 + e2)
    oa = (e0 * inv) * oa0 + (e1 * inv) * oa1 + (e2 * inv) * oa2

    ua = jnp.dot(oa.astype(BF16), wua_ref[...], preferred_element_type=F32)
    ub = jnp.dot(ob_ref[...], wub_ref[...], preferred_element_type=F32)
    mixed = gate[:, :D_MODEL] * ua + gate[:, D_MODEL:] * ub
    x1 = x + g1 * jnp.dot(mixed.astype(BF16), wo_ref[...], preferred_element_type=F32)

    h2 = (_rms(x1, n2_ref[...]) * (1.0 + sc2) + sh2).astype(BF16)
    acc = jnp.zeros_like(x1)
    for c in range(D_FF // FF_CHUNK):
        f = jnp.dot(h2, w1_ref[:, c * FF_CHUNK:(c + 1) * FF_CHUNK], preferred_element_type=F32)
        f = jnp.square(jnp.maximum(f, 0.0)).astype(BF16)
        acc = acc + jnp.dot(f, w2_ref[c * FF_CHUNK:(c + 1) * FF_CHUNK, :], preferred_element_type=F32)
    out_ref[...] = x1 + g2 * acc


def _post_attention(x, mod, n1, n2, oas, lses, ob, wg, bg, wua, wub, wo, w1, w2, layer):
    b_, l_, _ = x.shape
    t = TOK_TILE_POST
    per_b = l_ // t
    tok = lambda w: pl.BlockSpec((None, t, w), lambda i: (i // per_b, i % per_b, 0))
    dil_specs = []
    for _, d in DIL_GROUPS:
        assert t % (BF16_SUBLANES * d) == 0
        dil_specs.append(pl.BlockSpec((None, d, t // d, LANE_TILE), lambda i: (i // per_b, 0, i % per_b, 0)))
    return pl.pallas_call(
        _post_kernel,
        grid=(b_ * per_b,),
        in_specs=[
            tok(D_MODEL),
            pl.BlockSpec((None, None, N_MOD, D_MODEL), lambda i: (layer, i // per_b, 0, 0)),
            _layer_spec((1, D_MODEL), layer), _layer_spec((1, D_MODEL), layer),
            *dil_specs, *dil_specs,
            tok(WIDTH_B),
            _layer_spec((D_MODEL, 2 * D_MODEL), layer), _layer_spec((1, 2 * D_MODEL), layer),
            _layer_spec((WIDTH_A, D_MODEL), layer), _layer_spec((WIDTH_B, D_MODEL), layer),
            _layer_spec((D_MODEL, D_MODEL), layer),
            _layer_spec((D_MODEL, D_FF), layer), _layer_spec((D_FF, D_MODEL), layer),
        ],
        out_specs=tok(D_MODEL),
        out_shape=jax.ShapeDtypeStruct(x.shape, F32),
        scratch_shapes=[pltpu.VMEM((t, VREG_LANES), F32)] * 8,
        compiler_params=pltpu.CompilerParams(vmem_limit_bytes=VMEM_LIMIT),
        name="post_attention_mlp",
    )(x, mod, n1, n2, *oas, *lses, ob, wg, bg, wua, wub, wo, w1, w2)


def _head_sum_matrix():
    head = np.arange(LANE_TILE) // HEAD_DIM
    return jnp.asarray((head[:, None] == head[None, :]).astype(np.float32), dtype=BF16)


def kernel(x_prompt, x_sample, c_prompt, c_sample, norm1_g, norm2_g, w_mod, b_mod, w_in, q_norm_g, k_norm_g,
           rel_bias, rpb, w_gate, b_gate, w_up_a, w_up_b, w_o, w_ff1, w_ff2):
    n_prompt = c_prompt.shape[0]
    mod_all = _modulation(jnp.concatenate([c_prompt, c_sample], axis=0), w_mod, b_mod)
    mod_all = mod_all.reshape(DEPTH, -1, N_MOD, D_MODEL)
    ones = _head_sum_matrix()
    qg = (q_norm_g.reshape(DEPTH, 1, QKV_PART) * (HEAD_DIM ** -0.5 * LOG2E)).astype(F32)
    kg = k_norm_g.reshape(DEPTH, 1, QKV_PART).astype(F32)
    n1 = norm1_g.reshape(DEPTH, 1, D_MODEL)
    n2 = norm2_g.reshape(DEPTH, 1, D_MODEL)
    bg = b_gate.reshape(DEPTH, 1, 2 * D_MODEL)
    w_in_b, w_gate_b = w_in.astype(BF16), w_gate.astype(BF16)
    w_up_a_b, w_up_b_b, w_o_b = w_up_a.astype(BF16), w_up_b.astype(BF16), w_o.astype(BF16)
    w_ff1_b, w_ff2_b = w_ff1.astype(BF16), w_ff2.astype(BF16)

    xs = (x_prompt, x_sample)
    bias_b = _bias_b(rpb)
    bias_a = {}
    for x in xs:
        for g, (w, d) in enumerate(DIL_GROUPS):
            n = x.shape[1] // d
            if (g, min(2 * QB, n)) not in bias_a:
                bias_a[(g, min(2 * QB, n))] = _bias_a(rel_bias, g, w, d, n)

    ys = []
    for x, mod_x in ((x_prompt, mod_all[:, :n_prompt]), (x_sample, mod_all[:, n_prompt:])):
        l_ = x.shape[1]
        for l in range(DEPTH):
            q, k, v = _pre_attention(x, mod_x, n1, w_in_b, qg, kg, ones, l)
            oas, lses = [], []
            for g, (w, d) in enumerate(DIL_GROUPS):
                o, lse = _dilated_group(q[g], k[g], v[g], bias_a[(g, min(2 * QB, l_ // d))], d)
                oas.append(o)
                lses.append(lse)
            ob = _neighborhood(q[N_GROUPS], k[N_GROUPS], v[N_GROUPS], bias_b, l)
            x = _post_attention(x, mod_x, n1, n2, oas, lses, ob, w_gate_b, bg, w_up_a_b, w_up_b_b, w_o_b,
                                w_ff1_b, w_ff2_b, l)
        ys.append(x)
    return tuple(ys)
```

```python
import functools

import jax
import jax.numpy as jnp
import numpy as np
from jax import lax
from jax.experimental import pallas as pl
from jax.experimental.pallas import tpu as pltpu

D_MODEL = 1024
DEPTH = 2
HEAD_DIM = 64
DIL_GROUPS = ((128, 1), (512, 4), (2048, 16))
HEADS_PER_GROUP = 4
N_GROUPS = len(DIL_GROUPS)
N_HEADS_A = HEADS_PER_GROUP * N_GROUPS
N_HEADS_B = 8
N_HEADS = N_HEADS_A + N_HEADS_B
WIDTH_A = HEADS_PER_GROUP * HEAD_DIM
WIDTH_B = N_HEADS_B * HEAD_DIM
QKV_PART = N_HEADS * HEAD_DIM
N_BUCKETS = 32
MAX_DISTANCE = 1024
GRID_W = 64
KH_MAX = 8
KW = 16
D_FF = 4 * D_MODEL
N_MOD = 6
EPS = 1e-6
NEG = -1e30
LOG2E = 1.4426950408889634

LANE_TILE = 256
QB = 128
SLAB = 16
HALO_A = 64
KB_B = KH_MAX * GRID_W
ROWS_PER_STEP_B = 32
MAX_CHUNK_A = 2048
INTERLEAVE_B = 16
INTERLEAVE = 16
TOK_TILE_PRE = 1024
TOK_TILE_POST = 512
FF_CHUNK = 1024
BF16_SUBLANES = 16
VREG_LANES = 128
LANE_HALVES = (slice(0, VREG_LANES), slice(VREG_LANES, LANE_TILE))
MOD_COL_BLOCK = 1536
VMEM_LIMIT = 56 * 1024 * 1024

F32 = jnp.float32
BF16 = jnp.bfloat16


def _rms(x, g):
    return x * lax.rsqrt(jnp.mean(x * x, axis=-1, keepdims=True) + EPS) * g


def _const_spec(shape):
    return pl.BlockSpec(shape, lambda *_: (0,) * len(shape), pipeline_mode=pl.Buffered(1))


def _layer_spec(shape, layer):
    return pl.BlockSpec((None,) + shape, lambda *_: (layer,) + (0,) * len(shape), pipeline_mode=pl.Buffered(1))


def _mod_kernel(c_ref, w_ref, b_ref, o_ref):
    c = c_ref[...]
    s = c * jax.nn.sigmoid(c)
    o_ref[...] = jnp.dot(s, w_ref[...], preferred_element_type=F32) + b_ref[...]


def _modulation(c_all, w_mod, b_mod):
    nb = c_all.shape[0]
    ncol = N_MOD * D_MODEL
    blk = MOD_COL_BLOCK
    assert ncol % blk == 0
    return pl.pallas_call(
        _mod_kernel,
        grid=(DEPTH, ncol // blk),
        in_specs=[
            pl.BlockSpec((nb, D_MODEL), lambda l, j: (0, 0)),
            pl.BlockSpec((None, D_MODEL, blk), lambda l, j: (l, 0, j)),
            pl.BlockSpec((None, 1, blk), lambda l, j: (l, 0, j)),
        ],
        out_specs=pl.BlockSpec((None, nb, blk), lambda l, j: (l, 0, j)),
        out_shape=jax.ShapeDtypeStruct((DEPTH, nb, ncol), F32),
        compiler_params=pltpu.CompilerParams(vmem_limit_bytes=VMEM_LIMIT),
        name="modulation",
    )(c_all, w_mod, b_mod.reshape(DEPTH, 1, ncol))


def _pre_kernel(x_ref, mod_ref, n1_ref, win_ref, qg_ref, kg_ref, ones_ref, *refs):
    scr_refs = refs[-2:]
    outs = refs[:-2]
    t = x_ref.shape[0]
    x = x_ref[...]
    sh1 = mod_ref[0:1, :]
    sc1 = mod_ref[1:2, :]
    h = (_rms(x, n1_ref[...]) * (1.0 + sc1) + sh1).astype(BF16)
    ones = ones_ref[...]
    n_chunk = QKV_PART // LANE_TILE
    per_part = N_GROUPS + 1

    def project(part):
        return jnp.dot(h, win_ref[:, part * QKV_PART:(part + 1) * QKV_PART], preferred_element_type=F32)

    def finish(part, g_ref, y_part):
        for c in range(n_chunk):
            y = y_part[:, c * LANE_TILE:(c + 1) * LANE_TILE]
            if g_ref is not None:
                ss = jnp.dot((y * y).astype(BF16), ones, preferred_element_type=F32)
                y = y * lax.rsqrt(ss * (1.0 / HEAD_DIM) + EPS) * g_ref[:, c * LANE_TILE:(c + 1) * LANE_TILE]
            if c < N_GROUPS:
                out = outs[part * per_part + c]
                d = DIL_GROUPS[c][1]
                if d == 1:
                    out[0] = y.astype(out.dtype)
                else:
                    for s_ref, lanes in zip(scr_refs, LANE_HALVES):
                        s_ref[...] = y[:, lanes]
                    for r in range(d):
                        for s_ref, lanes in zip(scr_refs, LANE_HALVES):
                            out[r, :, lanes] = s_ref[pl.ds(r, t // d, stride=d), :].astype(out.dtype)
            else:
                out = outs[part * per_part + N_GROUPS]
                cb = c - N_GROUPS
                out[:, cb * LANE_TILE:(cb + 1) * LANE_TILE] = y.astype(out.dtype)

    for part, g_ref in enumerate((qg_ref, kg_ref, None)):
        finish(part, g_ref, project(part))


def _pre_attention(x, mod, n1, w_in, qg, kg, ones, layer):
    b_, l_, _ = x.shape
    t = TOK_TILE_PRE
    per_b = l_ // t
    tok = lambda w: pl.BlockSpec((None, t, w), lambda i: (i // per_b, i % per_b, 0))
    out_specs, out_shape = [], []
    for _ in range(3):
        for _, d in DIL_GROUPS:
            assert t % (BF16_SUBLANES * d) == 0
            out_specs.append(pl.BlockSpec((None, d, t // d, LANE_TILE), lambda i: (i // per_b, 0, i % per_b, 0)))
            out_shape.append(jax.ShapeDtypeStruct((b_, d, l_ // d, LANE_TILE), BF16))
        out_specs.append(tok(WIDTH_B))
        out_shape.append(jax.ShapeDtypeStruct((b_, l_, WIDTH_B), BF16))
    outs = pl.pallas_call(
        _pre_kernel,
        grid=(b_ * per_b,),
        in_specs=[
            tok(D_MODEL),
            pl.BlockSpec((None, None, N_MOD, D_MODEL), lambda i: (layer, i // per_b, 0, 0)),
            _layer_spec((1, D_MODEL), layer),
            _layer_spec((D_MODEL, 3 * QKV_PART), layer),
            _layer_spec((1, QKV_PART), layer),
            _layer_spec((1, QKV_PART), layer),
            _const_spec((LANE_TILE, LANE_TILE)),
        ],
        out_specs=out_specs,
        out_shape=out_shape,
        scratch_shapes=[pltpu.VMEM((t, VREG_LANES), F32)] * 2,
        compiler_params=pltpu.CompilerParams(vmem_limit_bytes=VMEM_LIMIT),
        name="pre_attention",
    )(x, mod, n1, w_in, qg, kg, ones)
    per_part = N_GROUPS + 1
    return [outs[p * per_part:(p + 1) * per_part] for p in range(3)]


def _attn_core(q, k, v, bias):
    nq = q.shape[0]
    head_of_lane = lax.broadcasted_iota(jnp.int32, (SLAB, LANE_TILE), 1) // HEAD_DIM
    zero = jnp.zeros((SLAB, LANE_TILE), q.dtype)
    qs = jnp.concatenate([jnp.where(head_of_lane == h, q[j:j + SLAB, :], zero)
                          for j in range(0, nq, SLAB) for h in range(HEADS_PER_GROUP)], axis=0)
    s = lax.dot_general(qs, k, (((1,), (1,)), ((), ())), preferred_element_type=F32) + bias
    m = jnp.max(s, axis=-1, keepdims=True)
    p = jnp.exp2(s - m)
    l = jnp.sum(p, axis=-1, keepdims=True)
    o4 = jnp.dot(p.astype(BF16), v, preferred_element_type=F32)

    def unstack(x4):
        slabs = []
        for j in range(0, HEADS_PER_GROUP * nq, HEADS_PER_GROUP * SLAB):
            x = jnp.broadcast_to(x4[j:j + SLAB, :], (SLAB, LANE_TILE))
            for h in range(1, HEADS_PER_GROUP):
                x = jnp.where(head_of_lane == h, x4[j + h * SLAB:j + (h + 1) * SLAB, :], x)
            slabs.append(x)
        return jnp.concatenate(slabs, axis=0)

    return unstack(o4), unstack(m), unstack(l)


def _slab_major(bias):
    lead, (rows, nk) = bias.shape[:-2], bias.shape[-2:]
    b = bias.reshape(lead + (HEADS_PER_GROUP, rows // (HEADS_PER_GROUP * SLAB), SLAB, nk))
    return jnp.swapaxes(b, -4, -3).reshape(bias.shape)


def _toeplitz(f, nq, nk):
    p = nq + nk - 1
    assert f.shape[-1] == p and nq >= 2
    u = jnp.roll(f, -(nq - 1), axis=-1)
    flat = jnp.tile(u, nq)[..., :nq * (p - 1)]
    return flat.reshape(f.shape[:-1] + (nq, p - 1))[..., :nk]


def _attn_a_kernel(q_ref, k_ref, v_ref, bias_ref, o_ref, m_ref, l_ref, *, n, kb):
    i = pl.program_id(2)
    n_res, chunk, _ = q_ref.shape
    blocks_per_chunk = chunk // QB
    last_block = n // QB - 1

    def one_block(item):
        res, j = (0, item) if n_res == 1 else (item // blocks_per_chunk, item % blocks_per_chunk)
        blk = i * blocks_per_chunk + j
        start = pl.multiple_of(jnp.clip(blk * QB - HALO_A, 0, n - kb), HALO_A)
        var = jnp.where(blk == 0, 0, jnp.where(blk == last_block, 2, 1))
        rows = pl.ds(pl.multiple_of(j * QB, QB), QB)
        o, m, l = _attn_core(q_ref[res, rows, :], k_ref[res, pl.ds(start, kb), :], v_ref[res, pl.ds(start, kb), :],
                             bias_ref[var])
        o_ref[res, rows, :] = o.astype(o_ref.dtype)
        m_ref[res, rows, :] = m
        l_ref[res, rows, :] = l

    def body(t, carry):
        for u in range(INTERLEAVE):
            one_block(t * INTERLEAVE + u)
        return carry

    lax.fori_loop(0, n_res * blocks_per_chunk // INTERLEAVE, body, 0)


def _dilated_group(q, k, v, bias, d):
    b_, d_, n, _ = q.shape
    assert d_ == d
    kb = bias.shape[-1]
    chunk = min(n, MAX_CHUNK_A)
    n_res = min(d, MAX_CHUNK_A // chunk)
    assert n % QB == 0 and n % chunk == 0 and kb == min(2 * QB, n) and (n == QB or n >= 2 * QB)
    assert d % n_res == 0 and (n_res * chunk // QB) % INTERLEAVE == 0
    qspec = pl.BlockSpec((None, n_res, chunk, LANE_TILE), lambda b, r, i: (b, r, i, 0))
    kvspec = pl.BlockSpec((None, n_res, n, LANE_TILE), lambda b, r, i: (b, r, 0, 0))
    return pl.pallas_call(
        functools.partial(_attn_a_kernel, n=n, kb=kb),
        grid=(b_, d // n_res, n // chunk),
        in_specs=[qspec, kvspec, kvspec, pl.BlockSpec(bias.shape, lambda b, r, i: (0, 0, 0))],
        out_specs=[qspec, qspec, qspec],
        out_shape=[jax.ShapeDtypeStruct(q.shape, BF16), jax.ShapeDtypeStruct(q.shape, F32),
                   jax.ShapeDtypeStruct(q.shape, F32)],
        compiler_params=pltpu.CompilerParams(vmem_limit_bytes=VMEM_LIMIT),
        name=f"dilated_attention_d{d}",
    )(q, k, v, bias)


def _t5_bucket(rel):
    nb = N_BUCKETS // 2
    max_exact = nb // 2
    ret = (rel > 0).astype(np.int32) * nb
    n = np.abs(rel)
    large = max_exact + (np.log(np.maximum(n, 1) / max_exact) / np.log(MAX_DISTANCE / max_exact) * (nb - max_exact)).astype(np.int32)
    large = np.minimum(large, nb - 1)
    return (ret + np.where(n < max_exact, n, large)).astype(np.int32)


def _bias_a(table, g, w, d, n):
    half = w // (2 * d)
    assert half == HALO_A
    kb = min(2 * QB, n)
    sl = slice(g * HEADS_PER_GROUP, (g + 1) * HEADS_PER_GROUP)
    offs = np.array((0, -HALO_A, QB - kb))
    rel = np.arange(QB + kb - 1)[None, :] - (QB - 1) + offs[:, None]
    f = jnp.transpose(table[_t5_bucket(d * rel)][:, :, sl], (0, 2, 1)).astype(F32) * LOG2E
    f = jnp.where((np.abs(rel) <= half)[:, None, :], f, NEG)
    return _slab_major(_toeplitz(f, QB, kb).reshape(3, HEADS_PER_GROUP * QB, kb))


def _attn_b_kernel(q_ref, k_ref, v_ref, bias_ref, o_ref, *, rows):
    i = pl.program_id(2)

    def one_row(a):
        r = i * ROWS_PER_STEP_B + a
        band = jnp.clip(r - KH_MAX // 2, 0, rows - KH_MAX)
        start = pl.multiple_of(band * GRID_W, GRID_W)
        qrows = pl.ds(pl.multiple_of(a * GRID_W, GRID_W), GRID_W)
        o, _, l = _attn_core(q_ref[qrows, :], k_ref[pl.ds(start, KB_B), :], v_ref[pl.ds(start, KB_B), :],
                             bias_ref[r - band])
        o_ref[qrows, :] = (o * (1.0 / l)).astype(o_ref.dtype)

    def body(t, carry):
        for u in range(INTERLEAVE_B):
            one_row(t * INTERLEAVE_B + u)
        return carry

    lax.fori_loop(0, ROWS_PER_STEP_B // INTERLEAVE_B, body, 0)


def _neighborhood(qb, kb, vb, bias, layer):
    b_, l_, _ = qb.shape
    rows = l_ // GRID_W
    assert rows % ROWS_PER_STEP_B == 0 and rows >= KH_MAX
    halves = WIDTH_B // LANE_TILE
    qspec = pl.BlockSpec((None, ROWS_PER_STEP_B * GRID_W, LANE_TILE), lambda hf, b, i: (b, i, hf))
    kvspec = pl.BlockSpec((None, l_, LANE_TILE), lambda hf, b, i: (b, 0, hf))
    bspec = pl.BlockSpec((None, KH_MAX, None, HEADS_PER_GROUP * GRID_W, KB_B), lambda hf, b, i: (layer, 0, hf, 0, 0))
    return pl.pallas_call(
        functools.partial(_attn_b_kernel, rows=rows),
        grid=(halves, b_, rows // ROWS_PER_STEP_B),
        in_specs=[qspec, kvspec, kvspec, bspec],
        out_specs=qspec,
        out_shape=jax.ShapeDtypeStruct((b_, l_, WIDTH_B), BF16),
        compiler_params=pltpu.CompilerParams(vmem_limit_bytes=VMEM_LIMIT),
        name="neighborhood_attention",
    )(qb, kb, vb, bias)


def _bias_b(rpb):
    cols = np.arange(GRID_W)
    c_start = np.clip(cols - KW // 2, 0, GRID_W - KW)
    col_ok = (cols[None, :] >= c_start[:, None]) & (cols[None, :] < c_start[:, None] + KW)
    assert np.all(np.abs(cols[None, :] - cols[:, None])[col_ok] <= KW - 1)
    pad = GRID_W - KW
    toep = _toeplitz(jnp.pad(rpb.astype(F32) * LOG2E, ((0, 0), (0, 0), (0, 0), (pad, pad))), GRID_W, GRID_W)
    halves = WIDTH_B // LANE_TILE
    toep = jnp.where(col_ok, toep, NEG).reshape(DEPTH, halves, HEADS_PER_GROUP, 2 * KH_MAX - 1,
                                                GRID_W // SLAB, SLAB, GRID_W)
    toep = jnp.transpose(toep, (0, 1, 4, 2, 5, 3, 6))
    shape = (DEPTH, halves, HEADS_PER_GROUP * GRID_W, KB_B)
    return jnp.stack([toep[..., KH_MAX - 1 - v:2 * KH_MAX - 1 - v, :].reshape(shape) for v in range(KH_MAX)], axis=1)


def _post_kernel(x_ref, mod_ref, n1_ref, n2_ref, *refs):
    a_refs, refs = refs[:3 * N_GROUPS], refs[3 * N_GROUPS:]
    ob_ref, wg_ref, bg_ref, wua_ref, wub_ref, wo_ref, w1_ref, w2_ref, out_ref = refs[:9]
    scr = refs[9:]
    t = x_ref.shape[0]
    x = x_ref[...]
    sh1, sc1, g1 = mod_ref[0:1, :], mod_ref[1:2, :], mod_ref[2:3, :]
    sh2, sc2, g2 = mod_ref[3:4, :], mod_ref[4:5, :], mod_ref[5:6, :]

    def token_order(ref, scr_refs):
        d = ref.shape[0]
        if d == 1:
            return ref[0].astype(F32)
        for r in range(d):
            for s_ref, lanes in zip(scr_refs, LANE_HALVES):
                s_ref[pl.ds(r, t // d, stride=d), :] = ref[r, :, lanes].astype(F32)
        return jnp.concatenate([s_ref[...] for s_ref in scr_refs], axis=-1)

    n_half = len(LANE_HALVES)
    dense, used = [], 0
    for ref in a_refs:
        dense.append(token_order(ref, scr[used:used + n_half]))
        used += n_half if ref.shape[0] > 1 else 0
    os_, ms, ls = dense[:N_GROUPS], dense[N_GROUPS:2 * N_GROUPS], dense[2 * N_GROUPS:]
    mx = functools.reduce(jnp.maximum, ms)
    es = [jnp.exp2(m - mx) for m in ms]
    num = functools.reduce(jnp.add, [e * o for e, o in zip(es, os_)])
    den = functools.reduce(jnp.add, [e * l for e, l in zip(es, ls)])
    oa = (num * (1.0 / den)).astype(BF16)

    h = (_rms(x, n1_ref[...]) * (1.0 + sc1) + sh1).astype(BF16)
    gate = jax.nn.sigmoid(jnp.dot(h, wg_ref[...], preferred_element_type=F32) + bg_ref[...])

    ua = jnp.dot(oa, wua_ref[...], preferred_element_type=F32)
    ub = jnp.dot(ob_ref[...], wub_ref[...], preferred_element_type=F32)
    mixed = gate[:, :D_MODEL] * ua + gate[:, D_MODEL:] * ub
    x1 = x + g1 * jnp.dot(mixed.astype(BF16), wo_ref[...], preferred_element_type=F32)

    h2 = (_rms(x1, n2_ref[...]) * (1.0 + sc2) + sh2).astype(BF16)
    acc = jnp.zeros_like(x1)
    for c in range(D_FF // FF_CHUNK):
        f = jnp.dot(h2, w1_ref[:, c * FF_CHUNK:(c + 1) * FF_CHUNK], preferred_element_type=F32)
        f = jnp.square(jnp.maximum(f, 0.0)).astype(BF16)
        acc = acc + jnp.dot(f, w2_ref[c * FF_CHUNK:(c + 1) * FF_CHUNK, :], preferred_element_type=F32)
    out_ref[...] = x1 + g2 * acc


def _post_attention(x, mod, n1, n2, attn_a, ob, wg, bg, wua, wub, wo, w1, w2, layer):
    b_, l_, _ = x.shape
    t = TOK_TILE_POST
    per_b = l_ // t
    tok = lambda w: pl.BlockSpec((None, t, w), lambda i: (i // per_b, i % per_b, 0))
    dil_specs = []
    for _, d in DIL_GROUPS:
        assert t % (BF16_SUBLANES * d) == 0
        dil_specs.append(pl.BlockSpec((None, d, t // d, LANE_TILE), lambda i: (i // per_b, 0, i % per_b, 0)))
    return pl.pallas_call(
        _post_kernel,
        grid=(b_ * per_b,),
        in_specs=[
            tok(D_MODEL),
            pl.BlockSpec((None, None, N_MOD, D_MODEL), lambda i: (layer, i // per_b, 0, 0)),
            _layer_spec((1, D_MODEL), layer), _layer_spec((1, D_MODEL), layer),
            *dil_specs, *dil_specs, *dil_specs,
            tok(WIDTH_B),
            _layer_spec((D_MODEL, 2 * D_MODEL), layer), _layer_spec((1, 2 * D_MODEL), layer),
            _layer_spec((WIDTH_A, D_MODEL), layer), _layer_spec((WIDTH_B, D_MODEL), layer),
            _layer_spec((D_MODEL, D_MODEL), layer),
            _layer_spec((D_MODEL, D_FF), layer), _layer_spec((D_FF, D_MODEL), layer),
        ],
        out_specs=tok(D_MODEL),
        out_shape=jax.ShapeDtypeStruct(x.shape, F32),
        scratch_shapes=[pltpu.VMEM((t, VREG_LANES), F32)] * (3 * len(LANE_HALVES) * sum(d > 1 for _, d in DIL_GROUPS)),
        compiler_params=pltpu.CompilerParams(vmem_limit_bytes=VMEM_LIMIT),
        name="post_attention_mlp",
    )(x, mod, n1, n2, *[a[j] for j in range(3) for a in attn_a], ob, wg, bg, wua, wub, wo, w1, w2)


def _head_sum_matrix():
    head = np.arange(LANE_TILE) // HEAD_DIM
    return jnp.asarray((head[:, None] == head[None, :]).astype(np.float32), dtype=BF16)


def kernel(x_prompt, x_sample, c_prompt, c_sample, norm1_g, norm2_g, w_mod, b_mod, w_in, q_norm_g, k_norm_g,
           rel_bias, rpb, w_gate, b_gate, w_up_a, w_up_b, w_o, w_ff1, w_ff2):
    n_prompt = c_prompt.shape[0]
    mod_all = _modulation(jnp.concatenate([c_prompt, c_sample], axis=0), w_mod, b_mod)
    mod_all = mod_all.reshape(DEPTH, -1, N_MOD, D_MODEL)
    ones = _head_sum_matrix()
    qg = (q_norm_g.reshape(DEPTH, 1, QKV_PART) * (HEAD_DIM ** -0.5 * LOG2E)).astype(F32)
    kg = k_norm_g.reshape(DEPTH, 1, QKV_PART).astype(F32)
    n1 = norm1_g.reshape(DEPTH, 1, D_MODEL)
    n2 = norm2_g.reshape(DEPTH, 1, D_MODEL)
    bg = b_gate.reshape(DEPTH, 1, 2 * D_MODEL)
    w_in_b, w_gate_b = w_in.astype(BF16), w_gate.astype(BF16)
    w_up_a_b, w_up_b_b, w_o_b = w_up_a.astype(BF16), w_up_b.astype(BF16), w_o.astype(BF16)
    w_ff1_b, w_ff2_b = w_ff1.astype(BF16), w_ff2.astype(BF16)

    xs = (x_prompt, x_sample)
    bias_b = _bias_b(rpb)
    bias_a = {}
    for x in xs:
        for g, (w, d) in enumerate(DIL_GROUPS):
            n = x.shape[1] // d
            if (g, min(2 * QB, n)) not in bias_a:
                bias_a[(g, min(2 * QB, n))] = _bias_a(rel_bias, g, w, d, n)

    ys = []
    for x, mod_x in ((x_prompt, mod_all[:, :n_prompt]), (x_sample, mod_all[:, n_prompt:])):
        l_ = x.shape[1]
        for l in range(DEPTH):
            q, k, v = _pre_attention(x, mod_x, n1, w_in_b, qg, kg, ones, l)
            attn_a = [_dilated_group(q[g], k[g], v[g], bias_a[(g, min(2 * QB, l_ // d))], d)
                      for g, (w, d) in enumerate(DIL_GROUPS)]
            ob = _neighborhood(q[N_GROUPS], k[N_GROUPS], v[N_GROUPS], bias_b, l)
            x = _post_attention(x, mod_x, n1, n2, attn_a, ob, w_gate_b, bg, w_up_a_b, w_up_b_b, w_o_b,
                                w_ff1_b, w_ff2_b, l)
        ys.append(x)
    return tuple(ys)
```

```python
import functools

import jax
import jax.numpy as jnp
import numpy as np
from jax import lax
from jax.experimental import pallas as pl
from jax.experimental.pallas import tpu as pltpu

D_MODEL = 1024
DEPTH = 2
HEAD_DIM = 64
DIL_GROUPS = ((128, 1), (512, 4), (2048, 16))
HEADS_PER_GROUP = 4
N_GROUPS = len(DIL_GROUPS)
N_HEADS_A = HEADS_PER_GROUP * N_GROUPS
N_HEADS_B = 8
N_HEADS = N_HEADS_A + N_HEADS_B
WIDTH_A = HEADS_PER_GROUP * HEAD_DIM
WIDTH_B = N_HEADS_B * HEAD_DIM
QKV_PART = N_HEADS * HEAD_DIM
N_BUCKETS = 32
MAX_DISTANCE = 1024
GRID_W = 64
KH_MAX = 8
KW = 16
D_FF = 4 * D_MODEL
N_MOD = 6
EPS = 1e-6
NEG = -1e30
LOG2E = 1.4426950408889634
LN2 = 0.6931471805599453

LANE_TILE = 256
QB = 128
SLAB = 16
HALO_A = 64
KB_B = KH_MAX * GRID_W
ROWS_PER_STEP_B = 32
MAX_CHUNK_A = 2048
INTERLEAVE_B = 16
INTERLEAVE = 16
TOK_TILE_PRE = 1024
TOK_TILE_POST = 512
FF_CHUNK = 1024
BF16_SUBLANES = 16
VREG_LANES = 128
LANE_HALVES = (slice(0, VREG_LANES), slice(VREG_LANES, LANE_TILE))
MOD_COL_BLOCK = 1536
VMEM_LIMIT = 56 * 1024 * 1024

F32 = jnp.float32
BF16 = jnp.bfloat16


def _rms(x, g):
    return x * lax.rsqrt(jnp.mean(x * x, axis=-1, keepdims=True) + EPS) * g


def _const_spec(shape):
    return pl.BlockSpec(shape, lambda *_: (0,) * len(shape), pipeline_mode=pl.Buffered(1))


def _layer_spec(shape, layer):
    return pl.BlockSpec((None,) + shape, lambda *_: (layer,) + (0,) * len(shape), pipeline_mode=pl.Buffered(1))


def _mod_kernel(c_ref, w_ref, b_ref, o_ref):
    c = c_ref[...]
    s = c * jax.nn.sigmoid(c)
    o_ref[...] = jnp.dot(s, w_ref[...], preferred_element_type=F32) + b_ref[...]


def _modulation(c_all, w_mod, b_mod):
    nb = c_all.shape[0]
    ncol = N_MOD * D_MODEL
    blk = MOD_COL_BLOCK
    assert ncol % blk == 0
    return pl.pallas_call(
        _mod_kernel,
        grid=(DEPTH, ncol // blk),
        in_specs=[
            pl.BlockSpec((nb, D_MODEL), lambda l, j: (0, 0)),
            pl.BlockSpec((None, D_MODEL, blk), lambda l, j: (l, 0, j)),
            pl.BlockSpec((None, 1, blk), lambda l, j: (l, 0, j)),
        ],
        out_specs=pl.BlockSpec((None, nb, blk), lambda l, j: (l, 0, j)),
        out_shape=jax.ShapeDtypeStruct((DEPTH, nb, ncol), F32),
        compiler_params=pltpu.CompilerParams(vmem_limit_bytes=VMEM_LIMIT),
        name="modulation",
    )(c_all, w_mod, b_mod.reshape(DEPTH, 1, ncol))


def _pre_kernel(x_ref, mod_ref, n1_ref, win_ref, qg_ref, kg_ref, ones_ref, *refs):
    scr_refs = refs[-2:]
    outs = refs[:-2]
    t = x_ref.shape[0]
    x = x_ref[...]
    sh1 = mod_ref[0:1, :]
    sc1 = mod_ref[1:2, :]
    h = (_rms(x, n1_ref[...]) * (1.0 + sc1) + sh1).astype(BF16)
    ones = ones_ref[...]
    n_chunk = QKV_PART // LANE_TILE
    per_part = N_GROUPS + 1

    def project(part):
        return jnp.dot(h, win_ref[:, part * QKV_PART:(part + 1) * QKV_PART], preferred_element_type=F32)

    def finish(part, g_ref, y_part):
        for c in range(n_chunk):
            y = y_part[:, c * LANE_TILE:(c + 1) * LANE_TILE]
            if g_ref is not None:
                ss = jnp.dot((y * y).astype(BF16), ones, preferred_element_type=F32)
                y = y * lax.rsqrt(ss * (1.0 / HEAD_DIM) + EPS) * g_ref[:, c * LANE_TILE:(c + 1) * LANE_TILE]
            if c < N_GROUPS:
                out = outs[part * per_part + c]
                d = DIL_GROUPS[c][1]
                if d == 1:
                    out[0] = y.astype(out.dtype)
                else:
                    for s_ref, lanes in zip(scr_refs, LANE_HALVES):
                        s_ref[...] = y[:, lanes]
                    for r in range(d):
                        for s_ref, lanes in zip(scr_refs, LANE_HALVES):
                            out[r, :, lanes] = s_ref[pl.ds(r, t // d, stride=d), :].astype(out.dtype)
            else:
                out = outs[part * per_part + N_GROUPS]
                cb = c - N_GROUPS
                out[:, cb * LANE_TILE:(cb + 1) * LANE_TILE] = y.astype(out.dtype)

    for part, g_ref in enumerate((qg_ref, kg_ref, None)):
        finish(part, g_ref, project(part))


def _pre_attention(x, mod, n1, w_in, qg, kg, ones, layer):
    b_, l_, _ = x.shape
    t = TOK_TILE_PRE
    per_b = l_ // t
    tok = lambda w: pl.BlockSpec((None, t, w), lambda i: (i // per_b, i % per_b, 0))
    out_specs, out_shape = [], []
    for _ in range(3):
        for _, d in DIL_GROUPS:
            assert t % (BF16_SUBLANES * d) == 0
            out_specs.append(pl.BlockSpec((None, d, t // d, LANE_TILE), lambda i: (i // per_b, 0, i % per_b, 0)))
            out_shape.append(jax.ShapeDtypeStruct((b_, d, l_ // d, LANE_TILE), BF16))
        out_specs.append(tok(WIDTH_B))
        out_shape.append(jax.ShapeDtypeStruct((b_, l_, WIDTH_B), BF16))
    outs = pl.pallas_call(
        _pre_kernel,
        grid=(b_ * per_b,),
        in_specs=[
            tok(D_MODEL),
            pl.BlockSpec((None, None, N_MOD, D_MODEL), lambda i: (layer, i // per_b, 0, 0)),
            _layer_spec((1, D_MODEL), layer),
            _layer_spec((D_MODEL, 3 * QKV_PART), layer),
            _layer_spec((1, QKV_PART), layer),
            _layer_spec((1, QKV_PART), layer),
            _const_spec((LANE_TILE, LANE_TILE)),
        ],
        out_specs=out_specs,
        out_shape=out_shape,
        scratch_shapes=[pltpu.VMEM((t, VREG_LANES), F32)] * 2,
        compiler_params=pltpu.CompilerParams(vmem_limit_bytes=VMEM_LIMIT),
        name="pre_attention",
    )(x, mod, n1, w_in, qg, kg, ones)
    per_part = N_GROUPS + 1
    return [outs[p * per_part:(p + 1) * per_part] for p in range(3)]


def _attn_core(q, k, v, bias):
    nq = q.shape[0]
    head_of_lane = lax.broadcasted_iota(jnp.int32, (SLAB, LANE_TILE), 1) // HEAD_DIM
    zero = jnp.zeros((SLAB, LANE_TILE), q.dtype)
    qs = jnp.concatenate([jnp.where(head_of_lane == h, q[j:j + SLAB, :], zero)
                          for j in range(0, nq, SLAB) for h in range(HEADS_PER_GROUP)], axis=0)
    s = lax.dot_general(qs, k, (((1,), (1,)), ((), ())), preferred_element_type=F32) + bias
    m = jnp.max(s, axis=-1, keepdims=True)
    p = jnp.exp2(s - m)
    l = jnp.sum(p, axis=-1, keepdims=True)
    o4 = jnp.dot(p.astype(BF16), v, preferred_element_type=F32)

    def unstack(x4):
        slabs = []
        for j in range(0, HEADS_PER_GROUP * nq, HEADS_PER_GROUP * SLAB):
            x = jnp.broadcast_to(x4[j:j + SLAB, :], (SLAB, LANE_TILE))
            for h in range(1, HEADS_PER_GROUP):
                x = jnp.where(head_of_lane == h, x4[j + h * SLAB:j + (h + 1) * SLAB, :], x)
            slabs.append(x)
        return jnp.concatenate(slabs, axis=0)

    l_dense = unstack(l)
    o = unstack(o4) * (1.0 / l_dense)
    lse = unstack(m) * LN2 + jnp.log(l_dense)
    return o, lse


def _slab_major(bias):
    lead, (rows, nk) = bias.shape[:-2], bias.shape[-2:]
    b = bias.reshape(lead + (HEADS_PER_GROUP, rows // (HEADS_PER_GROUP * SLAB), SLAB, nk))
    return jnp.swapaxes(b, -4, -3).reshape(bias.shape)


def _toeplitz(f, nq, nk):
    p = nq + nk - 1
    assert f.shape[-1] == p and nq >= 2
    u = jnp.roll(f, -(nq - 1), axis=-1)
    flat = jnp.tile(u, nq)[..., :nq * (p - 1)]
    return flat.reshape(f.shape[:-1] + (nq, p - 1))[..., :nk]


def _attn_a_kernel(q_ref, k_ref, v_ref, bias_ref, o_ref, lse_ref, *, n, kb):
    i = pl.program_id(2)
    n_res, chunk, _ = q_ref.shape
    blocks_per_chunk = chunk // QB
    last_block = n // QB - 1

    def one_block(item):
        res, j = (0, item) if n_res == 1 else (item // blocks_per_chunk, item % blocks_per_chunk)
        blk = i * blocks_per_chunk + j
        start = pl.multiple_of(jnp.clip(blk * QB - HALO_A, 0, n - kb), HALO_A)
        var = jnp.where(blk == 0, 0, jnp.where(blk == last_block, 2, 1))
        rows = pl.ds(pl.multiple_of(j * QB, QB), QB)
        o, lse = _attn_core(q_ref[res, rows, :], k_ref[res, pl.ds(start, kb), :], v_ref[res, pl.ds(start, kb), :],
                            bias_ref[var])
        o_ref[res, rows, :] = o.astype(o_ref.dtype)
        lse_ref[res, rows, :] = lse

    def body(t, carry):
        for u in range(INTERLEAVE):
            one_block(t * INTERLEAVE + u)
        return carry

    lax.fori_loop(0, n_res * blocks_per_chunk // INTERLEAVE, body, 0)


def _dilated_group(q, k, v, bias, d):
    b_, d_, n, _ = q.shape
    assert d_ == d
    kb = bias.shape[-1]
    chunk = min(n, MAX_CHUNK_A)
    n_res = min(d, MAX_CHUNK_A // chunk)
    assert n % QB == 0 and n % chunk == 0 and kb == min(2 * QB, n) and (n == QB or n >= 2 * QB)
    assert d % n_res == 0 and (n_res * chunk // QB) % INTERLEAVE == 0
    qspec = pl.BlockSpec((None, n_res, chunk, LANE_TILE), lambda b, r, i: (b, r, i, 0))
    kvspec = pl.BlockSpec((None, n_res, n, LANE_TILE), lambda b, r, i: (b, r, 0, 0))
    return pl.pallas_call(
        functools.partial(_attn_a_kernel, n=n, kb=kb),
        grid=(b_, d // n_res, n // chunk),
        in_specs=[qspec, kvspec, kvspec, pl.BlockSpec(bias.shape, lambda b, r, i: (0, 0, 0))],
        out_specs=[qspec, qspec],
        out_shape=[jax.ShapeDtypeStruct(q.shape, BF16), jax.ShapeDtypeStruct(q.shape, F32)],
        compiler_params=pltpu.CompilerParams(vmem_limit_bytes=VMEM_LIMIT),
        name=f"dilated_attention_d{d}",
    )(q, k, v, bias)


def _t5_bucket(rel):
    nb = N_BUCKETS // 2
    max_exact = nb // 2
    ret = (rel > 0).astype(np.int32) * nb
    n = np.abs(rel)
    large = max_exact + (np.log(np.maximum(n, 1) / max_exact) / np.log(MAX_DISTANCE / max_exact) * (nb - max_exact)).astype(np.int32)
    large = np.minimum(large, nb - 1)
    return (ret + np.where(n < max_exact, n, large)).astype(np.int32)


def _bias_a(table, g, w, d, n):
    half = w // (2 * d)
    assert half == HALO_A
    kb = min(2 * QB, n)
    sl = slice(g * HEADS_PER_GROUP, (g + 1) * HEADS_PER_GROUP)
    offs = np.array((0, -HALO_A, QB - kb))
    rel = np.arange(QB + kb - 1)[None, :] - (QB - 1) + offs[:, None]
    f = jnp.transpose(table[_t5_bucket(d * rel)][:, :, sl], (0, 2, 1)).astype(F32) * LOG2E
    f = jnp.where((np.abs(rel) <= half)[:, None, :], f, NEG)
    return _slab_major(_toeplitz(f, QB, kb).reshape(3, HEADS_PER_GROUP * QB, kb))


def _attn_b_kernel(q_ref, k_ref, v_ref, bias_ref, o_ref, *, rows):
    i = pl.program_id(2)

    def one_row(a):
        r = i * ROWS_PER_STEP_B + a
        band = jnp.clip(r - KH_MAX // 2, 0, rows - KH_MAX)
        start = pl.multiple_of(band * GRID_W, GRID_W)
        qrows = pl.ds(pl.multiple_of(a * GRID_W, GRID_W), GRID_W)
        o, _ = _attn_core(q_ref[qrows, :], k_ref[pl.ds(start, KB_B), :], v_ref[pl.ds(start, KB_B), :],
                          bias_ref[r - band])
        o_ref[qrows, :] = o.astype(o_ref.dtype)

    def body(t, carry):
        for u in range(INTERLEAVE_B):
            one_row(t * INTERLEAVE_B + u)
        return carry

    lax.fori_loop(0, ROWS_PER_STEP_B // INTERLEAVE_B, body, 0)


def _neighborhood(qb, kb, vb, bias, layer):
    b_, l_, _ = qb.shape
    rows = l_ // GRID_W
    assert rows % ROWS_PER_STEP_B == 0 and rows >= KH_MAX
    halves = WIDTH_B // LANE_TILE
    qspec = pl.BlockSpec((None, ROWS_PER_STEP_B * GRID_W, LANE_TILE), lambda hf, b, i: (b, i, hf))
    kvspec = pl.BlockSpec((None, l_, LANE_TILE), lambda hf, b, i: (b, 0, hf))
    bspec = pl.BlockSpec((None, KH_MAX, None, HEADS_PER_GROUP * GRID_W, KB_B), lambda hf, b, i: (layer, 0, hf, 0, 0))
    return pl.pallas_call(
        functools.partial(_attn_b_kernel, rows=rows),
        grid=(halves, b_, rows // ROWS_PER_STEP_B),
        in_specs=[qspec, kvspec, kvspec, bspec],
        out_specs=qspec,
        out_shape=jax.ShapeDtypeStruct((b_, l_, WIDTH_B), BF16),
        compiler_params=pltpu.CompilerParams(vmem_limit_bytes=VMEM_LIMIT),
        name="neighborhood_attention",
    )(qb, kb, vb, bias)


def _bias_b(rpb):
    cols = np.arange(GRID_W)
    c_start = np.clip(cols - KW // 2, 0, GRID_W - KW)
    col_ok = (cols[None, :] >= c_start[:, None]) & (cols[None, :] < c_start[:, None] + KW)
    assert np.all(np.abs(cols[None, :] - cols[:, None])[col_ok] <= KW - 1)
    pad = GRID_W - KW
    toep = _toeplitz(jnp.pad(rpb.astype(F32) * LOG2E, ((0, 0), (0, 0), (0, 0), (pad, pad))), GRID_W, GRID_W)
    halves = WIDTH_B // LANE_TILE
    toep = jnp.where(col_ok, toep, NEG).reshape(DEPTH, halves, HEADS_PER_GROUP, 2 * KH_MAX - 1,
                                                GRID_W // SLAB, SLAB, GRID_W)
    toep = jnp.transpose(toep, (0, 1, 4, 2, 5, 3, 6))
    shape = (DEPTH, halves, HEADS_PER_GROUP * GRID_W, KB_B)
    return jnp.stack([toep[..., KH_MAX - 1 - v:2 * KH_MAX - 1 - v, :].reshape(shape) for v in range(KH_MAX)], axis=1)


def _post_kernel(x_ref, mod_ref, n1_ref, n2_ref, oa0_ref, oa1_ref, oa2_ref, l0_ref, l1_ref, l2_ref, ob_ref,
                 wg_ref, bg_ref, wua_ref, wub_ref, wo_ref, w1_ref, w2_ref, out_ref, *scr):
    t = x_ref.shape[0]
    x = x_ref[...]
    sh1, sc1, g1 = mod_ref[0:1, :], mod_ref[1:2, :], mod_ref[2:3, :]
    sh2, sc2, g2 = mod_ref[3:4, :], mod_ref[4:5, :], mod_ref[5:6, :]
    h = (_rms(x, n1_ref[...]) * (1.0 + sc1) + sh1).astype(BF16)
    gate = jax.nn.sigmoid(jnp.dot(h, wg_ref[...], preferred_element_type=F32) + bg_ref[...])

    def token_order(ref, scr_refs):
        d = ref.shape[0]
        if d == 1:
            return ref[0].astype(F32)
        for r in range(d):
            for s_ref, lanes in zip(scr_refs, LANE_HALVES):
                s_ref[pl.ds(r, t // d, stride=d), :] = ref[r, :, lanes].astype(F32)
        return jnp.concatenate([s_ref[...] for s_ref in scr_refs], axis=-1)

    oa0, oa1, oa2 = token_order(oa0_ref, None), token_order(oa1_ref, scr[0:2]), token_order(oa2_ref, scr[2:4])
    l0, l1, l2 = token_order(l0_ref, None), token_order(l1_ref, scr[4:6]), token_order(l2_ref, scr[6:8])
    mx = jnp.maximum(jnp.maximum(l0, l1), l2)
    e0, e1, e2 = jnp.exp(l0 - mx), jnp.exp(l1 - mx), jnp.exp(l2 - mx)
    inv = 1.0 / (e0 + e1 + e2)
    oa = (e0 * inv) * oa0 + (e1 * inv) * oa1 + (e2 * inv) * oa2

    ua = jnp.dot(oa.astype(BF16), wua_ref[...], preferred_element_type=F32)
    ub = jnp.dot(ob_ref[...], wub_ref[...], preferred_element_type=F32)
    mixed = gate[:, :D_MODEL] * ua + gate[:, D_MODEL:] * ub
    x1 = x + g1 * jnp.dot(mixed.astype(BF16), wo_ref[...], preferred_element_type=F32)

    h2 = (_rms(x1, n2_ref[...]) * (1.0 + sc2) + sh2).astype(BF16)
    acc = jnp.zeros_like(x1)
    for c in range(D_FF // FF_CHUNK):
        f = jnp.dot(h2, w1_ref[:, c * FF_CHUNK:(c + 1) * FF_CHUNK], preferred_element_type=F32)
        f = jnp.square(jnp.maximum(f, 0.0)).astype(BF16)
        acc = acc + jnp.dot(f, w2_ref[c * FF_CHUNK:(c + 1) * FF_CHUNK, :], preferred_element_type=F32)
    out_ref[...] = x1 + g2 * acc


def _post_attention(x, mod, n1, n2, oas, lses, ob, wg, bg, wua, wub, wo, w1, w2, layer):
    b_, l_, _ = x.shape
    t = TOK_TILE_POST
    per_b = l_ // t
    tok = lambda w: pl.BlockSpec((None, t, w), lambda i: (i // per_b, i % per_b, 0))
    dil_specs = []
    for _, d in DIL_GROUPS:
        assert t % (BF16_SUBLANES * d) == 0
        dil_specs.append(pl.BlockSpec((None, d, t // d, LANE_TILE), lambda i: (i // per_b, 0, i % per_b, 0)))
    return pl.pallas_call(
        _post_kernel,
        grid=(b_ * per_b,),
        in_specs=[
            tok(D_MODEL),
            pl.BlockSpec((None, None, N_MOD, D_MODEL), lambda i: (layer, i // per_b, 0, 0)),
            _layer_spec((1, D_MODEL), layer), _layer_spec((1, D_MODEL), layer),
            *dil_specs, *dil_specs,
            tok(WIDTH_B),
            _layer_spec((D_MODEL, 2 * D_MODEL), layer), _layer_spec((1, 2 * D_MODEL), layer),
            _layer_spec((WIDTH_A, D_MODEL), layer), _layer_spec((WIDTH_B, D_MODEL), layer),
            _layer_spec((D_MODEL, D_MODEL), layer),
            _layer_spec((D_MODEL, D_FF), layer), _layer_spec((D_FF, D_MODEL), layer),
        ],
        out_specs=tok(D_MODEL),
        out_shape=jax.ShapeDtypeStruct(x.shape, F32),
        scratch_shapes=[pltpu.VMEM((t, VREG_LANES), F32)] * 8,
        compiler_params=pltpu.CompilerParams(vmem_limit_bytes=VMEM_LIMIT),
        name="post_attention_mlp",
    )(x, mod, n1, n2, *oas, *lses, ob, wg, bg, wua, wub, wo, w1, w2)


def _head_sum_matrix():
    head = np.arange(LANE_TILE) // HEAD_DIM
    return jnp.asarray((head[:, None] == head[None, :]).astype(np.float32), dtype=BF16)


def kernel(x_prompt, x_sample, c_prompt, c_sample, norm1_g, norm2_g, w_mod, b_mod, w_in, q_norm_g, k_norm_g,
           rel_bias, rpb, w_gate, b_gate, w_up_a, w_up_b, w_o, w_ff1, w_ff2):
    n_prompt = c_prompt.shape[0]
    mod_all = _modulation(jnp.concatenate([c_prompt, c_sample], axis=0), w_mod, b_mod)
    mod_all = mod_all.reshape(DEPTH, -1, N_MOD, D_MODEL)
    ones = _head_sum_matrix()
    qg = (q_norm_g.reshape(DEPTH, 1, QKV_PART) * (HEAD_DIM ** -0.5 * LOG2E)).astype(F32)
    kg = k_norm_g.reshape(DEPTH, 1, QKV_PART).astype(F32)
    n1 = norm1_g.reshape(DEPTH, 1, D_MODEL)
    n2 = norm2_g.reshape(DEPTH, 1, D_MODEL)
    bg = b_gate.reshape(DEPTH, 1, 2 * D_MODEL)
    w_in_b, w_gate_b = w_in.astype(BF16), w_gate.astype(BF16)
    w_up_a_b, w_up_b_b, w_o_b = w_up_a.astype(BF16), w_up_b.astype(BF16), w_o.astype(BF16)
    w_ff1_b, w_ff2_b = w_ff1.astype(BF16), w_ff2.astype(BF16)

    xs = (x_prompt, x_sample)
    bias_b = _bias_b(rpb)
    bias_a = {}
    for x in xs:
        for g, (w, d) in enumerate(DIL_GROUPS):
            n = x.shape[1] // d
            if (g, min(2 * QB, n)) not in bias_a:
                bias_a[(g, min(2 * QB, n))] = _bias_a(rel_bias, g, w, d, n)

    ys = []
    for x, mod_x in ((x_prompt, mod_all[:, :n_prompt]), (x_sample, mod_all[:, n_prompt:])):
        l_ = x.shape[1]
        for l in range(DEPTH):
            q, k, v = _pre_attention(x, mod_x, n1, w_in_b, qg, kg, ones, l)
            oas, lses = [], []
            for g, (w, d) in enumerate(DIL_GROUPS):
                o, lse = _dilated_group(q[g], k[g], v[g], bias_a[(g, min(2 * QB, l_ // d))], d)
                oas.append(o)
                lses.append(lse)
            ob = _neighborhood(q[N_GROUPS], k[N_GROUPS], v[N_GROUPS], bias_b, l)
            x = _post_attention(x, mod_x, n1, n2, oas, lses, ob, w_gate_b, bg, w_up_a_b, w_up_b_b, w_o_b,
                                w_ff1_b, w_ff2_b, l)
        ys.append(x)
    return tuple(ys)
```

```python
import functools

import jax
import jax.numpy as jnp
import numpy as np
from jax import lax
from jax.experimental import pallas as pl
from jax.experimental.pallas import tpu as pltpu

D_MODEL = 1024
DEPTH = 2
HEAD_DIM = 64
DIL_GROUPS = ((128, 1), (512, 4), (2048, 16))
HEADS_PER_GROUP = 4
N_GROUPS = len(DIL_GROUPS)
N_HEADS_A = HEADS_PER_GROUP * N_GROUPS
N_HEADS_B = 8
N_HEADS = N_HEADS_A + N_HEADS_B
WIDTH_A = HEADS_PER_GROUP * HEAD_DIM
WIDTH_B = N_HEADS_B * HEAD_DIM
QKV_PART = N_HEADS * HEAD_DIM
N_BUCKETS = 32
MAX_DISTANCE = 1024
GRID_W = 64
KH_MAX = 8
KW = 16
D_FF = 4 * D_MODEL
N_MOD = 6
EPS = 1e-6
NEG = -1e30
LOG2E = 1.4426950408889634
LN2 = 0.6931471805599453

LANE_TILE = 256
QB = 128
SLAB = 16
HALO_A = 64
KB_B = KH_MAX * GRID_W
ROWS_PER_STEP_B = 32
MAX_CHUNK_A = 2048
INTERLEAVE_B = 16
INTERLEAVE = 16
TOK_TILE_PRE = 1024
TOK_TILE_POST = 512
FF_CHUNK = 1024
BF16_SUBLANES = 16
VREG_LANES = 128
LANE_HALVES = (slice(0, VREG_LANES), slice(VREG_LANES, LANE_TILE))
MOD_COL_BLOCK = 1536
VMEM_LIMIT = 56 * 1024 * 1024

F32 = jnp.float32
BF16 = jnp.bfloat16


def _rms(x, g):
    return x * lax.rsqrt(jnp.mean(x * x, axis=-1, keepdims=True) + EPS) * g


def _const_spec(shape):
    return pl.BlockSpec(shape, lambda *_: (0,) * len(shape), pipeline_mode=pl.Buffered(1))


def _layer_spec(shape, layer):
    return pl.BlockSpec((None,) + shape, lambda *_: (layer,) + (0,) * len(shape), pipeline_mode=pl.Buffered(1))


def _mod_kernel(c_ref, w_ref, b_ref, o_ref):
    c = c_ref[...]
    s = c * jax.nn.sigmoid(c)
    o_ref[...] = jnp.dot(s, w_ref[...], preferred_element_type=F32) + b_ref[...]


def _modulation(c_all, w_mod, b_mod):
    nb = c_all.shape[0]
    ncol = N_MOD * D_MODEL
    blk = MOD_COL_BLOCK
    assert ncol % blk == 0
    return pl.pallas_call(
        _mod_kernel,
        grid=(DEPTH, ncol // blk),
        in_specs=[
            pl.BlockSpec((nb, D_MODEL), lambda l, j: (0, 0)),
            pl.BlockSpec((None, D_MODEL, blk), lambda l, j: (l, 0, j)),
            pl.BlockSpec((None, 1, blk), lambda l, j: (l, 0, j)),
        ],
        out_specs=pl.BlockSpec((None, nb, blk), lambda l, j: (l, 0, j)),
        out_shape=jax.ShapeDtypeStruct((DEPTH, nb, ncol), F32),
        compiler_params=pltpu.CompilerParams(vmem_limit_bytes=VMEM_LIMIT),
        name="modulation",
    )(c_all, w_mod, b_mod.reshape(DEPTH, 1, ncol))


def _pre_kernel(x_ref, mod_ref, n1_ref, win_ref, qg_ref, kg_ref, ones_ref, *refs):
    scr_refs = refs[-2:]
    outs = refs[:-2]
    t = x_ref.shape[0]
    x = x_ref[...]
    sh1 = mod_ref[0:1, :]
    sc1 = mod_ref[1:2, :]
    h = (_rms(x, n1_ref[...]) * (1.0 + sc1) + sh1).astype(BF16)
    ones = ones_ref[...]
    n_chunk = QKV_PART // LANE_TILE
    per_part = N_GROUPS + 1

    def project(part):
        return jnp.dot(h, win_ref[:, part * QKV_PART:(part + 1) * QKV_PART], preferred_element_type=F32)

    def finish(part, g_ref, y_part):
        if g_ref is not None:
            sq = (y_part * y_part).astype(BF16)
            sq = jnp.concatenate([sq[:, c * LANE_TILE:(c + 1) * LANE_TILE] for c in range(n_chunk)], axis=0)
            ss_all = jnp.dot(sq, ones, preferred_element_type=F32)
        for c in range(n_chunk):
            y = y_part[:, c * LANE_TILE:(c + 1) * LANE_TILE]
            if g_ref is not None:
                ss = ss_all[c * t:(c + 1) * t, :]
                y = y * lax.rsqrt(ss * (1.0 / HEAD_DIM) + EPS) * g_ref[:, c * LANE_TILE:(c + 1) * LANE_TILE]
            if c < N_GROUPS:
                out = outs[part * per_part + c]
                d = DIL_GROUPS[c][1]
                if d == 1:
                    out[0] = y.astype(out.dtype)
                else:
                    for s_ref, lanes in zip(scr_refs, LANE_HALVES):
                        s_ref[...] = y[:, lanes]
                    for r in range(d):
                        for s_ref, lanes in zip(scr_refs, LANE_HALVES):
                            out[r, :, lanes] = s_ref[pl.ds(r, t // d, stride=d), :].astype(out.dtype)
            else:
                out = outs[part * per_part + N_GROUPS]
                cb = c - N_GROUPS
                out[:, cb * LANE_TILE:(cb + 1) * LANE_TILE] = y.astype(out.dtype)

    for part, g_ref in enumerate((qg_ref, kg_ref, None)):
        finish(part, g_ref, project(part))


def _pre_attention(x, mod, n1, w_in, qg, kg, ones, layer):
    b_, l_, _ = x.shape
    t = TOK_TILE_PRE
    per_b = l_ // t
    tok = lambda w: pl.BlockSpec((None, t, w), lambda i: (i // per_b, i % per_b, 0))
    out_specs, out_shape = [], []
    for _ in range(3):
        for _, d in DIL_GROUPS:
            assert t % (BF16_SUBLANES * d) == 0
            out_specs.append(pl.BlockSpec((None, d, t // d, LANE_TILE), lambda i: (i // per_b, 0, i % per_b, 0)))
            out_shape.append(jax.ShapeDtypeStruct((b_, d, l_ // d, LANE_TILE), BF16))
        out_specs.append(tok(WIDTH_B))
        out_shape.append(jax.ShapeDtypeStruct((b_, l_, WIDTH_B), BF16))
    outs = pl.pallas_call(
        _pre_kernel,
        grid=(b_ * per_b,),
        in_specs=[
            tok(D_MODEL),
            pl.BlockSpec((None, None, N_MOD, D_MODEL), lambda i: (layer, i // per_b, 0, 0)),
            _layer_spec((1, D_MODEL), layer),
            _layer_spec((D_MODEL, 3 * QKV_PART), layer),
            _layer_spec((1, QKV_PART), layer),
            _layer_spec((1, QKV_PART), layer),
            _const_spec((LANE_TILE, LANE_TILE)),
        ],
        out_specs=out_specs,
        out_shape=out_shape,
        scratch_shapes=[pltpu.VMEM((t, VREG_LANES), F32)] * 2,
        compiler_params=pltpu.CompilerParams(vmem_limit_bytes=VMEM_LIMIT),
        name="pre_attention",
    )(x, mod, n1, w_in, qg, kg, ones)
    per_part = N_GROUPS + 1
    return [outs[p * per_part:(p + 1) * per_part] for p in range(3)]


def _attn_core(q, k, v, bias):
    nq = q.shape[0]
    head_of_lane = lax.broadcasted_iota(jnp.int32, (SLAB, LANE_TILE), 1) // HEAD_DIM
    zero = jnp.zeros((SLAB, LANE_TILE), q.dtype)
    qs = jnp.concatenate([jnp.where(head_of_lane == h, q[j:j + SLAB, :], zero)
                          for j in range(0, nq, SLAB) for h in range(HEADS_PER_GROUP)], axis=0)
    s = lax.dot_general(qs, k, (((1,), (1,)), ((), ())), preferred_element_type=F32) + bias
    m = jnp.max(s, axis=-1, keepdims=True)
    p = jnp.exp2(s - m)
    l = jnp.sum(p, axis=-1, keepdims=True)
    o4 = jnp.dot(p.astype(BF16), v, preferred_element_type=F32)

    def unstack(x4):
        slabs = []
        for j in range(0, HEADS_PER_GROUP * nq, HEADS_PER_GROUP * SLAB):
            x = jnp.broadcast_to(x4[j:j + SLAB, :], (SLAB, LANE_TILE))
            for h in range(1, HEADS_PER_GROUP):
                x = jnp.where(head_of_lane == h, x4[j + h * SLAB:j + (h + 1) * SLAB, :], x)
            slabs.append(x)
        return jnp.concatenate(slabs, axis=0)

    l_dense = unstack(l)
    o = unstack(o4) * (1.0 / l_dense)
    lse = unstack(m) * LN2 + jnp.log(l_dense)
    return o, lse


def _slab_major(bias):
    lead, (rows, nk) = bias.shape[:-2], bias.shape[-2:]
    b = bias.reshape(lead + (HEADS_PER_GROUP, rows // (HEADS_PER_GROUP * SLAB), SLAB, nk))
    return jnp.swapaxes(b, -4, -3).reshape(bias.shape)


def _toeplitz(f, nq, nk):
    p = nq + nk - 1
    assert f.shape[-1] == p and nq >= 2
    u = jnp.roll(f, -(nq - 1), axis=-1)
    flat = jnp.tile(u, nq)[..., :nq * (p - 1)]
    return flat.reshape(f.shape[:-1] + (nq, p - 1))[..., :nk]


def _attn_a_kernel(q_ref, k_ref, v_ref, bias_ref, o_ref, lse_ref, *, n, kb):
    i = pl.program_id(2)
    n_res, chunk, _ = q_ref.shape
    blocks_per_chunk = chunk // QB
    last_block = n // QB - 1

    def one_block(item):
        res, j = (0, item) if n_res == 1 else (item // blocks_per_chunk, item % blocks_per_chunk)
        blk = i * blocks_per_chunk + j
        start = pl.multiple_of(jnp.clip(blk * QB - HALO_A, 0, n - kb), HALO_A)
        var = jnp.where(blk == 0, 0, jnp.where(blk == last_block, 2, 1))
        rows = pl.ds(pl.multiple_of(j * QB, QB), QB)
        o, lse = _attn_core(q_ref[res, rows, :], k_ref[res, pl.ds(start, kb), :], v_ref[res, pl.ds(start, kb), :],
                            bias_ref[var])
        o_ref[res, rows, :] = o.astype(o_ref.dtype)
        lse_ref[res, rows, :] = lse

    def body(t, carry):
        for u in range(INTERLEAVE):
            one_block(t * INTERLEAVE + u)
        return carry

    lax.fori_loop(0, n_res * blocks_per_chunk // INTERLEAVE, body, 0)


def _dilated_group(q, k, v, bias, d):
    b_, d_, n, _ = q.shape
    assert d_ == d
    kb = bias.shape[-1]
    chunk = min(n, MAX_CHUNK_A)
    n_res = min(d, MAX_CHUNK_A // chunk)
    assert n % QB == 0 and n % chunk == 0 and kb == min(2 * QB, n) and (n == QB or n >= 2 * QB)
    assert d % n_res == 0 and (n_res * chunk // QB) % INTERLEAVE == 0
    qspec = pl.BlockSpec((None, n_res, chunk, LANE_TILE), lambda b, r, i: (b, r, i, 0))
    kvspec = pl.BlockSpec((None, n_res, n, LANE_TILE), lambda b, r, i: (b, r, 0, 0))
    return pl.pallas_call(
        functools.partial(_attn_a_kernel, n=n, kb=kb),
        grid=(b_, d // n_res, n // chunk),
        in_specs=[qspec, kvspec, kvspec, pl.BlockSpec(bias.shape, lambda b, r, i: (0, 0, 0))],
        out_specs=[qspec, qspec],
        out_shape=[jax.ShapeDtypeStruct(q.shape, BF16), jax.ShapeDtypeStruct(q.shape, F32)],
        compiler_params=pltpu.CompilerParams(vmem_limit_bytes=VMEM_LIMIT),
        name=f"dilated_attention_d{d}",
    )(q, k, v, bias)


def _t5_bucket(rel):
    nb = N_BUCKETS // 2
    max_exact = nb // 2
    ret = (rel > 0).astype(np.int32) * nb
    n = np.abs(rel)
    large = max_exact + (np.log(np.maximum(n, 1) / max_exact) / np.log(MAX_DISTANCE / max_exact) * (nb - max_exact)).astype(np.int32)
    large = np.minimum(large, nb - 1)
    return (ret + np.where(n < max_exact, n, large)).astype(np.int32)


def _bias_a(table, g, w, d, n):
    half = w // (2 * d)
    assert half == HALO_A
    kb = min(2 * QB, n)
    sl = slice(g * HEADS_PER_GROUP, (g + 1) * HEADS_PER_GROUP)
    offs = np.array((0, -HALO_A, QB - kb))
    rel = np.arange(QB + kb - 1)[None, :] - (QB - 1) + offs[:, None]
    f = jnp.transpose(table[_t5_bucket(d * rel)][:, :, sl], (0, 2, 1)).astype(F32) * LOG2E
    f = jnp.where((np.abs(rel) <= half)[:, None, :], f, NEG)
    return _slab_major(_toeplitz(f, QB, kb).reshape(3, HEADS_PER_GROUP * QB, kb))


def _attn_b_kernel(q_ref, k_ref, v_ref, bias_ref, o_ref, *, rows):
    i = pl.program_id(2)

    def one_row(a):
        r = i * ROWS_PER_STEP_B + a
        band = jnp.clip(r - KH_MAX // 2, 0, rows - KH_MAX)
        start = pl.multiple_of(band * GRID_W, GRID_W)
        qrows = pl.ds(pl.multiple_of(a * GRID_W, GRID_W), GRID_W)
        o, _ = _attn_core(q_ref[qrows, :], k_ref[pl.ds(start, KB_B), :], v_ref[pl.ds(start, KB_B), :],
                          bias_ref[r - band])
        o_ref[qrows, :] = o.astype(o_ref.dtype)

    def body(t, carry):
        for u in range(INTERLEAVE_B):
            one_row(t * INTERLEAVE_B + u)
        return carry

    lax.fori_loop(0, ROWS_PER_STEP_B // INTERLEAVE_B, body, 0)


def _neighborhood(qb, kb, vb, bias, layer):
    b_, l_, _ = qb.shape
    rows = l_ // GRID_W
    assert rows % ROWS_PER_STEP_B == 0 and rows >= KH_MAX
    halves = WIDTH_B // LANE_TILE
    qspec = pl.BlockSpec((None, ROWS_PER_STEP_B * GRID_W, LANE_TILE), lambda hf, b, i: (b, i, hf))
    kvspec = pl.BlockSpec((None, l_, LANE_TILE), lambda hf, b, i: (b, 0, hf))
    bspec = pl.BlockSpec((None, KH_MAX, None, HEADS_PER_GROUP * GRID_W, KB_B), lambda hf, b, i: (layer, 0, hf, 0, 0))
    return pl.pallas_call(
        functools.partial(_attn_b_kernel, rows=rows),
        grid=(halves, b_, rows // ROWS_PER_STEP_B),
        in_specs=[qspec, kvspec, kvspec, bspec],
        out_specs=qspec,
        out_shape=jax.ShapeDtypeStruct((b_, l_, WIDTH_B), BF16),
        compiler_params=pltpu.CompilerParams(vmem_limit_bytes=VMEM_LIMIT),
        name="neighborhood_attention",
    )(qb, kb, vb, bias)


def _bias_b(rpb):
    cols = np.arange(GRID_W)
    c_start = np.clip(cols - KW // 2, 0, GRID_W - KW)
    col_ok = (cols[None, :] >= c_start[:, None]) & (cols[None, :] < c_start[:, None] + KW)
    assert np.all(np.abs(cols[None, :] - cols[:, None])[col_ok] <= KW - 1)
    pad = GRID_W - KW
    toep = _toeplitz(jnp.pad(rpb.astype(F32) * LOG2E, ((0, 0), (0, 0), (0, 0), (pad, pad))), GRID_W, GRID_W)
    halves = WIDTH_B // LANE_TILE
    toep = jnp.where(col_ok, toep, NEG).reshape(DEPTH, halves, HEADS_PER_GROUP, 2 * KH_MAX - 1,
                                                GRID_W // SLAB, SLAB, GRID_W)
    toep = jnp.transpose(toep, (0, 1, 4, 2, 5, 3, 6))
    shape = (DEPTH, halves, HEADS_PER_GROUP * GRID_W, KB_B)
    return jnp.stack([toep[..., KH_MAX - 1 - v:2 * KH_MAX - 1 - v, :].reshape(shape) for v in range(KH_MAX)], axis=1)


def _post_kernel(x_ref, mod_ref, n1_ref, n2_ref, oa0_ref, oa1_ref, oa2_ref, l0_ref, l1_ref, l2_ref, ob_ref,
                 wg_ref, bg_ref, wua_ref, wub_ref, wo_ref, w1_ref, w2_ref, out_ref, *scr):
    t = x_ref.shape[0]
    x = x_ref[...]
    sh1, sc1, g1 = mod_ref[0:1, :], mod_ref[1:2, :], mod_ref[2:3, :]
    sh2, sc2, g2 = mod_ref[3:4, :], mod_ref[4:5, :], mod_ref[5:6, :]
    h = (_rms(x, n1_ref[...]) * (1.0 + sc1) + sh1).astype(BF16)
    gate = jax.nn.sigmoid(jnp.dot(h, wg_ref[...], preferred_element_type=F32) + bg_ref[...])

    def token_order(ref, scr_refs):
        d = ref.shape[0]
        if d == 1:
            return ref[0].astype(F32)
        for r in range(d):
            for s_ref, lanes in zip(scr_refs, LANE_HALVES):
                s_ref[pl.ds(r, t // d, stride=d), :] = ref[r, :, lanes].astype(F32)
        return jnp.concatenate([s_ref[...] for s_ref in scr_refs], axis=-1)

    oa0, oa1, oa2 = token_order(oa0_ref, None), token_order(oa1_ref, scr[0:2]), token_order(oa2_ref, scr[2:4])
    l0, l1, l2 = token_order(l0_ref, None), token_order(l1_ref, scr[4:6]), token_order(l2_ref, scr[6:8])
    mx = jnp.maximum(jnp.maximum(l0, l1), l2)
    e0, e1, e2 = jnp.exp(l0 - mx), jnp.exp(l1 - mx), jnp.exp(l2 - mx)
    inv = 1.0 / (e0 + e1 + e2)
    oa = (e0 * inv) * oa0 + (e1 * inv) * oa1 + (e2 * inv) * oa2

    ua = jnp.dot(oa.astype(BF16), wua_ref[...], preferred_element_type=F32)
    ub = jnp.dot(ob_ref[...], wub_ref[...], preferred_element_type=F32)
    mixed = gate[:, :D_MODEL] * ua + gate[:, D_MODEL:] * ub
    x1 = x + g1 * jnp.dot(mixed.astype(BF16), wo_ref[...], preferred_element_type=F32)

    h2 = (_rms(x1, n2_ref[...]) * (1.0 + sc2) + sh2).astype(BF16)
    acc = jnp.zeros_like(x1)
    for c in range(D_FF // FF_CHUNK):
        f = jnp.dot(h2, w1_ref[:, c * FF_CHUNK:(c + 1) * FF_CHUNK], preferred_element_type=F32)
        f = jnp.square(jnp.maximum(f, 0.0)).astype(BF16)
        acc = acc + jnp.dot(f, w2_ref[c * FF_CHUNK:(c + 1) * FF_CHUNK, :], preferred_element_type=F32)
    out_ref[...] = x1 + g2 * acc


def _post_attention(x, mod, n1, n2, oas, lses, ob, wg, bg, wua, wub, wo, w1, w2, layer):
    b_, l_, _ = x.shape
    t = TOK_TILE_POST
    per_b = l_ // t
    tok = lambda w: pl.BlockSpec((None, t, w), lambda i: (i // per_b, i % per_b, 0))
    dil_specs = []
    for _, d in DIL_GROUPS:
        assert t % (BF16_SUBLANES * d) == 0
        dil_specs.append(pl.BlockSpec((None, d, t // d, LANE_TILE), lambda i: (i // per_b, 0, i % per_b, 0)))
    return pl.pallas_call(
        _post_kernel,
        grid=(b_ * per_b,),
        in_specs=[
            tok(D_MODEL),
            pl.BlockSpec((None, None, N_MOD, D_MODEL), lambda i: (layer, i // per_b, 0, 0)),
            _layer_spec((1, D_MODEL), layer), _layer_spec((1, D_MODEL), layer),
            *dil_specs, *dil_specs,
            tok(WIDTH_B),
            _layer_spec((D_MODEL, 2 * D_MODEL), layer), _layer_spec((1, 2 * D_MODEL), layer),
            _layer_spec((WIDTH_A, D_MODEL), layer), _layer_spec((WIDTH_B, D_MODEL), layer),
            _layer_spec((D_MODEL, D_MODEL), layer),
            _layer_spec((D_MODEL, D_FF), layer), _layer_spec((D_FF, D_MODEL), layer),
        ],
        out_specs=tok(D_MODEL),
        out_shape=jax.ShapeDtypeStruct(x.shape, F32),
        scratch_shapes=[pltpu.VMEM((t, VREG_LANES), F32)] * 8,
        compiler_params=pltpu.CompilerParams(vmem_limit_bytes=VMEM_LIMIT),
        name="post_attention_mlp",
    )(x, mod, n1, n2, *oas, *lses, ob, wg, bg, wua, wub, wo, w1, w2)


def _head_sum_matrix():
    head = np.arange(LANE_TILE) // HEAD_DIM
    return jnp.asarray((head[:, None] == head[None, :]).astype(np.float32), dtype=BF16)


def kernel(x_prompt, x_sample, c_prompt, c_sample, norm1_g, norm2_g, w_mod, b_mod, w_in, q_norm_g, k_norm_g,
           rel_bias, rpb, w_gate, b_gate, w_up_a, w_up_b, w_o, w_ff1, w_ff2):
    n_prompt = c_prompt.shape[0]
    mod_all = _modulation(jnp.concatenate([c_prompt, c_sample], axis=0), w_mod, b_mod)
    mod_all = mod_all.reshape(DEPTH, -1, N_MOD, D_MODEL)
    ones = _head_sum_matrix()
    qg = (q_norm_g.reshape(DEPTH, 1, QKV_PART) * (HEAD_DIM ** -0.5 * LOG2E)).astype(F32)
    kg = k_norm_g.reshape(DEPTH, 1, QKV_PART).astype(F32)
    n1 = norm1_g.reshape(DEPTH, 1, D_MODEL)
    n2 = norm2_g.reshape(DEPTH, 1, D_MODEL)
    bg = b_gate.reshape(DEPTH, 1, 2 * D_MODEL)
    w_in_b, w_gate_b = w_in.astype(BF16), w_gate.astype(BF16)
    w_up_a_b, w_up_b_b, w_o_b = w_up_a.astype(BF16), w_up_b.astype(BF16), w_o.astype(BF16)
    w_ff1_b, w_ff2_b = w_ff1.astype(BF16), w_ff2.astype(BF16)

    xs = (x_prompt, x_sample)
    bias_b = _bias_b(rpb)
    bias_a = {}
    for x in xs:
        for g, (w, d) in enumerate(DIL_GROUPS):
            n = x.shape[1] // d
            if (g, min(2 * QB, n)) not in bias_a:
                bias_a[(g, min(2 * QB, n))] = _bias_a(rel_bias, g, w, d, n)

    ys = []
    for x, mod_x in ((x_prompt, mod_all[:, :n_prompt]), (x_sample, mod_all[:, n_prompt:])):
        l_ = x.shape[1]
        for l in range(DEPTH):
            q, k, v = _pre_attention(x, mod_x, n1, w_in_b, qg, kg, ones, l)
            oas, lses = [], []
            for g, (w, d) in enumerate(DIL_GROUPS):
                o, lse = _dilated_group(q[g], k[g], v[g], bias_a[(g, min(2 * QB, l_ // d))], d)
                oas.append(o)
                lses.append(lse)
            ob = _neighborhood(q[N_GROUPS], k[N_GROUPS], v[N_GROUPS], bias_b, l)
            x = _post_attention(x, mod_x, n1, n2, oas, lses, ob, w_gate_b, bg, w_up_a_b, w_up_b_b, w_o_b,
                                w_ff1_b, w_ff2_b, l)
        ys.append(x)
    return tuple(ys)
```

```python
import functools

import jax
import jax.numpy as jnp
import numpy as np
from jax import lax
from jax.experimental import pallas as pl
from jax.experimental.pallas import tpu as pltpu

D_MODEL = 1024
DEPTH = 2
HEAD_DIM = 64
DIL_GROUPS = ((128, 1), (512, 4), (2048, 16))
HEADS_PER_GROUP = 4
N_GROUPS = len(DIL_GROUPS)
N_HEADS_A = HEADS_PER_GROUP * N_GROUPS
N_HEADS_B = 8
N_HEADS = N_HEADS_A + N_HEADS_B
WIDTH_A = HEADS_PER_GROUP * HEAD_DIM
WIDTH_B = N_HEADS_B * HEAD_DIM
QKV_PART = N_HEADS * HEAD_DIM
N_BUCKETS = 32
MAX_DISTANCE = 1024
GRID_W = 64
KH_MAX = 8
KW = 16
D_FF = 4 * D_MODEL
N_MOD = 6
EPS = 1e-6
NEG = -1e30
LOG2E = 1.4426950408889634
LN2 = 0.6931471805599453

LANE_TILE = 256
QB = 128
SLAB = 16
HALO_A = 64
KB_B = KH_MAX * GRID_W
ROWS_PER_STEP_B = 32
MAX_CHUNK_A = 4096
INTERLEAVE_B = 16
INTERLEAVE = 16
TOK_TILE_PRE = 1024
TOK_TILE_POST = 512
FF_CHUNK = 1024
BF16_SUBLANES = 16
VREG_LANES = 128
LANE_HALVES = (slice(0, VREG_LANES), slice(VREG_LANES, LANE_TILE))
MOD_COL_BLOCK = 1536
VMEM_LIMIT = 56 * 1024 * 1024

F32 = jnp.float32
BF16 = jnp.bfloat16


def _rms(x, g):
    return x * lax.rsqrt(jnp.mean(x * x, axis=-1, keepdims=True) + EPS) * g


def _const_spec(shape):
    return pl.BlockSpec(shape, lambda *_: (0,) * len(shape), pipeline_mode=pl.Buffered(1))


def _layer_spec(shape, layer):
    return pl.BlockSpec((None,) + shape, lambda *_: (layer,) + (0,) * len(shape), pipeline_mode=pl.Buffered(1))


def _mod_kernel(c_ref, w_ref, b_ref, o_ref):
    c = c_ref[...]
    s = c * jax.nn.sigmoid(c)
    o_ref[...] = jnp.dot(s, w_ref[...], preferred_element_type=F32) + b_ref[...]


def _modulation(c_all, w_mod, b_mod):
    nb = c_all.shape[0]
    ncol = N_MOD * D_MODEL
    blk = MOD_COL_BLOCK
    assert ncol % blk == 0
    return pl.pallas_call(
        _mod_kernel,
        grid=(DEPTH, ncol // blk),
        in_specs=[
            pl.BlockSpec((nb, D_MODEL), lambda l, j: (0, 0)),
            pl.BlockSpec((None, D_MODEL, blk), lambda l, j: (l, 0, j)),
            pl.BlockSpec((None, 1, blk), lambda l, j: (l, 0, j)),
        ],
        out_specs=pl.BlockSpec((None, nb, blk), lambda l, j: (l, 0, j)),
        out_shape=jax.ShapeDtypeStruct((DEPTH, nb, ncol), F32),
        compiler_params=pltpu.CompilerParams(vmem_limit_bytes=VMEM_LIMIT),
        name="modulation",
    )(c_all, w_mod, b_mod.reshape(DEPTH, 1, ncol))


def _pre_kernel(x_ref, mod_ref, n1_ref, win_ref, qg_ref, kg_ref, ones_ref, *refs):
    scr_refs = refs[-2:]
    outs = refs[:-2]
    t = x_ref.shape[0]
    x = x_ref[...]
    sh1 = mod_ref[0:1, :]
    sc1 = mod_ref[1:2, :]
    h = (_rms(x, n1_ref[...]) * (1.0 + sc1) + sh1).astype(BF16)
    ones = ones_ref[...]
    n_chunk = QKV_PART // LANE_TILE
    per_part = N_GROUPS + 1

    def project(part):
        return jnp.dot(h, win_ref[:, part * QKV_PART:(part + 1) * QKV_PART], preferred_element_type=F32)

    def finish(part, g_ref, y_part):
        if g_ref is not None:
            sq = (y_part * y_part).astype(BF16)
            sq = jnp.concatenate([sq[:, c * LANE_TILE:(c + 1) * LANE_TILE] for c in range(n_chunk)], axis=0)
            ss_all = jnp.dot(sq, ones, preferred_element_type=F32)
        for c in range(n_chunk):
            y = y_part[:, c * LANE_TILE:(c + 1) * LANE_TILE]
            if g_ref is not None:
                ss = ss_all[c * t:(c + 1) * t, :]
                y = y * lax.rsqrt(ss * (1.0 / HEAD_DIM) + EPS) * g_ref[:, c * LANE_TILE:(c + 1) * LANE_TILE]
            if c < N_GROUPS:
                out = outs[part * per_part + c]
                d = DIL_GROUPS[c][1]
                if d == 1:
                    out[0] = y.astype(out.dtype)
                else:
                    for s_ref, lanes in zip(scr_refs, LANE_HALVES):
                        s_ref[...] = y[:, lanes]
                    for r in range(d):
                        for s_ref, lanes in zip(scr_refs, LANE_HALVES):
                            out[r, :, lanes] = s_ref[pl.ds(r, t // d, stride=d), :].astype(out.dtype)
            else:
                out = outs[part * per_part + N_GROUPS]
                cb = c - N_GROUPS
                out[:, cb * LANE_TILE:(cb + 1) * LANE_TILE] = y.astype(out.dtype)

    for part, g_ref in enumerate((qg_ref, kg_ref, None)):
        finish(part, g_ref, project(part))


def _pre_attention(x, mod, n1, w_in, qg, kg, ones, layer):
    b_, l_, _ = x.shape
    t = TOK_TILE_PRE
    per_b = l_ // t
    tok = lambda w: pl.BlockSpec((None, t, w), lambda i: (i // per_b, i % per_b, 0))
    out_specs, out_shape = [], []
    for _ in range(3):
        for _, d in DIL_GROUPS:
            assert t % (BF16_SUBLANES * d) == 0
            out_specs.append(pl.BlockSpec((None, d, t // d, LANE_TILE), lambda i: (i // per_b, 0, i % per_b, 0)))
            out_shape.append(jax.ShapeDtypeStruct((b_, d, l_ // d, LANE_TILE), BF16))
        out_specs.append(tok(WIDTH_B))
        out_shape.append(jax.ShapeDtypeStruct((b_, l_, WIDTH_B), BF16))
    outs = pl.pallas_call(
        _pre_kernel,
        grid=(b_ * per_b,),
        in_specs=[
            tok(D_MODEL),
            pl.BlockSpec((None, None, N_MOD, D_MODEL), lambda i: (layer, i // per_b, 0, 0)),
            _layer_spec((1, D_MODEL), layer),
            _layer_spec((D_MODEL, 3 * QKV_PART), layer),
            _layer_spec((1, QKV_PART), layer),
            _layer_spec((1, QKV_PART), layer),
            _const_spec((LANE_TILE, LANE_TILE)),
        ],
        out_specs=out_specs,
        out_shape=out_shape,
        scratch_shapes=[pltpu.VMEM((t, VREG_LANES), F32)] * 2,
        compiler_params=pltpu.CompilerParams(vmem_limit_bytes=VMEM_LIMIT),
        name="pre_attention",
    )(x, mod, n1, w_in, qg, kg, ones)
    per_part = N_GROUPS + 1
    return [outs[p * per_part:(p + 1) * per_part] for p in range(3)]


def _attn_core(q, k, v, bias):
    nq = q.shape[0]
    head_of_lane = lax.broadcasted_iota(jnp.int32, (SLAB, LANE_TILE), 1) // HEAD_DIM
    zero = jnp.zeros((SLAB, LANE_TILE), q.dtype)
    qs = jnp.concatenate([jnp.where(head_of_lane == h, q[j:j + SLAB, :], zero)
                          for j in range(0, nq, SLAB) for h in range(HEADS_PER_GROUP)], axis=0)
    s = lax.dot_general(qs, k, (((1,), (1,)), ((), ())), preferred_element_type=F32) + bias
    m = jnp.max(s, axis=-1, keepdims=True)
    p = jnp.exp2(s - m)
    l = jnp.sum(p, axis=-1, keepdims=True)
    o4 = jnp.dot(p.astype(BF16), v, preferred_element_type=F32)

    def unstack(x4):
        slabs = []
        for j in range(0, HEADS_PER_GROUP * nq, HEADS_PER_GROUP * SLAB):
            x = jnp.broadcast_to(x4[j:j + SLAB, :], (SLAB, LANE_TILE))
            for h in range(1, HEADS_PER_GROUP):
                x = jnp.where(head_of_lane == h, x4[j + h * SLAB:j + (h + 1) * SLAB, :], x)
            slabs.append(x)
        return jnp.concatenate(slabs, axis=0)

    l_dense = unstack(l)
    o = unstack(o4) * (1.0 / l_dense)
    lse = unstack(m) * LN2 + jnp.log(l_dense)
    return o, lse


def _slab_major(bias):
    lead, (rows, nk) = bias.shape[:-2], bias.shape[-2:]
    b = bias.reshape(lead + (HEADS_PER_GROUP, rows // (HEADS_PER_GROUP * SLAB), SLAB, nk))
    return jnp.swapaxes(b, -4, -3).reshape(bias.shape)


def _toeplitz(f, nq, nk):
    p = nq + nk - 1
    assert f.shape[-1] == p and nq >= 2
    u = jnp.roll(f, -(nq - 1), axis=-1)
    flat = jnp.tile(u, nq)[..., :nq * (p - 1)]
    return flat.reshape(f.shape[:-1] + (nq, p - 1))[..., :nk]


def _attn_a_kernel(q_ref, k_ref, v_ref, bias_ref, o_ref, lse_ref, *, n, kb):
    i = pl.program_id(2)
    n_res, chunk, _ = q_ref.shape
    blocks_per_chunk = chunk // QB
    last_block = n // QB - 1

    def one_block(item):
        res, j = (0, item) if n_res == 1 else (item // blocks_per_chunk, item % blocks_per_chunk)
        blk = i * blocks_per_chunk + j
        start = pl.multiple_of(jnp.clip(blk * QB - HALO_A, 0, n - kb), HALO_A)
        var = jnp.where(blk == 0, 0, jnp.where(blk == last_block, 2, 1))
        rows = pl.ds(pl.multiple_of(j * QB, QB), QB)
        o, lse = _attn_core(q_ref[res, rows, :], k_ref[res, pl.ds(start, kb), :], v_ref[res, pl.ds(start, kb), :],
                            bias_ref[var])
        o_ref[res, rows, :] = o.astype(o_ref.dtype)
        lse_ref[res, rows, :] = lse

    def body(t, carry):
        for u in range(INTERLEAVE):
            one_block(t * INTERLEAVE + u)
        return carry

    lax.fori_loop(0, n_res * blocks_per_chunk // INTERLEAVE, body, 0)


def _dilated_group(q, k, v, bias, d):
    b_, d_, n, _ = q.shape
    assert d_ == d
    kb = bias.shape[-1]
    chunk = min(n, MAX_CHUNK_A)
    n_res = min(d, MAX_CHUNK_A // chunk)
    assert n % QB == 0 and n % chunk == 0 and kb == min(2 * QB, n) and (n == QB or n >= 2 * QB)
    assert d % n_res == 0 and (n_res * chunk // QB) % INTERLEAVE == 0
    qspec = pl.BlockSpec((None, n_res, chunk, LANE_TILE), lambda b, r, i: (b, r, i, 0))
    kvspec = pl.BlockSpec((None, n_res, n, LANE_TILE), lambda b, r, i: (b, r, 0, 0))
    return pl.pallas_call(
        functools.partial(_attn_a_kernel, n=n, kb=kb),
        grid=(b_, d // n_res, n // chunk),
        in_specs=[qspec, kvspec, kvspec, pl.BlockSpec(bias.shape, lambda b, r, i: (0, 0, 0))],
        out_specs=[qspec, qspec],
        out_shape=[jax.ShapeDtypeStruct(q.shape, BF16), jax.ShapeDtypeStruct(q.shape, F32)],
        compiler_params=pltpu.CompilerParams(vmem_limit_bytes=VMEM_LIMIT),
        name=f"dilated_attention_d{d}",
    )(q, k, v, bias)


def _t5_bucket(rel):
    nb = N_BUCKETS // 2
    max_exact = nb // 2
    ret = (rel > 0).astype(np.int32) * nb
    n = np.abs(rel)
    large = max_exact + (np.log(np.maximum(n, 1) / max_exact) / np.log(MAX_DISTANCE / max_exact) * (nb - max_exact)).astype(np.int32)
    large = np.minimum(large, nb - 1)
    return (ret + np.where(n < max_exact, n, large)).astype(np.int32)


def _bias_a(table, g, w, d, n):
    half = w // (2 * d)
    assert half == HALO_A
    kb = min(2 * QB, n)
    sl = slice(g * HEADS_PER_GROUP, (g + 1) * HEADS_PER_GROUP)
    offs = np.array((0, -HALO_A, QB - kb))
    rel = np.arange(QB + kb - 1)[None, :] - (QB - 1) + offs[:, None]
    f = jnp.transpose(table[_t5_bucket(d * rel)][:, :, sl], (0, 2, 1)).astype(F32) * LOG2E
    f = jnp.where((np.abs(rel) <= half)[:, None, :], f, NEG)
    return _slab_major(_toeplitz(f, QB, kb).reshape(3, HEADS_PER_GROUP * QB, kb))


def _attn_b_kernel(q_ref, k_ref, v_ref, bias_ref, o_ref, *, rows):
    i = pl.program_id(2)

    def one_row(a):
        r = i * ROWS_PER_STEP_B + a
        band = jnp.clip(r - KH_MAX // 2, 0, rows - KH_MAX)
        start = pl.multiple_of(band * GRID_W, GRID_W)
        qrows = pl.ds(pl.multiple_of(a * GRID_W, GRID_W), GRID_W)
        o, _ = _attn_core(q_ref[qrows, :], k_ref[pl.ds(start, KB_B), :], v_ref[pl.ds(start, KB_B), :],
                          bias_ref[r - band])
        o_ref[qrows, :] = o.astype(o_ref.dtype)

    def body(t, carry):
        for u in range(INTERLEAVE_B):
            one_row(t * INTERLEAVE_B + u)
        return carry

    lax.fori_loop(0, ROWS_PER_STEP_B // INTERLEAVE_B, body, 0)


def _neighborhood(qb, kb, vb, bias, layer):
    b_, l_, _ = qb.shape
    rows = l_ // GRID_W
    assert rows % ROWS_PER_STEP_B == 0 and rows >= KH_MAX
    halves = WIDTH_B // LANE_TILE
    qspec = pl.BlockSpec((None, ROWS_PER_STEP_B * GRID_W, LANE_TILE), lambda hf, b, i: (b, i, hf))
    kvspec = pl.BlockSpec((None, l_, LANE_TILE), lambda hf, b, i: (b, 0, hf))
    bspec = pl.BlockSpec((None, KH_MAX, None, HEADS_PER_GROUP * GRID_W, KB_B), lambda hf, b, i: (layer, 0, hf, 0, 0))
    return pl.pallas_call(
        functools.partial(_attn_b_kernel, rows=rows),
        grid=(halves, b_, rows // ROWS_PER_STEP_B),
        in_specs=[qspec, kvspec, kvspec, bspec],
        out_specs=qspec,
        out_shape=jax.ShapeDtypeStruct((b_, l_, WIDTH_B), BF16),
        compiler_params=pltpu.CompilerParams(vmem_limit_bytes=VMEM_LIMIT),
        name="neighborhood_attention",
    )(qb, kb, vb, bias)


def _bias_b(rpb):
    cols = np.arange(GRID_W)
    c_start = np.clip(cols - KW // 2, 0, GRID_W - KW)
    col_ok = (cols[None, :] >= c_start[:, None]) & (cols[None, :] < c_start[:, None] + KW)
    assert np.all(np.abs(cols[None, :] - cols[:, None])[col_ok] <= KW - 1)
    pad = GRID_W - KW
    toep = _toeplitz(jnp.pad(rpb.astype(F32) * LOG2E, ((0, 0), (0, 0), (0, 0), (pad, pad))), GRID_W, GRID_W)
    halves = WIDTH_B // LANE_TILE
    toep = jnp.where(col_ok, toep, NEG).reshape(DEPTH, halves, HEADS_PER_GROUP, 2 * KH_MAX - 1,
                                                GRID_W // SLAB, SLAB, GRID_W)
    toep = jnp.transpose(toep, (0, 1, 4, 2, 5, 3, 6))
    shape = (DEPTH, halves, HEADS_PER_GROUP * GRID_W, KB_B)
    return jnp.stack([toep[..., KH_MAX - 1 - v:2 * KH_MAX - 1 - v, :].reshape(shape) for v in range(KH_MAX)], axis=1)


def _post_kernel(x_ref, mod_ref, n1_ref, n2_ref, oa0_ref, oa1_ref, oa2_ref, l0_ref, l1_ref, l2_ref, ob_ref,
                 wg_ref, bg_ref, wua_ref, wub_ref, wo_ref, w1_ref, w2_ref, out_ref, *scr):
    t = x_ref.shape[0]
    x = x_ref[...]
    sh1, sc1, g1 = mod_ref[0:1, :], mod_ref[1:2, :], mod_ref[2:3, :]
    sh2, sc2, g2 = mod_ref[3:4, :], mod_ref[4:5, :], mod_ref[5:6, :]
    h = (_rms(x, n1_ref[...]) * (1.0 + sc1) + sh1).astype(BF16)
    gate = jax.nn.sigmoid(jnp.dot(h, wg_ref[...], preferred_element_type=F32) + bg_ref[...])

    def token_order(ref, scr_refs):
        d = ref.shape[0]
        if d == 1:
            return ref[0].astype(F32)
        for r in range(d):
            for s_ref, lanes in zip(scr_refs, LANE_HALVES):
                s_ref[pl.ds(r, t // d, stride=d), :] = ref[r, :, lanes].astype(F32)
        return jnp.concatenate([s_ref[...] for s_ref in scr_refs], axis=-1)

    oa0, oa1, oa2 = token_order(oa0_ref, None), token_order(oa1_ref, scr[0:2]), token_order(oa2_ref, scr[2:4])
    l0, l1, l2 = token_order(l0_ref, None), token_order(l1_ref, scr[4:6]), token_order(l2_ref, scr[6:8])
    mx = jnp.maximum(jnp.maximum(l0, l1), l2)
    e0, e1, e2 = jnp.exp(l0 - mx), jnp.exp(l1 - mx), jnp.exp(l2 - mx)
    inv = 1.0 / (e0 + e1 + e2)
    oa = (e0 * inv) * oa0 + (e1 * inv) * oa1 + (e2 * inv) * oa2

    ua = jnp.dot(oa.astype(BF16), wua_ref[...], preferred_element_type=F32)
    ub = jnp.dot(ob_ref[...], wub_ref[...], preferred_element_type=F32)
    mixed = gate[:, :D_MODEL] * ua + gate[:, D_MODEL:] * ub
    x1 = x + g1 * jnp.dot(mixed.astype(BF16), wo_ref[...], preferred_element_type=F32)

    h2 = (_rms(x1, n2_ref[...]) * (1.0 + sc2) + sh2).astype(BF16)
    acc = jnp.zeros_like(x1)
    for c in range(D_FF // FF_CHUNK):
        f = jnp.dot(h2, w1_ref[:, c * FF_CHUNK:(c + 1) * FF_CHUNK], preferred_element_type=F32)
        f = jnp.square(jnp.maximum(f, 0.0)).astype(BF16)
        acc = acc + jnp.dot(f, w2_ref[c * FF_CHUNK:(c + 1) * FF_CHUNK, :], preferred_element_type=F32)
    out_ref[...] = x1 + g2 * acc


def _post_attention(x, mod, n1, n2, oas, lses, ob, wg, bg, wua, wub, wo, w1, w2, layer):
    b_, l_, _ = x.shape
    t = TOK_TILE_POST
    per_b = l_ // t
    tok = lambda w: pl.BlockSpec((None, t, w), lambda i: (i // per_b, i % per_b, 0))
    dil_specs = []
    for _, d in DIL_GROUPS:
        assert t % (BF16_SUBLANES * d) == 0
        dil_specs.append(pl.BlockSpec((None, d, t // d, LANE_TILE), lambda i: (i // per_b, 0, i % per_b, 0)))
    return pl.pallas_call(
        _post_kernel,
        grid=(b_ * per_b,),
        in_specs=[
            tok(D_MODEL),
            pl.BlockSpec((None, None, N_MOD, D_MODEL), lambda i: (layer, i // per_b, 0, 0)),
            _layer_spec((1, D_MODEL), layer), _layer_spec((1, D_MODEL), layer),
            *dil_specs, *dil_specs,
            tok(WIDTH_B),
            _layer_spec((D_MODEL, 2 * D_MODEL), layer), _layer_spec((1, 2 * D_MODEL), layer),
            _layer_spec((WIDTH_A, D_MODEL), layer), _layer_spec((WIDTH_B, D_MODEL), layer),
            _layer_spec((D_MODEL, D_MODEL), layer),
            _layer_spec((D_MODEL, D_FF), layer), _layer_spec((D_FF, D_MODEL), layer),
        ],
        out_specs=tok(D_MODEL),
        out_shape=jax.ShapeDtypeStruct(x.shape, F32),
        scratch_shapes=[pltpu.VMEM((t, VREG_LANES), F32)] * 8,
        compiler_params=pltpu.CompilerParams(vmem_limit_bytes=VMEM_LIMIT),
        name="post_attention_mlp",
    )(x, mod, n1, n2, *oas, *lses, ob, wg, bg, wua, wub, wo, w1, w2)


def _head_sum_matrix():
    head = np.arange(LANE_TILE) // HEAD_DIM
    return jnp.asarray((head[:, None] == head[None, :]).astype(np.float32), dtype=BF16)


def kernel(x_prompt, x_sample, c_prompt, c_sample, norm1_g, norm2_g, w_mod, b_mod, w_in, q_norm_g, k_norm_g,
           rel_bias, rpb, w_gate, b_gate, w_up_a, w_up_b, w_o, w_ff1, w_ff2):
    n_prompt = c_prompt.shape[0]
    mod_all = _modulation(jnp.concatenate([c_prompt, c_sample], axis=0), w_mod, b_mod)
    mod_all = mod_all.reshape(DEPTH, -1, N_MOD, D_MODEL)
    ones = _head_sum_matrix()
    qg = (q_norm_g.reshape(DEPTH, 1, QKV_PART) * (HEAD_DIM ** -0.5 * LOG2E)).astype(F32)
    kg = k_norm_g.reshape(DEPTH, 1, QKV_PART).astype(F32)
    n1 = norm1_g.reshape(DEPTH, 1, D_MODEL)
    n2 = norm2_g.reshape(DEPTH, 1, D_MODEL)
    bg = b_gate.reshape(DEPTH, 1, 2 * D_MODEL)
    w_in_b, w_gate_b = w_in.astype(BF16), w_gate.astype(BF16)
    w_up_a_b, w_up_b_b, w_o_b = w_up_a.astype(BF16), w_up_b.astype(BF16), w_o.astype(BF16)
    w_ff1_b, w_ff2_b = w_ff1.astype(BF16), w_ff2.astype(BF16)

    xs = (x_prompt, x_sample)
    bias_b = _bias_b(rpb)
    bias_a = {}
    for x in xs:
        for g, (w, d) in enumerate(DIL_GROUPS):
            n = x.shape[1] // d
            if (g, min(2 * QB, n)) not in bias_a:
                bias_a[(g, min(2 * QB, n))] = _bias_a(rel_bias, g, w, d, n)

    ys = []
    for x, mod_x in ((x_prompt, mod_all[:, :n_prompt]), (x_sample, mod_all[:, n_prompt:])):
        l_ = x.shape[1]
        for l in range(DEPTH):
            q, k, v = _pre_attention(x, mod_x, n1, w_in_b, qg, kg, ones, l)
            oas, lses = [], []
            for g, (w, d) in enumerate(DIL_GROUPS):
                o, lse = _dilated_group(q[g], k[g], v[g], bias_a[(g, min(2 * QB, l_ // d))], d)
                oas.append(o)
                lses.append(lse)
            ob = _neighborhood(q[N_GROUPS], k[N_GROUPS], v[N_GROUPS], bias_b, l)
            x = _post_attention(x, mod_x, n1, n2, oas, lses, ob, w_gate_b, bg, w_up_a_b, w_up_b_b, w_o_b,
                                w_ff1_b, w_ff2_b, l)
        ys.append(x)
    return tuple(ys)
```

```python
import functools

import jax
import jax.numpy as jnp
import numpy as np
from jax import lax
from jax.experimental import pallas as pl
from jax.experimental.pallas import tpu as pltpu

D_MODEL = 1024
DEPTH = 2
HEAD_DIM = 64
DIL_GROUPS = ((128, 1), (512, 4), (2048, 16))
HEADS_PER_GROUP = 4
N_GROUPS = len(DIL_GROUPS)
N_HEADS_A = HEADS_PER_GROUP * N_GROUPS
N_HEADS_B = 8
N_HEADS = N_HEADS_A + N_HEADS_B
WIDTH_A = HEADS_PER_GROUP * HEAD_DIM
WIDTH_B = N_HEADS_B * HEAD_DIM
QKV_PART = N_HEADS * HEAD_DIM
N_BUCKETS = 32
MAX_DISTANCE = 1024
GRID_W = 64
KH_MAX = 8
KW = 16
D_FF = 4 * D_MODEL
N_MOD = 6
EPS = 1e-6
NEG = -1e30
LOG2E = 1.4426950408889634
LN2 = 0.6931471805599453

LANE_TILE = 256
QB = 128
SLAB = 16
HALO_A = 64
KB_B = KH_MAX * GRID_W
ROWS_PER_STEP_B = 32
MAX_CHUNK_A = 4096
INTERLEAVE_B = 16
INTERLEAVE = 16
TOK_TILE_PRE = 512
TOK_TILE_POST = 512
FF_CHUNK = 1024
BF16_SUBLANES = 16
VREG_LANES = 128
LANE_HALVES = (slice(0, VREG_LANES), slice(VREG_LANES, LANE_TILE))
MOD_COL_BLOCK = 1536
VMEM_LIMIT = 56 * 1024 * 1024

F32 = jnp.float32
BF16 = jnp.bfloat16


def _rms(x, g):
    return x * lax.rsqrt(jnp.mean(x * x, axis=-1, keepdims=True) + EPS) * g


def _const_spec(shape):
    return pl.BlockSpec(shape, lambda *_: (0,) * len(shape), pipeline_mode=pl.Buffered(1))


def _layer_spec(shape, layer):
    return pl.BlockSpec((None,) + shape, lambda *_: (layer,) + (0,) * len(shape), pipeline_mode=pl.Buffered(1))


def _mod_kernel(c_ref, w_ref, b_ref, o_ref):
    c = c_ref[...]
    s = c * jax.nn.sigmoid(c)
    o_ref[...] = jnp.dot(s, w_ref[...], preferred_element_type=F32) + b_ref[...]


def _modulation(c_all, w_mod, b_mod):
    nb = c_all.shape[0]
    ncol = N_MOD * D_MODEL
    blk = MOD_COL_BLOCK
    assert ncol % blk == 0
    return pl.pallas_call(
        _mod_kernel,
        grid=(DEPTH, ncol // blk),
        in_specs=[
            pl.BlockSpec((nb, D_MODEL), lambda l, j: (0, 0)),
            pl.BlockSpec((None, D_MODEL, blk), lambda l, j: (l, 0, j)),
            pl.BlockSpec((None, 1, blk), lambda l, j: (l, 0, j)),
        ],
        out_specs=pl.BlockSpec((None, nb, blk), lambda l, j: (l, 0, j)),
        out_shape=jax.ShapeDtypeStruct((DEPTH, nb, ncol), F32),
        compiler_params=pltpu.CompilerParams(vmem_limit_bytes=VMEM_LIMIT),
        name="modulation",
    )(c_all, w_mod, b_mod.reshape(DEPTH, 1, ncol))


def _pre_kernel(x_ref, mod_ref, n1_ref, win_ref, qg_ref, kg_ref, ones_ref, wg_ref, bg_ref, *refs):
    scr_refs = refs[-2:]
    gate_ref = refs[-3]
    outs = refs[:-3]
    t = x_ref.shape[0]
    x = x_ref[...]
    sh1 = mod_ref[0:1, :]
    sc1 = mod_ref[1:2, :]
    h = (_rms(x, n1_ref[...]) * (1.0 + sc1) + sh1).astype(BF16)
    ones = ones_ref[...]
    n_chunk = QKV_PART // LANE_TILE
    per_part = N_GROUPS + 1

    def project(part):
        return jnp.dot(h, win_ref[:, part * QKV_PART:(part + 1) * QKV_PART], preferred_element_type=F32)

    def finish(part, g_ref, y_part):
        if g_ref is not None:
            sq = (y_part * y_part).astype(BF16)
            sq = jnp.concatenate([sq[:, c * LANE_TILE:(c + 1) * LANE_TILE] for c in range(n_chunk)], axis=0)
            ss_all = jnp.dot(sq, ones, preferred_element_type=F32)
        for c in range(n_chunk):
            y = y_part[:, c * LANE_TILE:(c + 1) * LANE_TILE]
            if g_ref is not None:
                ss = ss_all[c * t:(c + 1) * t, :]
                y = y * lax.rsqrt(ss * (1.0 / HEAD_DIM) + EPS) * g_ref[:, c * LANE_TILE:(c + 1) * LANE_TILE]
            if c < N_GROUPS:
                out = outs[part * per_part + c]
                d = DIL_GROUPS[c][1]
                if d == 1:
                    out[0] = y.astype(out.dtype)
                else:
                    for s_ref, lanes in zip(scr_refs, LANE_HALVES):
                        s_ref[...] = y[:, lanes]
                    for r in range(d):
                        for s_ref, lanes in zip(scr_refs, LANE_HALVES):
                            out[r, :, lanes] = s_ref[pl.ds(r, t // d, stride=d), :].astype(out.dtype)
            else:
                out = outs[part * per_part + N_GROUPS]
                cb = c - N_GROUPS
                out[:, cb * LANE_TILE:(cb + 1) * LANE_TILE] = y.astype(out.dtype)

    for part, g_ref in enumerate((qg_ref, kg_ref, None)):
        finish(part, g_ref, project(part))
    gate_ref[...] = jax.nn.sigmoid(
        jnp.dot(h, wg_ref[...], preferred_element_type=F32) + bg_ref[...]).astype(gate_ref.dtype)


def _pre_attention(x, mod, n1, w_in, qg, kg, ones, wg, bg, layer):
    b_, l_, _ = x.shape
    t = TOK_TILE_PRE
    per_b = l_ // t
    tok = lambda w: pl.BlockSpec((None, t, w), lambda i: (i // per_b, i % per_b, 0))
    out_specs, out_shape = [], []
    for _ in range(3):
        for _, d in DIL_GROUPS:
            assert t % (BF16_SUBLANES * d) == 0
            out_specs.append(pl.BlockSpec((None, d, t // d, LANE_TILE), lambda i: (i // per_b, 0, i % per_b, 0)))
            out_shape.append(jax.ShapeDtypeStruct((b_, d, l_ // d, LANE_TILE), BF16))
        out_specs.append(tok(WIDTH_B))
        out_shape.append(jax.ShapeDtypeStruct((b_, l_, WIDTH_B), BF16))
    out_specs.append(tok(2 * D_MODEL))
    out_shape.append(jax.ShapeDtypeStruct((b_, l_, 2 * D_MODEL), BF16))
    outs = pl.pallas_call(
        _pre_kernel,
        grid=(b_ * per_b,),
        in_specs=[
            tok(D_MODEL),
            pl.BlockSpec((None, None, N_MOD, D_MODEL), lambda i: (layer, i // per_b, 0, 0)),
            _layer_spec((1, D_MODEL), layer),
            _layer_spec((D_MODEL, 3 * QKV_PART), layer),
            _layer_spec((1, QKV_PART), layer),
            _layer_spec((1, QKV_PART), layer),
            _const_spec((LANE_TILE, LANE_TILE)),
            _layer_spec((D_MODEL, 2 * D_MODEL), layer),
            _layer_spec((1, 2 * D_MODEL), layer),
        ],
        out_specs=out_specs,
        out_shape=out_shape,
        scratch_shapes=[pltpu.VMEM((t, VREG_LANES), F32)] * 2,
        compiler_params=pltpu.CompilerParams(vmem_limit_bytes=VMEM_LIMIT),
        name="pre_attention",
    )(x, mod, n1, w_in, qg, kg, ones, wg, bg)
    per_part = N_GROUPS + 1
    return [outs[p * per_part:(p + 1) * per_part] for p in range(3)] + [outs[-1]]


def _attn_core(q, k, v, bias):
    nq = q.shape[0]
    head_of_lane = lax.broadcasted_iota(jnp.int32, (SLAB, LANE_TILE), 1) // HEAD_DIM
    zero = jnp.zeros((SLAB, LANE_TILE), q.dtype)
    qs = jnp.concatenate([jnp.where(head_of_lane == h, q[j:j + SLAB, :], zero)
                          for j in range(0, nq, SLAB) for h in range(HEADS_PER_GROUP)], axis=0)
    s = lax.dot_general(qs, k, (((1,), (1,)), ((), ())), preferred_element_type=F32) + bias
    m = jnp.max(s, axis=-1, keepdims=True)
    p = jnp.exp2(s - m)
    l = jnp.sum(p, axis=-1, keepdims=True)
    o4 = jnp.dot(p.astype(BF16), v, preferred_element_type=F32)

    def unstack(x4):
        slabs = []
        for j in range(0, HEADS_PER_GROUP * nq, HEADS_PER_GROUP * SLAB):
            x = jnp.broadcast_to(x4[j:j + SLAB, :], (SLAB, LANE_TILE))
            for h in range(1, HEADS_PER_GROUP):
                x = jnp.where(head_of_lane == h, x4[j + h * SLAB:j + (h + 1) * SLAB, :], x)
            slabs.append(x)
        return jnp.concatenate(slabs, axis=0)

    l_dense = unstack(l)
    o = unstack(o4) * (1.0 / l_dense)
    lse = unstack(m) * LN2 + jnp.log(l_dense)
    return o, lse


def _slab_major(bias):
    lead, (rows, nk) = bias.shape[:-2], bias.shape[-2:]
    b = bias.reshape(lead + (HEADS_PER_GROUP, rows // (HEADS_PER_GROUP * SLAB), SLAB, nk))
    return jnp.swapaxes(b, -4, -3).reshape(bias.shape)


def _toeplitz(f, nq, nk):
    p = nq + nk - 1
    assert f.shape[-1] == p and nq >= 2
    u = jnp.roll(f, -(nq - 1), axis=-1)
    flat = jnp.tile(u, nq)[..., :nq * (p - 1)]
    return flat.reshape(f.shape[:-1] + (nq, p - 1))[..., :nk]


def _attn_a_kernel(q_ref, k_ref, v_ref, bias_ref, o_ref, lse_ref, *, n, kb):
    i = pl.program_id(2)
    n_res, chunk, _ = q_ref.shape
    blocks_per_chunk = chunk // QB
    last_block = n // QB - 1

    def one_block(item):
        res, j = (0, item) if n_res == 1 else (item // blocks_per_chunk, item % blocks_per_chunk)
        blk = i * blocks_per_chunk + j
        start = pl.multiple_of(jnp.clip(blk * QB - HALO_A, 0, n - kb), HALO_A)
        var = jnp.where(blk == 0, 0, jnp.where(blk == last_block, 2, 1))
        rows = pl.ds(pl.multiple_of(j * QB, QB), QB)
        o, lse = _attn_core(q_ref[res, rows, :], k_ref[res, pl.ds(start, kb), :], v_ref[res, pl.ds(start, kb), :],
                            bias_ref[var])
        o_ref[res, rows, :] = o.astype(o_ref.dtype)
        lse_ref[res, rows, :] = lse

    def body(t, carry):
        for u in range(INTERLEAVE):
            one_block(t * INTERLEAVE + u)
        return carry

    lax.fori_loop(0, n_res * blocks_per_chunk // INTERLEAVE, body, 0)


def _dilated_group(q, k, v, bias, d):
    b_, d_, n, _ = q.shape
    assert d_ == d
    kb = bias.shape[-1]
    chunk = min(n, MAX_CHUNK_A)
    n_res = min(d, MAX_CHUNK_A // chunk)
    assert n % QB == 0 and n % chunk == 0 and kb == min(2 * QB, n) and (n == QB or n >= 2 * QB)
    assert d % n_res == 0 and (n_res * chunk // QB) % INTERLEAVE == 0
    qspec = pl.BlockSpec((None, n_res, chunk, LANE_TILE), lambda b, r, i: (b, r, i, 0))
    kvspec = pl.BlockSpec((None, n_res, n, LANE_TILE), lambda b, r, i: (b, r, 0, 0))
    return pl.pallas_call(
        functools.partial(_attn_a_kernel, n=n, kb=kb),
        grid=(b_, d // n_res, n // chunk),
        in_specs=[qspec, kvspec, kvspec, pl.BlockSpec(bias.shape, lambda b, r, i: (0, 0, 0))],
        out_specs=[qspec, qspec],
        out_shape=[jax.ShapeDtypeStruct(q.shape, BF16), jax.ShapeDtypeStruct(q.shape, F32)],
        compiler_params=pltpu.CompilerParams(vmem_limit_bytes=VMEM_LIMIT),
        name=f"dilated_attention_d{d}",
    )(q, k, v, bias)


def _t5_bucket(rel):
    nb = N_BUCKETS // 2
    max_exact = nb // 2
    ret = (rel > 0).astype(np.int32) * nb
    n = np.abs(rel)
    large = max_exact + (np.log(np.maximum(n, 1) / max_exact) / np.log(MAX_DISTANCE / max_exact) * (nb - max_exact)).astype(np.int32)
    large = np.minimum(large, nb - 1)
    return (ret + np.where(n < max_exact, n, large)).astype(np.int32)


def _bias_a(table, g, w, d, n):
    half = w // (2 * d)
    assert half == HALO_A
    kb = min(2 * QB, n)
    sl = slice(g * HEADS_PER_GROUP, (g + 1) * HEADS_PER_GROUP)
    offs = np.array((0, -HALO_A, QB - kb))
    rel = np.arange(QB + kb - 1)[None, :] - (QB - 1) + offs[:, None]
    f = jnp.transpose(table[_t5_bucket(d * rel)][:, :, sl], (0, 2, 1)).astype(F32) * LOG2E
    f = jnp.where((np.abs(rel) <= half)[:, None, :], f, NEG)
    return _slab_major(_toeplitz(f, QB, kb).reshape(3, HEADS_PER_GROUP * QB, kb))


def _attn_b_kernel(q_ref, k_ref, v_ref, bias_ref, o_ref, *, rows):
    i = pl.program_id(2)

    def one_row(a):
        r = i * ROWS_PER_STEP_B + a
        band = jnp.clip(r - KH_MAX // 2, 0, rows - KH_MAX)
        start = pl.multiple_of(band * GRID_W, GRID_W)
        qrows = pl.ds(pl.multiple_of(a * GRID_W, GRID_W), GRID_W)
        o, _ = _attn_core(q_ref[qrows, :], k_ref[pl.ds(start, KB_B), :], v_ref[pl.ds(start, KB_B), :],
                          bias_ref[r - band])
        o_ref[qrows, :] = o.astype(o_ref.dtype)

    def body(t, carry):
        for u in range(INTERLEAVE_B):
            one_row(t * INTERLEAVE_B + u)
        return carry

    lax.fori_loop(0, ROWS_PER_STEP_B // INTERLEAVE_B, body, 0)


def _neighborhood(qb, kb, vb, bias, layer):
    b_, l_, _ = qb.shape
    rows = l_ // GRID_W
    assert rows % ROWS_PER_STEP_B == 0 and rows >= KH_MAX
    halves = WIDTH_B // LANE_TILE
    qspec = pl.BlockSpec((None, ROWS_PER_STEP_B * GRID_W, LANE_TILE), lambda hf, b, i: (b, i, hf))
    kvspec = pl.BlockSpec((None, l_, LANE_TILE), lambda hf, b, i: (b, 0, hf))
    bspec = pl.BlockSpec((None, KH_MAX, None, HEADS_PER_GROUP * GRID_W, KB_B), lambda hf, b, i: (layer, 0, hf, 0, 0))
    return pl.pallas_call(
        functools.partial(_attn_b_kernel, rows=rows),
        grid=(halves, b_, rows // ROWS_PER_STEP_B),
        in_specs=[qspec, kvspec, kvspec, bspec],
        out_specs=qspec,
        out_shape=jax.ShapeDtypeStruct((b_, l_, WIDTH_B), BF16),
        compiler_params=pltpu.CompilerParams(vmem_limit_bytes=VMEM_LIMIT),
        name="neighborhood_attention",
    )(qb, kb, vb, bias)


def _bias_b(rpb):
    cols = np.arange(GRID_W)
    c_start = np.clip(cols - KW // 2, 0, GRID_W - KW)
    col_ok = (cols[None, :] >= c_start[:, None]) & (cols[None, :] < c_start[:, None] + KW)
    assert np.all(np.abs(cols[None, :] - cols[:, None])[col_ok] <= KW - 1)
    pad = GRID_W - KW
    toep = _toeplitz(jnp.pad(rpb.astype(F32) * LOG2E, ((0, 0), (0, 0), (0, 0), (pad, pad))), GRID_W, GRID_W)
    halves = WIDTH_B // LANE_TILE
    toep = jnp.where(col_ok, toep, NEG).reshape(DEPTH, halves, HEADS_PER_GROUP, 2 * KH_MAX - 1,
                                                GRID_W // SLAB, SLAB, GRID_W)
    toep = jnp.transpose(toep, (0, 1, 4, 2, 5, 3, 6))
    shape = (DEPTH, halves, HEADS_PER_GROUP * GRID_W, KB_B)
    return jnp.stack([toep[..., KH_MAX - 1 - v:2 * KH_MAX - 1 - v, :].reshape(shape) for v in range(KH_MAX)], axis=1)


def _post_kernel(x_ref, mod_ref, n2_ref, oa0_ref, oa1_ref, oa2_ref, l0_ref, l1_ref, l2_ref, ob_ref, gate_ref,
                 wua_ref, wub_ref, wo_ref, w1_ref, w2_ref, out_ref, *scr):
    t = x_ref.shape[0]
    x = x_ref[...]
    g1 = mod_ref[2:3, :]
    sh2, sc2, g2 = mod_ref[3:4, :], mod_ref[4:5, :], mod_ref[5:6, :]
    gate = gate_ref[...]

    def token_order(ref, scr_refs):
        d = ref.shape[0]
        if d == 1:
            return ref[0].astype(F32)
        for r in range(d):
            for s_ref, lanes in zip(scr_refs, LANE_HALVES):
                s_ref[pl.ds(r, t // d, stride=d), :] = ref[r, :, lanes].astype(F32)
        return jnp.concatenate([s_ref[...] for s_ref in scr_refs], axis=-1)

    oa0, oa1, oa2 = token_order(oa0_ref, None), token_order(oa1_ref, scr[0:2]), token_order(oa2_ref, scr[2:4])
    l0, l1, l2 = token_order(l0_ref, None), token_order(l1_ref, scr[4:6]), token_order(l2_ref, scr[6:8])
    mx = jnp.maximum(jnp.maximum(l0, l1), l2)
    e0, e1, e2 = jnp.exp(l0 - mx), jnp.exp(l1 - mx), jnp.exp(l2 - mx)
    inv = 1.0 / (e0 + e1 + e2)
    oa = (e0 * inv) * oa0 + (e1 * inv) * oa1 + (e2 * inv) * oa2

    ua = jnp.dot(oa.astype(BF16), wua_ref[...], preferred_element_type=F32)
    ub = jnp.dot(ob_ref[...], wub_ref[...], preferred_element_type=F32)
    mixed = gate[:, :D_MODEL] * ua + gate[:, D_MODEL:] * ub
    x1 = x + g1 * jnp.dot(mixed.astype(BF16), wo_ref[...], preferred_element_type=F32)

    h2 = (_rms(x1, n2_ref[...]) * (1.0 + sc2) + sh2).astype(BF16)
    acc = jnp.zeros_like(x1)
    for c in range(D_FF // FF_CHUNK):
        f = jnp.dot(h2, w1_ref[:, c * FF_CHUNK:(c + 1) * FF_CHUNK], preferred_element_type=F32)
        f = jnp.square(jnp.maximum(f, 0.0)).astype(BF16)
        acc = acc + jnp.dot(f, w2_ref[c * FF_CHUNK:(c + 1) * FF_CHUNK, :], preferred_element_type=F32)
    out_ref[...] = x1 + g2 * acc


def _post_attention(x, mod, n2, oas, lses, ob, gate, wua, wub, wo, w1, w2, layer):
    b_, l_, _ = x.shape
    t = TOK_TILE_POST
    per_b = l_ // t
    tok = lambda w: pl.BlockSpec((None, t, w), lambda i: (i // per_b, i % per_b, 0))
    dil_specs = []
    for _, d in DIL_GROUPS:
        assert t % (BF16_SUBLANES * d) == 0
        dil_specs.append(pl.BlockSpec((None, d, t // d, LANE_TILE), lambda i: (i // per_b, 0, i % per_b, 0)))
    return pl.pallas_call(
        _post_kernel,
        grid=(b_ * per_b,),
        in_specs=[
            tok(D_MODEL),
            pl.BlockSpec((None, None, N_MOD, D_MODEL), lambda i: (layer, i // per_b, 0, 0)),
            _layer_spec((1, D_MODEL), layer),
            *dil_specs, *dil_specs,
            tok(WIDTH_B),
            tok(2 * D_MODEL),
            _layer_spec((WIDTH_A, D_MODEL), layer), _layer_spec((WIDTH_B, D_MODEL), layer),
            _layer_spec((D_MODEL, D_MODEL), layer),
            _layer_spec((D_MODEL, D_FF), layer), _layer_spec((D_FF, D_MODEL), layer),
        ],
        out_specs=tok(D_MODEL),
        out_shape=jax.ShapeDtypeStruct(x.shape, F32),
        scratch_shapes=[pltpu.VMEM((t, VREG_LANES), F32)] * 8,
        compiler_params=pltpu.CompilerParams(vmem_limit_bytes=VMEM_LIMIT),
        name="post_attention_mlp",
    )(x, mod, n2, *oas, *lses, ob, gate, wua, wub, wo, w1, w2)


def _head_sum_matrix():
    head = np.arange(LANE_TILE) // HEAD_DIM
    return jnp.asarray((head[:, None] == head[None, :]).astype(np.float32), dtype=BF16)


def kernel(x_prompt, x_sample, c_prompt, c_sample, norm1_g, norm2_g, w_mod, b_mod, w_in, q_norm_g, k_norm_g,
           rel_bias, rpb, w_gate, b_gate, w_up_a, w_up_b, w_o, w_ff1, w_ff2):
    n_prompt = c_prompt.shape[0]
    mod_all = _modulation(jnp.concatenate([c_prompt, c_sample], axis=0), w_mod, b_mod)
    mod_all = mod_all.reshape(DEPTH, -1, N_MOD, D_MODEL)
    ones = _head_sum_matrix()
    qg = (q_norm_g.reshape(DEPTH, 1, QKV_PART) * (HEAD_DIM ** -0.5 * LOG2E)).astype(F32)
    kg = k_norm_g.reshape(DEPTH, 1, QKV_PART).astype(F32)
    n1 = norm1_g.reshape(DEPTH, 1, D_MODEL)
    n2 = norm2_g.reshape(DEPTH, 1, D_MODEL)
    bg = b_gate.reshape(DEPTH, 1, 2 * D_MODEL)
    w_in_b, w_gate_b = w_in.astype(BF16), w_gate.astype(BF16)
    w_up_a_b, w_up_b_b, w_o_b = w_up_a.astype(BF16), w_up_b.astype(BF16), w_o.astype(BF16)
    w_ff1_b, w_ff2_b = w_ff1.astype(BF16), w_ff2.astype(BF16)

    xs = (x_prompt, x_sample)
    bias_b = _bias_b(rpb)
    bias_a = {}
    for x in xs:
        for g, (w, d) in enumerate(DIL_GROUPS):
            n = x.shape[1] // d
            if (g, min(2 * QB, n)) not in bias_a:
                bias_a[(g, min(2 * QB, n))] = _bias_a(rel_bias, g, w, d, n)

    ys = []
    for x, mod_x in ((x_prompt, mod_all[:, :n_prompt]), (x_sample, mod_all[:, n_prompt:])):
        l_ = x.shape[1]
        for l in range(DEPTH):
            q, k, v, gate = _pre_attention(x, mod_x, n1, w_in_b, qg, kg, ones, w_gate_b, bg, l)
            oas, lses = [], []
            for g, (w, d) in enumerate(DIL_GROUPS):
                o, lse = _dilated_group(q[g], k[g], v[g], bias_a[(g, min(2 * QB, l_ // d))], d)
                oas.append(o)
                lses.append(lse)
            ob = _neighborhood(q[N_GROUPS], k[N_GROUPS], v[N_GROUPS], bias_b, l)
            x = _post_attention(x, mod_x, n2, oas, lses, ob, gate, w_up_a_b, w_up_b_b, w_o_b, w_ff1_b, w_ff2_b, l)
        ys.append(x)
    return tuple(ys)
```

```python
import functools

import jax
import jax.numpy as jnp
import numpy as np
from jax import lax
from jax.experimental import pallas as pl
from jax.experimental.pallas import tpu as pltpu

D_MODEL = 1024
DEPTH = 2
HEAD_DIM = 64
DIL_GROUPS = ((128, 1), (512, 4), (2048, 16))
HEADS_PER_GROUP = 4
N_GROUPS = len(DIL_GROUPS)
N_HEADS_A = HEADS_PER_GROUP * N_GROUPS
N_HEADS_B = 8
N_HEADS = N_HEADS_A + N_HEADS_B
WIDTH_A = HEADS_PER_GROUP * HEAD_DIM
WIDTH_B = N_HEADS_B * HEAD_DIM
QKV_PART = N_HEADS * HEAD_DIM
N_BUCKETS = 32
MAX_DISTANCE = 1024
GRID_W = 64
KH_MAX = 8
KW = 16
D_FF = 4 * D_MODEL
N_MOD = 6
EPS = 1e-6
NEG = -1e30
LOG2E = 1.4426950408889634
LN2 = 0.6931471805599453

LANE_TILE = 256
QB = 128
SLAB = 16
HALO_A = 64
KB_B = KH_MAX * GRID_W
ROWS_PER_STEP_B = 32
INPUT_BUFFERS_A = 3
MAX_CHUNK_A = 4096
INTERLEAVE_B = 16
INTERLEAVE = 16
TOK_TILE_PRE = 1024
TOK_TILE_POST = 512
FF_CHUNK = 1024
BF16_SUBLANES = 16
VREG_LANES = 128
LANE_HALVES = (slice(0, VREG_LANES), slice(VREG_LANES, LANE_TILE))
MOD_COL_BLOCK = 1536
VMEM_LIMIT = 56 * 1024 * 1024

F32 = jnp.float32
BF16 = jnp.bfloat16


def _rms(x, g):
    return x * lax.rsqrt(jnp.mean(x * x, axis=-1, keepdims=True) + EPS) * g


def _const_spec(shape):
    return pl.BlockSpec(shape, lambda *_: (0,) * len(shape), pipeline_mode=pl.Buffered(1))


def _layer_spec(shape, layer):
    return pl.BlockSpec((None,) + shape, lambda *_: (layer,) + (0,) * len(shape), pipeline_mode=pl.Buffered(1))


def _mod_kernel(c_ref, w_ref, b_ref, o_ref):
    c = c_ref[...]
    s = c * jax.nn.sigmoid(c)
    o_ref[...] = jnp.dot(s, w_ref[...], preferred_element_type=F32) + b_ref[...]


def _modulation(c_all, w_mod, b_mod):
    nb = c_all.shape[0]
    ncol = N_MOD * D_MODEL
    blk = MOD_COL_BLOCK
    assert ncol % blk == 0
    return pl.pallas_call(
        _mod_kernel,
        grid=(DEPTH, ncol // blk),
        in_specs=[
            pl.BlockSpec((nb, D_MODEL), lambda l, j: (0, 0)),
            pl.BlockSpec((None, D_MODEL, blk), lambda l, j: (l, 0, j)),
            pl.BlockSpec((None, 1, blk), lambda l, j: (l, 0, j)),
        ],
        out_specs=pl.BlockSpec((None, nb, blk), lambda l, j: (l, 0, j)),
        out_shape=jax.ShapeDtypeStruct((DEPTH, nb, ncol), F32),
        compiler_params=pltpu.CompilerParams(vmem_limit_bytes=VMEM_LIMIT),
        name="modulation",
    )(c_all, w_mod, b_mod.reshape(DEPTH, 1, ncol))


def _pre_kernel(x_ref, mod_ref, n1_ref, win_ref, qg_ref, kg_ref, ones_ref, *refs):
    scr_refs = refs[-2:]
    outs = refs[:-2]
    t = x_ref.shape[0]
    x = x_ref[...]
    sh1 = mod_ref[0:1, :]
    sc1 = mod_ref[1:2, :]
    h = (_rms(x, n1_ref[...]) * (1.0 + sc1) + sh1).astype(BF16)
    ones = ones_ref[...]
    n_chunk = QKV_PART // LANE_TILE
    per_part = N_GROUPS + 1

    def project(part):
        return jnp.dot(h, win_ref[:, part * QKV_PART:(part + 1) * QKV_PART], preferred_element_type=F32)

    def finish(part, g_ref, y_part):
        if g_ref is not None:
            sq = (y_part * y_part).astype(BF16)
            sq = jnp.concatenate([sq[:, c * LANE_TILE:(c + 1) * LANE_TILE] for c in range(n_chunk)], axis=0)
            ss_all = jnp.dot(sq, ones, preferred_element_type=F32)
        for c in range(n_chunk):
            y = y_part[:, c * LANE_TILE:(c + 1) * LANE_TILE]
            if g_ref is not None:
                ss = ss_all[c * t:(c + 1) * t, :]
                y = y * lax.rsqrt(ss * (1.0 / HEAD_DIM) + EPS) * g_ref[:, c * LANE_TILE:(c + 1) * LANE_TILE]
            if c < N_GROUPS:
                out = outs[part * per_part + c]
                d = DIL_GROUPS[c][1]
                if d == 1:
                    out[0] = y.astype(out.dtype)
                else:
                    for s_ref, lanes in zip(scr_refs, LANE_HALVES):
                        s_ref[...] = y[:, lanes]
                    for r in range(d):
                        for s_ref, lanes in zip(scr_refs, LANE_HALVES):
                            out[r, :, lanes] = s_ref[pl.ds(r, t // d, stride=d), :].astype(out.dtype)
            else:
                out = outs[part * per_part + N_GROUPS]
                cb = c - N_GROUPS
                out[:, cb * LANE_TILE:(cb + 1) * LANE_TILE] = y.astype(out.dtype)

    for part, g_ref in enumerate((qg_ref, kg_ref, None)):
        finish(part, g_ref, project(part))


def _pre_attention(x, mod, n1, w_in, qg, kg, ones, layer):
    b_, l_, _ = x.shape
    t = TOK_TILE_PRE
    per_b = l_ // t
    tok = lambda w: pl.BlockSpec((None, t, w), lambda i: (i // per_b, i % per_b, 0))
    out_specs, out_shape = [], []
    for _ in range(3):
        for _, d in DIL_GROUPS:
            assert t % (BF16_SUBLANES * d) == 0
            out_specs.append(pl.BlockSpec((None, d, t // d, LANE_TILE), lambda i: (i // per_b, 0, i % per_b, 0)))
            out_shape.append(jax.ShapeDtypeStruct((b_, d, l_ // d, LANE_TILE), BF16))
        out_specs.append(tok(WIDTH_B))
        out_shape.append(jax.ShapeDtypeStruct((b_, l_, WIDTH_B), BF16))
    outs = pl.pallas_call(
        _pre_kernel,
        grid=(b_ * per_b,),
        in_specs=[
            tok(D_MODEL),
            pl.BlockSpec((None, None, N_MOD, D_MODEL), lambda i: (layer, i // per_b, 0, 0)),
            _layer_spec((1, D_MODEL), layer),
            _layer_spec((D_MODEL, 3 * QKV_PART), layer),
            _layer_spec((1, QKV_PART), layer),
            _layer_spec((1, QKV_PART), layer),
            _const_spec((LANE_TILE, LANE_TILE)),
        ],
        out_specs=out_specs,
        out_shape=out_shape,
        scratch_shapes=[pltpu.VMEM((t, VREG_LANES), F32)] * 2,
        compiler_params=pltpu.CompilerParams(vmem_limit_bytes=VMEM_LIMIT),
        name="pre_attention",
    )(x, mod, n1, w_in, qg, kg, ones)
    per_part = N_GROUPS + 1
    return [outs[p * per_part:(p + 1) * per_part] for p in range(3)]


def _attn_core(q, k, v, bias):
    nq = q.shape[0]
    head_of_lane = lax.broadcasted_iota(jnp.int32, (SLAB, LANE_TILE), 1) // HEAD_DIM
    zero = jnp.zeros((SLAB, LANE_TILE), q.dtype)
    qs = jnp.concatenate([jnp.where(head_of_lane == h, q[j:j + SLAB, :], zero)
                          for j in range(0, nq, SLAB) for h in range(HEADS_PER_GROUP)], axis=0)
    s = lax.dot_general(qs, k, (((1,), (1,)), ((), ())), preferred_element_type=F32) + bias
    m = jnp.max(s, axis=-1, keepdims=True)
    p = jnp.exp2(s - m)
    l = jnp.sum(p, axis=-1, keepdims=True)
    o4 = jnp.dot(p.astype(BF16), v, preferred_element_type=F32)

    def unstack(x4):
        slabs = []
        for j in range(0, HEADS_PER_GROUP * nq, HEADS_PER_GROUP * SLAB):
            x = jnp.broadcast_to(x4[j:j + SLAB, :], (SLAB, LANE_TILE))
            for h in range(1, HEADS_PER_GROUP):
                x = jnp.where(head_of_lane == h, x4[j + h * SLAB:j + (h + 1) * SLAB, :], x)
            slabs.append(x)
        return jnp.concatenate(slabs, axis=0)

    l_dense = unstack(l)
    o = unstack(o4) * (1.0 / l_dense)
    lse = unstack(m) * LN2 + jnp.log(l_dense)
    return o, lse


def _slab_major(bias):
    lead, (rows, nk) = bias.shape[:-2], bias.shape[-2:]
    b = bias.reshape(lead + (HEADS_PER_GROUP, rows // (HEADS_PER_GROUP * SLAB), SLAB, nk))
    return jnp.swapaxes(b, -4, -3).reshape(bias.shape)


def _toeplitz(f, nq, nk):
    p = nq + nk - 1
    assert f.shape[-1] == p and nq >= 2
    u = jnp.roll(f, -(nq - 1), axis=-1)
    flat = jnp.tile(u, nq)[..., :nq * (p - 1)]
    return flat.reshape(f.shape[:-1] + (nq, p - 1))[..., :nk]


def _attn_a_kernel(q_ref, k_ref, v_ref, bias_ref, o_ref, lse_ref, *, n, kb):
    i = pl.program_id(2)
    n_res, chunk, _ = q_ref.shape
    blocks_per_chunk = chunk // QB
    last_block = n // QB - 1

    def one_block(item):
        res, j = (0, item) if n_res == 1 else (item // blocks_per_chunk, item % blocks_per_chunk)
        blk = i * blocks_per_chunk + j
        start = pl.multiple_of(jnp.clip(blk * QB - HALO_A, 0, n - kb), HALO_A)
        var = jnp.where(blk == 0, 0, jnp.where(blk == last_block, 2, 1))
        rows = pl.ds(pl.multiple_of(j * QB, QB), QB)
        o, lse = _attn_core(q_ref[res, rows, :], k_ref[res, pl.ds(start, kb), :], v_ref[res, pl.ds(start, kb), :],
                            bias_ref[var])
        o_ref[res, rows, :] = o.astype(o_ref.dtype)
        lse_ref[res, rows, :] = lse

    def body(t, carry):
        for u in range(INTERLEAVE):
            one_block(t * INTERLEAVE + u)
        return carry

    lax.fori_loop(0, n_res * blocks_per_chunk // INTERLEAVE, body, 0)


def _dilated_group(q, k, v, bias, d):
    b_, d_, n, _ = q.shape
    assert d_ == d
    kb = bias.shape[-1]
    chunk = min(n, MAX_CHUNK_A)
    n_res = min(d, MAX_CHUNK_A // chunk)
    assert n % QB == 0 and n % chunk == 0 and kb == min(2 * QB, n) and (n == QB or n >= 2 * QB)
    assert d % n_res == 0 and (n_res * chunk // QB) % INTERLEAVE == 0
    q_idx = lambda b, r, i: (b, r, i, 0)
    qspec_in = pl.BlockSpec((None, n_res, chunk, LANE_TILE), q_idx, pipeline_mode=pl.Buffered(INPUT_BUFFERS_A))
    ospec = pl.BlockSpec((None, n_res, chunk, LANE_TILE), q_idx)
    kvspec = pl.BlockSpec((None, n_res, n, LANE_TILE), lambda b, r, i: (b, r, 0, 0))
    inner = pltpu.emit_pipeline(
        functools.partial(_attn_a_kernel, n=n, kb=kb),
        grid=(b_, d // n_res, n // chunk),
        in_specs=[qspec_in, kvspec, kvspec, pl.BlockSpec(bias.shape, lambda b, r, i: (0, 0, 0))],
        out_specs=[ospec, ospec],
    )

    def outer(q_hbm, k_hbm, v_hbm, bias_hbm, o_hbm, lse_hbm):
        inner(q_hbm, k_hbm, v_hbm, bias_hbm, o_hbm, lse_hbm)

    hbm = pl.BlockSpec(memory_space=pl.ANY)
    return pl.pallas_call(
        outer,
        in_specs=[hbm] * 4,
        out_specs=[hbm] * 2,
        out_shape=[jax.ShapeDtypeStruct(q.shape, BF16), jax.ShapeDtypeStruct(q.shape, F32)],
        compiler_params=pltpu.CompilerParams(vmem_limit_bytes=VMEM_LIMIT),
        name=f"dilated_attention_d{d}",
    )(q, k, v, bias)


def _t5_bucket(rel):
    nb = N_BUCKETS // 2
    max_exact = nb // 2
    ret = (rel > 0).astype(np.int32) * nb
    n = np.abs(rel)
    large = max_exact + (np.log(np.maximum(n, 1) / max_exact) / np.log(MAX_DISTANCE / max_exact) * (nb - max_exact)).astype(np.int32)
    large = np.minimum(large, nb - 1)
    return (ret + np.where(n < max_exact, n, large)).astype(np.int32)


def _bias_a(table, g, w, d, n):
    half = w // (2 * d)
    assert half == HALO_A
    kb = min(2 * QB, n)
    sl = slice(g * HEADS_PER_GROUP, (g + 1) * HEADS_PER_GROUP)
    offs = np.array((0, -HALO_A, QB - kb))
    rel = np.arange(QB + kb - 1)[None, :] - (QB - 1) + offs[:, None]
    f = jnp.transpose(table[_t5_bucket(d * rel)][:, :, sl], (0, 2, 1)).astype(F32) * LOG2E
    f = jnp.where((np.abs(rel) <= half)[:, None, :], f, NEG)
    return _slab_major(_toeplitz(f, QB, kb).reshape(3, HEADS_PER_GROUP * QB, kb))


def _attn_b_kernel(q_ref, k_ref, v_ref, bias_ref, o_ref, *, rows):
    i = pl.program_id(2)

    def one_row(a):
        r = i * ROWS_PER_STEP_B + a
        band = jnp.clip(r - KH_MAX // 2, 0, rows - KH_MAX)
        start = pl.multiple_of(band * GRID_W, GRID_W)
        qrows = pl.ds(pl.multiple_of(a * GRID_W, GRID_W), GRID_W)
        o, _ = _attn_core(q_ref[qrows, :], k_ref[pl.ds(start, KB_B), :], v_ref[pl.ds(start, KB_B), :],
                          bias_ref[r - band])
        o_ref[qrows, :] = o.astype(o_ref.dtype)

    def body(t, carry):
        for u in range(INTERLEAVE_B):
            one_row(t * INTERLEAVE_B + u)
        return carry

    lax.fori_loop(0, ROWS_PER_STEP_B // INTERLEAVE_B, body, 0)


def _neighborhood(qb, kb, vb, bias, layer):
    b_, l_, _ = qb.shape
    rows = l_ // GRID_W
    assert rows % ROWS_PER_STEP_B == 0 and rows >= KH_MAX
    halves = WIDTH_B // LANE_TILE
    qspec = pl.BlockSpec((None, ROWS_PER_STEP_B * GRID_W, LANE_TILE), lambda hf, b, i: (b, i, hf))
    kvspec = pl.BlockSpec((None, l_, LANE_TILE), lambda hf, b, i: (b, 0, hf))
    bspec = pl.BlockSpec((None, KH_MAX, None, HEADS_PER_GROUP * GRID_W, KB_B), lambda hf, b, i: (layer, 0, hf, 0, 0))
    return pl.pallas_call(
        functools.partial(_attn_b_kernel, rows=rows),
        grid=(halves, b_, rows // ROWS_PER_STEP_B),
        in_specs=[qspec, kvspec, kvspec, bspec],
        out_specs=qspec,
        out_shape=jax.ShapeDtypeStruct((b_, l_, WIDTH_B), BF16),
        compiler_params=pltpu.CompilerParams(vmem_limit_bytes=VMEM_LIMIT),
        name="neighborhood_attention",
    )(qb, kb, vb, bias)


def _bias_b(rpb):
    cols = np.arange(GRID_W)
    c_start = np.clip(cols - KW // 2, 0, GRID_W - KW)
    col_ok = (cols[None, :] >= c_start[:, None]) & (cols[None, :] < c_start[:, None] + KW)
    assert np.all(np.abs(cols[None, :] - cols[:, None])[col_ok] <= KW - 1)
    pad = GRID_W - KW
    toep = _toeplitz(jnp.pad(rpb.astype(F32) * LOG2E, ((0, 0), (0, 0), (0, 0), (pad, pad))), GRID_W, GRID_W)
    halves = WIDTH_B // LANE_TILE
    toep = jnp.where(col_ok, toep, NEG).reshape(DEPTH, halves, HEADS_PER_GROUP, 2 * KH_MAX - 1,
                                                GRID_W // SLAB, SLAB, GRID_W)
    toep = jnp.transpose(toep, (0, 1, 4, 2, 5, 3, 6))
    shape = (DEPTH, halves, HEADS_PER_GROUP * GRID_W, KB_B)
    return jnp.stack([toep[..., KH_MAX - 1 - v:2 * KH_MAX - 1 - v, :].reshape(shape) for v in range(KH_MAX)], axis=1)


def _post_kernel(x_ref, mod_ref, n1_ref, n2_ref, oa0_ref, oa1_ref, oa2_ref, l0_ref, l1_ref, l2_ref, ob_ref,
                 wg_ref, bg_ref, wua_ref, wub_ref, wo_ref, w1_ref, w2_ref, out_ref, *scr):
    t = x_ref.shape[0]
    x = x_ref[...]
    sh1, sc1, g1 = mod_ref[0:1, :], mod_ref[1:2, :], mod_ref[2:3, :]
    sh2, sc2, g2 = mod_ref[3:4, :], mod_ref[4:5, :], mod_ref[5:6, :]
    h = (_rms(x, n1_ref[...]) * (1.0 + sc1) + sh1).astype(BF16)
    gate = jax.nn.sigmoid(jnp.dot(h, wg_ref[...], preferred_element_type=F32) + bg_ref[...])

    def token_order(ref, scr_refs):
        d = ref.shape[0]
        if d == 1:
            return ref[0].astype(F32)
        for r in range(d):
            for s_ref, lanes in zip(scr_refs, LANE_HALVES):
                s_ref[pl.ds(r, t // d, stride=d), :] = ref[r, :, lanes].astype(F32)
        return jnp.concatenate([s_ref[...] for s_ref in scr_refs], axis=-1)

    oa0, oa1, oa2 = token_order(oa0_ref, None), token_order(oa1_ref, scr[0:2]), token_order(oa2_ref, scr[2:4])
    l0, l1, l2 = token_order(l0_ref, None), token_order(l1_ref, scr[4:6]), token_order(l2_ref, scr[6:8])
    mx = jnp.maximum(jnp.maximum(l0, l1), l2)
    e0, e1, e2 = jnp.exp(l0 - mx), jnp.exp(l1 - mx), jnp.exp(l2 - mx)
    inv = 1.0 / (e0 + e1 + e2)
    oa = (e0 * inv) * oa0 + (e1 * inv) * oa1 + (e2 * inv) * oa2

    ua = jnp.dot(oa.astype(BF16), wua_ref[...], preferred_element_type=F32)
    ub = jnp.dot(ob_ref[...], wub_ref[...], preferred_element_type=F32)
    mixed = gate[:, :D_MODEL] * ua + gate[:, D_MODEL:] * ub
    x1 = x + g1 * jnp.dot(mixed.astype(BF16), wo_ref[...], preferred_element_type=F32)

    h2 = (_rms(x1, n2_ref[...]) * (1.0 + sc2) + sh2).astype(BF16)
    acc = jnp.zeros_like(x1)
    for c in range(D_FF // FF_CHUNK):
        f = jnp.dot(h2, w1_ref[:, c * FF_CHUNK:(c + 1) * FF_CHUNK], preferred_element_type=F32)
        f = jnp.square(jnp.maximum(f, 0.0)).astype(BF16)
        acc = acc + jnp.dot(f, w2_ref[c * FF_CHUNK:(c + 1) * FF_CHUNK, :], preferred_element_type=F32)
    out_ref[...] = x1 + g2 * acc


def _post_attention(x, mod, n1, n2, oas, lses, ob, wg, bg, wua, wub, wo, w1, w2, layer):
    b_, l_, _ = x.shape
    t = TOK_TILE_POST
    per_b = l_ // t
    tok = lambda w: pl.BlockSpec((None, t, w), lambda i: (i // per_b, i % per_b, 0))
    dil_specs = []
    for _, d in DIL_GROUPS:
        assert t % (BF16_SUBLANES * d) == 0
        dil_specs.append(pl.BlockSpec((None, d, t // d, LANE_TILE), lambda i: (i // per_b, 0, i % per_b, 0)))
    return pl.pallas_call(
        _post_kernel,
        grid=(b_ * per_b,),
        in_specs=[
            tok(D_MODEL),
            pl.BlockSpec((None, None, N_MOD, D_MODEL), lambda i: (layer, i // per_b, 0, 0)),
            _layer_spec((1, D_MODEL), layer), _layer_spec((1, D_MODEL), layer),
            *dil_specs, *dil_specs,
            tok(WIDTH_B),
            _layer_spec((D_MODEL, 2 * D_MODEL), layer), _layer_spec((1, 2 * D_MODEL), layer),
            _layer_spec((WIDTH_A, D_MODEL), layer), _layer_spec((WIDTH_B, D_MODEL), layer),
            _layer_spec((D_MODEL, D_MODEL), layer),
            _layer_spec((D_MODEL, D_FF), layer), _layer_spec((D_FF, D_MODEL), layer),
        ],
        out_specs=tok(D_MODEL),
        out_shape=jax.ShapeDtypeStruct(x.shape, F32),
        scratch_shapes=[pltpu.VMEM((t, VREG_LANES), F32)] * 8,
        compiler_params=pltpu.CompilerParams(vmem_limit_bytes=VMEM_LIMIT),
        name="post_attention_mlp",
    )(x, mod, n1, n2, *oas, *lses, ob, wg, bg, wua, wub, wo, w1, w2)


def _head_sum_matrix():
    head = np.arange(LANE_TILE) // HEAD_DIM
    return jnp.asarray((head[:, None] == head[None, :]).astype(np.float32), dtype=BF16)


def kernel(x_prompt, x_sample, c_prompt, c_sample, norm1_g, norm2_g, w_mod, b_mod, w_in, q_norm_g, k_norm_g,
           rel_bias, rpb, w_gate, b_gate, w_up_a, w_up_b, w_o, w_ff1, w_ff2):
    n_prompt = c_prompt.shape[0]
    mod_all = _modulation(jnp.concatenate([c_prompt, c_sample], axis=0), w_mod, b_mod)
    mod_all = mod_all.reshape(DEPTH, -1, N_MOD, D_MODEL)
    ones = _head_sum_matrix()
    qg = (q_norm_g.reshape(DEPTH, 1, QKV_PART) * (HEAD_DIM ** -0.5 * LOG2E)).astype(F32)
    kg = k_norm_g.reshape(DEPTH, 1, QKV_PART).astype(F32)
    n1 = norm1_g.reshape(DEPTH, 1, D_MODEL)
    n2 = norm2_g.reshape(DEPTH, 1, D_MODEL)
    bg = b_gate.reshape(DEPTH, 1, 2 * D_MODEL)
    w_in_b, w_gate_b = w_in.astype(BF16), w_gate.astype(BF16)
    w_up_a_b, w_up_b_b, w_o_b = w_up_a.astype(BF16), w_up_b.astype(BF16), w_o.astype(BF16)
    w_ff1_b, w_ff2_b = w_ff1.astype(BF16), w_ff2.astype(BF16)

    xs = (x_prompt, x_sample)
    bias_b = _bias_b(rpb)
    bias_a = {}
    for x in xs:
        for g, (w, d) in enumerate(DIL_GROUPS):
            n = x.shape[1] // d
            if (g, min(2 * QB, n)) not in bias_a:
                bias_a[(g, min(2 * QB, n))] = _bias_a(rel_bias, g, w, d, n)

    ys = []
    for x, mod_x in ((x_prompt, mod_all[:, :n_prompt]), (x_sample, mod_all[:, n_prompt:])):
        l_ = x.shape[1]
        for l in range(DEPTH):
            q, k, v = _pre_attention(x, mod_x, n1, w_in_b, qg, kg, ones, l)
            oas, lses = [], []
            for g, (w, d) in enumerate(DIL_GROUPS):
                o, lse = _dilated_group(q[g], k[g], v[g], bias_a[(g, min(2 * QB, l_ // d))], d)
                oas.append(o)
                lses.append(lse)
            ob = _neighborhood(q[N_GROUPS], k[N_GROUPS], v[N_GROUPS], bias_b, l)
            x = _post_attention(x, mod_x, n1, n2, oas, lses, ob, w_gate_b, bg, w_up_a_b, w_up_b_b, w_o_b,
                                w_ff1_b, w_ff2_b, l)
        ys.append(x)
    return tuple(ys)
```
